```python
import math
import jax
import jax.numpy as jnp
from jax import lax
import numpy as np


D_MODEL = 2048
BATCH = 4
SEQ = 4096
DEPTH = 4

GRID_W = 64
CTX_LEN = 256
N_MIXERS = 4
RMS_EPS = 1e-6
ROPE_THETA = 10000.0
Q_BLOCK = 128

GDN_QK_HEADS = D_MODEL // 128
GDN_V_HEADS = 2 * GDN_QK_HEADS
GDN_DK = 128
GDN_DV = 128
GDN_CONV = 5
GDN_CHUNK = 64
GDN_QK_W = GDN_QK_HEADS * GDN_DK
GDN_V_W = GDN_V_HEADS * GDN_DV
GDN_CONV_CH = 2 * GDN_QK_W + GDN_V_W
GDN_IN = GDN_CONV_CH + GDN_V_W + 4 * GDN_V_HEADS

GQA_DH = 128
GQA_HEADS = D_MODEL // GQA_DH
GQA_KV_HEADS = GQA_HEADS // 4
GQA_QW = GQA_HEADS * GQA_DH
GQA_KVW = GQA_KV_HEADS * GQA_DH
GQA_IN = 2 * GQA_QW + 2 * GQA_KVW

POOL_WINDOWS = (2, 4, 8, 16)
POOL_W = D_MODEL
POOL_GW = POOL_W // len(POOL_WINDOWS)

DIFF_DH = 64
DIFF_HEADS = D_MODEL // (2 * DIFF_DH)
DIFF_QW = DIFF_HEADS * 2 * DIFF_DH
DIFF_VW = DIFF_HEADS * 2 * DIFF_DH
DIFF_IN = 2 * DIFF_QW + 2 * DIFF_VW

kernel_name = 'hybrid_interleaved_diffusion_backbone'


def _n_layers_of(m):
    return len(range(m, DEPTH, N_MIXERS))


def _rmsnorm(x, g):
    xf = x.astype(jnp.float32)
    y = xf * lax.rsqrt(jnp.mean(xf * xf, axis=-1, keepdims=True) + RMS_EPS)
    return (y * g.astype(jnp.float32)).astype(x.dtype)


def _l2norm(x):
    xf = x.astype(jnp.float32)
    return (xf * lax.rsqrt(jnp.sum(xf * xf, axis=-1, keepdims=True) + 1e-6)).astype(x.dtype)


def _axial_rope_tables(n_tokens, head_dim):
    rows = n_tokens // GRID_W
    r = jnp.repeat(jnp.arange(rows, dtype=jnp.float32), GRID_W)
    col = jnp.tile(jnp.arange(GRID_W, dtype=jnp.float32), rows)
    d_axis = head_dim // 2
    inv = ROPE_THETA ** (-jnp.arange(0, d_axis, 2, dtype=jnp.float32) / d_axis)
    ang = jnp.concatenate([r[:, None] * inv, col[:, None] * inv], axis=-1)
    return jnp.cos(ang), jnp.sin(ang)


def _apply_rope(x, cos, sin):
    xf = x.astype(jnp.float32).reshape(*x.shape[:-1], -1, 2)
    x1, x2 = xf[..., 0], xf[..., 1]
    cs, sn = cos[None, :, None, :], sin[None, :, None, :]
    out = jnp.stack([x1 * cs - x2 * sn, x1 * sn + x2 * cs], axis=-1)
    return out.reshape(x.shape).astype(x.dtype)


def _sweep_blocks(fn, *qs):
    B, S = qs[0].shape[:2]
    nb = S // Q_BLOCK
    blocks = tuple(jnp.moveaxis(a.reshape(B, nb, Q_BLOCK, *a.shape[2:]), 1, 0) for a in qs)
    out = lax.map(lambda blk: fn(*blk), blocks)
    return jnp.moveaxis(out, 0, 1).reshape(B, S, *out.shape[3:])


def _dwconv_centred(x, w):
    K = w.shape[0]
    return lax.conv_general_dilated(x, w[:, None, :].astype(x.dtype), window_strides=(1,),
                                    padding=[(K // 2, K // 2)],
                                    dimension_numbers=('NWC', 'WIO', 'NWC'),
                                    feature_group_count=x.shape[-1])


def _gdn_project(h, w_in, conv_w, a_log, dt_bias):
    B, T, _ = h.shape
    p = jnp.dot(h, w_in)
    qkv, z, ab = jnp.split(p, [GDN_CONV_CH, GDN_CONV_CH + GDN_V_W], axis=-1)
    qkv = jax.nn.silu(_dwconv_centred(qkv, conv_w))
    q, k, v = jnp.split(qkv, [GDN_QK_W, 2 * GDN_QK_W], axis=-1)
    rep = GDN_V_HEADS // GDN_QK_HEADS
    q = jnp.repeat(_l2norm(q.reshape(B, T, GDN_QK_HEADS, GDN_DK)), rep, axis=2) * (GDN_DK ** -0.5)
    k = jnp.repeat(_l2norm(k.reshape(B, T, GDN_QK_HEADS, GDN_DK)), rep, axis=2)
    v = v.reshape(B, T, GDN_V_HEADS, GDN_DV)
    ab = ab.reshape(B, T, 2, 2, GDN_V_HEADS).astype(jnp.float32)
    g = -jnp.exp(a_log.astype(jnp.float32)) * jax.nn.softplus(ab[:, :, :, 0] + dt_bias.astype(jnp.float32))
    beta = jax.nn.sigmoid(ab[:, :, :, 1])
    return q, k, v, z, g, beta


def _gated_delta_chunked(q, k, v, g, beta, state):
    B, T, H, _ = k.shape
    dv = v.shape[-1]
    C = GDN_CHUNK
    N = T // C
    f32 = jnp.float32

    def chunks(a):
        return jnp.moveaxis(a.astype(f32).reshape(B, N, C, H, -1), 3, 1)

    q, k, v = chunks(q), chunks(k), chunks(v)
    g = jnp.cumsum(jnp.moveaxis(g.astype(f32).reshape(B, N, C, H), 3, 1), axis=-1)
    beta = jnp.moveaxis(beta.astype(f32).reshape(B, N, C, H), 3, 1)[..., None]
    idx = jnp.arange(C)
    incl = idx[:, None] >= idx[None, :]
    strict = idx[:, None] > idx[None, :]
    gdiff = g[..., :, None] - g[..., None, :]
    decay = jnp.where(incl, jnp.exp(jnp.where(incl, gdiff, 0.0)), 0.0)
    kb = k * beta
    lmat = jnp.where(strict, jnp.einsum('bhnid,bhnjd->bhnij', kb, k) * decay, 0.0)
    eye = jnp.eye(C, dtype=f32)
    tinv = lax.linalg.triangular_solve(eye + lmat, jnp.broadcast_to(eye, lmat.shape),
                                       left_side=True, lower=True, unit_diagonal=True)
    u = jnp.einsum('bhnij,bhnje->bhnie', tinv, v * beta)
    w = jnp.einsum('bhnij,bhnjd->bhnid', tinv, kb * jnp.exp(g)[..., None])
    qk = jnp.where(incl, jnp.einsum('bhnid,bhnjd->bhnij', q, k) * decay, 0.0)
    q_dec = q * jnp.exp(g)[..., None]
    k_dec = k * jnp.exp(g[..., -1:] - g)[..., None]
    g_tot = jnp.exp(g[..., -1])
    xs = tuple(jnp.moveaxis(a, 2, 0) for a in (qk, q_dec, k_dec, u, w, g_tot))

    def step(S, inp):
        qk_n, qd_n, kd_n, u_n, w_n, gt_n = inp
        v_new = u_n - jnp.einsum('bhcd,bhde->bhce', w_n, S)
        o = jnp.einsum('bhcd,bhde->bhce', qd_n, S) + jnp.einsum('bhij,bhje->bhie', qk_n, v_new)
        S = S * gt_n[..., None, None] + jnp.einsum('bhcd,bhce->bhde', kd_n, v_new)
        return S, o

    S, o = lax.scan(step, state, xs)
    o = jnp.transpose(o, (1, 0, 3, 2, 4)).reshape(B, T, H, dv)
    return o, S


def _gdn_direction(q, k, v, g, beta, state0, reverse):
    if reverse:
        q, k, v, g, beta = (jnp.flip(a, 1) for a in (q, k, v, g, beta))
    o, s = _gated_delta_chunked(q, k, v, g, beta, state0)
    return (jnp.flip(o, 1) if reverse else o), s


def _gdn_mixer(h_c, h_l, w_in, conv_w, a_log, dt_bias, out_norm, w_out, need_ctx):
    qc, kc, vc, zc, gc, bc = _gdn_project(h_c, w_in, conv_w, a_log, dt_bias)
    ql, kl, vl, zl, gl, bl = _gdn_project(h_l, w_in, conv_w, a_log, dt_bias)
    B = h_l.shape[0]
    s0 = jnp.zeros((B, GDN_V_HEADS, GDN_DK, GDN_DV), jnp.float32)
    oc_f, sc_f = _gdn_direction(qc, kc, vc, gc[:, :, 0], bc[:, :, 0], s0, False)
    oc_b, sc_b = _gdn_direction(qc, kc, vc, gc[:, :, 1], bc[:, :, 1], s0, True)
    ol_f, _ = _gdn_direction(ql, kl, vl, gl[:, :, 0], bl[:, :, 0], sc_f, False)
    ol_b, _ = _gdn_direction(ql, kl, vl, gl[:, :, 1], bl[:, :, 1], sc_b, True)

    def finish(o, z):
        Bz, T = z.shape[:2]
        y = _rmsnorm(o.astype(z.dtype), out_norm) * jax.nn.silu(z.reshape(Bz, T, GDN_V_HEADS, GDN_DV))
        return jnp.dot(y.reshape(Bz, T, GDN_V_W), w_out)

    o_l = finish(ol_f + ol_b, zl)
    o_c = finish(oc_f + oc_b, zc) if need_ctx else None
    return o_c, o_l


def _gqa_attend(q, k, v):
    p = jax.nn.softmax(jnp.einsum('bqhgd,bkhd->bhgqk', q, k, preferred_element_type=jnp.float32), axis=-1)
    return jnp.einsum('bhgqk,bkhd->bqhgd', p.astype(v.dtype), v)


def _gqa_mixer(h_c, h_l, w_in, q_norm, k_norm, w_out, cos, sin, need_ctx):
    B, S, _ = h_l.shape
    Bc, L = h_c.shape[:2]
    G = GQA_HEADS // GQA_KV_HEADS
    scale = GQA_DH ** -0.5
    cuts = [GQA_QW, GQA_QW + GQA_KVW, GQA_QW + 2 * GQA_KVW]
    q_l, k_l, v_l, z_l = jnp.split(jnp.dot(h_l, w_in), cuts, axis=-1)
    q_l = _apply_rope(_rmsnorm(q_l.reshape(B, S, GQA_HEADS, GQA_DH), q_norm), cos, sin) * scale
    k_l = _apply_rope(_rmsnorm(k_l.reshape(B, S, GQA_KV_HEADS, GQA_DH), k_norm), cos, sin)
    v_l = v_l.reshape(B, S, GQA_KV_HEADS, GQA_DH)
    if need_ctx:
        q_c, k_c, v_c, z_c = jnp.split(jnp.dot(h_c, w_in), cuts, axis=-1)
    else:
        k_c, v_c = jnp.split(jnp.dot(h_c, w_in[:, GQA_QW:GQA_QW + 2 * GQA_KVW]), [GQA_KVW], axis=-1)
    k_c = _rmsnorm(k_c.reshape(Bc, L, GQA_KV_HEADS, GQA_DH), k_norm)
    v_c = v_c.reshape(Bc, L, GQA_KV_HEADS, GQA_DH)
    k_all = jnp.concatenate([k_c, k_l], axis=1)
    v_all = jnp.concatenate([v_c, v_l], axis=1)
    o = _sweep_blocks(lambda qb: _gqa_attend(qb, k_all, v_all),
                      q_l.reshape(B, S, GQA_KV_HEADS, G, GQA_DH))
    o_l = jnp.dot(o.reshape(B, S, GQA_QW) * jax.nn.silu(z_l), w_out)
    o_c = None
    if need_ctx:
        q_c = _rmsnorm(q_c.reshape(Bc, L, GQA_HEADS, GQA_DH), q_norm) * scale
        oc = _gqa_attend(q_c.reshape(Bc, L, GQA_KV_HEADS, G, GQA_DH), k_c, v_c)
        o_c = jnp.dot(oc.reshape(Bc, L, GQA_QW) * jax.nn.silu(z_c), w_out)
    return o_c, o_l


def _multiscale_pool(u):
    B, T, E = u.shape
    uf = u.astype(jnp.float32)
    csum = jnp.concatenate([jnp.zeros((B, 1, E), jnp.float32), jnp.cumsum(uf, axis=1)], axis=1)
    t = jnp.arange(T)
    means = []
    for gi, w in enumerate(POOL_WINDOWS):
        lo = jnp.clip(t - w // 2, 0, T)
        hi = jnp.clip(t - w // 2 + w, 0, T)
        cg = csum[..., gi * POOL_GW:(gi + 1) * POOL_GW]
        means.append((cg[:, hi] - cg[:, lo]) / (hi - lo).astype(jnp.float32)[None, :, None])
    return (jnp.concatenate(means, axis=-1) - uf).astype(u.dtype)


def _pool_mixer(h_c, h_l, w_in, w_grp, ch_scale, w_out, need_ctx):
    def branch(h):
        B, T, _ = h.shape
        u, z = jnp.split(jnp.dot(h, w_in), 2, axis=-1)
        d = _multiscale_pool(u).reshape(B, T, len(POOL_WINDOWS), POOL_GW)
        r = jnp.einsum('btgc,gcd->btgd', d, w_grp).reshape(B, T, POOL_W) * ch_scale
        return jnp.dot(r * jax.nn.silu(z), w_out)
    o_l = branch(h_l)
    o_c = branch(h_c) if need_ctx else None
    return o_c, o_l


def _diff_attend(q, k, v, lam):
    p = jax.nn.softmax(jnp.einsum('bqhpd,bkhpd->bhpqk', q, k, preferred_element_type=jnp.float32), axis=-1)
    w = (p[:, :, 0] - lam * p[:, :, 1]).astype(v.dtype)
    return jnp.einsum('bhqk,bkhe->bqhe', w, v)


def _diff_mixer(h_c, h_l, w_in, q_norm, k_norm, lq1, lk1, lq2, lk2, sub_norm, w_out,
                cos, sin, lam_init, need_ctx):
    H, d = DIFF_HEADS, DIFF_DH
    B, S, _ = h_l.shape
    Bc, L = h_c.shape[:2]
    lam = (jnp.exp(jnp.sum(lq1.astype(jnp.float32) * lk1.astype(jnp.float32)))
           - jnp.exp(jnp.sum(lq2.astype(jnp.float32) * lk2.astype(jnp.float32))) + lam_init)

    def qk_heads(a, g, rope):
        Ba, T, _ = a.shape
        a = _rmsnorm(a.reshape(Ba, T, 2 * H, d), g)
        if rope:
            a = _apply_rope(a, cos, sin)
        return a.reshape(Ba, T, H, 2, d)

    def finish(o, z):
        Bo, T = z.shape[:2]
        y = _rmsnorm(o, sub_norm) * (1.0 - lam_init)
        return jnp.dot(y.reshape(Bo, T, DIFF_VW) * jax.nn.silu(z), w_out)

    cuts = [DIFF_QW, 2 * DIFF_QW, 2 * DIFF_QW + DIFF_VW]
    q_l, k_l, v_l, z_l = jnp.split(jnp.dot(h_l, w_in), cuts, axis=-1)
    if need_ctx:
        q_c, k_c, v_c, z_c = jnp.split(jnp.dot(h_c, w_in), cuts, axis=-1)
    else:
        k_c, v_c = jnp.split(jnp.dot(h_c, w_in[:, DIFF_QW:2 * DIFF_QW + DIFF_VW]), [DIFF_QW], axis=-1)
    scale = d ** -0.5
    q_l = qk_heads(q_l, q_norm, True) * scale
    k_c = qk_heads(k_c, k_norm, False)
    v_c = v_c.reshape(Bc, L, H, 2 * d)
    k_all = jnp.concatenate([k_c, qk_heads(k_l, k_norm, True)], axis=1)
    v_all = jnp.concatenate([v_c, v_l.reshape(B, S, H, 2 * d)], axis=1)
    o = _sweep_blocks(lambda qb: _diff_attend(qb, k_all, v_all, lam), q_l)
    o_l = finish(o, z_l)
    o_c = None
    if need_ctx:
        q_c = qk_heads(q_c, q_norm, False) * scale
        o_c = finish(_diff_attend(q_c, k_c, v_c, lam), z_c)
    return o_c, o_l


def setup_inputs(seed: int = 0) -> dict:
    key = jax.random.key(seed)
    ks = iter(jax.random.split(key, 40))
    f32 = jnp.float32
    D = D_MODEL

    def nrm(shape, scale):
        return jax.random.normal(next(ks), shape, f32) * scale

    def gain(shape):
        return 1.0 + nrm(shape, 0.1)

    nA, nB, nC, nD = (_n_layers_of(m) for m in range(N_MIXERS))
    gdn_a_log = jnp.log(jax.random.uniform(next(ks), (nA, 2, GDN_V_HEADS), f32, 1.0, 16.0))
    dt = jnp.exp(jax.random.uniform(next(ks), (nA, 2, GDN_V_HEADS), f32,
                                    math.log(1e-3), math.log(1e-1)))
    gdn_dt_bias = dt + jnp.log(-jnp.expm1(-dt))
    return {
        'x': nrm((BATCH, SEQ, D), 1.0),
        'c': nrm((BATCH, D), 1.0),
        'ctx': nrm((BATCH, CTX_LEN, D), 1.0),
        'c_ctx': nrm((D,), 1.0),
        'norm_g': gain((DEPTH, D)),
        'mod_w': nrm((DEPTH, D, 3 * D), 0.5 * D ** -0.5),
        'mod_b': nrm((DEPTH, 3 * D), 0.01),
        'gdn_w_in': nrm((nA, D, GDN_IN), D ** -0.5),
        'gdn_conv_w': nrm((nA, GDN_CONV, GDN_CONV_CH), GDN_CONV ** -0.5),
        'gdn_a_log': gdn_a_log,
        'gdn_dt_bias': gdn_dt_bias,
        'gdn_out_norm': gain((nA, GDN_DV)),
        'gdn_w_out': nrm((nA, GDN_V_W, D), GDN_V_W ** -0.5),
        'gqa_w_in': nrm((nB, D, GQA_IN), D ** -0.5),
        'gqa_q_norm': gain((nB, GQA_DH)),
        'gqa_k_norm': gain((nB, GQA_DH)),
        'gqa_w_out': nrm((nB, GQA_QW, D), GQA_QW ** -0.5),
        'pool_w_in': nrm((nC, D, 2 * POOL_W), D ** -0.5),
        'pool_w_grp': nrm((nC, len(POOL_WINDOWS), POOL_GW, POOL_GW), POOL_GW ** -0.5),
        'pool_scale': gain((nC, POOL_W)),
        'pool_w_out': nrm((nC, POOL_W, D), POOL_W ** -0.5),
        'diff_w_in': nrm((nD, D, DIFF_IN), D ** -0.5),
        'diff_q_norm': gain((nD, DIFF_DH)),
        'diff_k_norm': gain((nD, DIFF_DH)),
        'diff_lambda_q1': nrm((nD, DIFF_DH), 0.1),
        'diff_lambda_k1': nrm((nD, DIFF_DH), 0.1),
        'diff_lambda_q2': nrm((nD, DIFF_DH), 0.1),
        'diff_lambda_k2': nrm((nD, DIFF_DH), 0.1),
        'diff_sub_norm': gain((nD, 2 * DIFF_DH)),
        'diff_w_out': nrm((nD, DIFF_VW, D), DIFF_VW ** -0.5),
    }


def reference(x, c, ctx, c_ctx, norm_g, mod_w, mod_b,
              gdn_w_in, gdn_conv_w, gdn_a_log, gdn_dt_bias, gdn_out_norm, gdn_w_out,
              gqa_w_in, gqa_q_norm, gqa_k_norm, gqa_w_out,
              pool_w_in, pool_w_grp, pool_scale, pool_w_out,
              diff_w_in, diff_q_norm, diff_k_norm, diff_lambda_q1, diff_lambda_k1,
              diff_lambda_q2, diff_lambda_k2, diff_sub_norm, diff_w_out):
    n_lat = x.shape[1]
    cos_gqa, sin_gqa = _axial_rope_tables(n_lat, GQA_DH)
    cos_diff, sin_diff = _axial_rope_tables(n_lat, DIFF_DH)
    lat, cx = x, ctx
    for i in range(DEPTH):
        m, j = i % N_MIXERS, i // N_MIXERS
        need_ctx = i < DEPTH - 1
        mod_l = jnp.dot(jax.nn.silu(c), mod_w[i]) + mod_b[i]
        shift_l, scale_l, gate_l = jnp.split(mod_l[:, None, :], 3, axis=-1)
        mod_c = jnp.dot(jax.nn.silu(c_ctx), mod_w[i]) + mod_b[i]
        shift_c, scale_c, gate_c = jnp.split(mod_c, 3, axis=-1)
        h_l = _rmsnorm(lat, norm_g[i]) * (1.0 + scale_l) + shift_l
        h_c = _rmsnorm(cx, norm_g[i]) * (1.0 + scale_c) + shift_c
        if m == 0:
            o_c, o_l = _gdn_mixer(h_c, h_l, gdn_w_in[j], gdn_conv_w[j], gdn_a_log[j], gdn_dt_bias[j],
                                  gdn_out_norm[j], gdn_w_out[j], need_ctx)
        elif m == 1:
            o_c, o_l = _gqa_mixer(h_c, h_l, gqa_w_in[j], gqa_q_norm[j], gqa_k_norm[j], gqa_w_out[j],
                                  cos_gqa, sin_gqa, need_ctx)
        elif m == 2:
            o_c, o_l = _pool_mixer(h_c, h_l, pool_w_in[j], pool_w_grp[j], pool_scale[j], pool_w_out[j],
                                   need_ctx)
        else:
            lam_init = 0.8 - 0.6 * math.exp(-0.3 * i)
            o_c, o_l = _diff_mixer(h_c, h_l, diff_w_in[j], diff_q_norm[j], diff_k_norm[j],
                                   diff_lambda_q1[j], diff_lambda_k1[j], diff_lambda_q2[j],
                                   diff_lambda_k2[j], diff_sub_norm[j], diff_w_out[j],
                                   cos_diff, sin_diff, lam_init, need_ctx)
        lat = lat + gate_l * o_l
        if need_ctx:
            cx = cx + gate_c * o_c
    return lat
```

```python
import functools
import math

import jax
import jax.numpy as jnp
from jax import lax
from jax.experimental import pallas as pl
from jax.experimental.pallas import tpu as pltpu

F32 = jnp.float32
BF16 = jnp.bfloat16

D_MODEL = 2048
N_MIXERS = 4
RMS_EPS = 1e-6
L2_EPS = 1e-6
ROPE_THETA = 10000.0
GRID_W = 64

HEAD = 128
GDN_QK_HEADS = D_MODEL // HEAD
GDN_V_HEADS = 2 * GDN_QK_HEADS
GDN_CONV = 5
GDN_CHUNK = 64
GDN_QK_W = GDN_QK_HEADS * HEAD
GDN_V_W = GDN_V_HEADS * HEAD
GDN_CONV_CH = 2 * GDN_QK_W + GDN_V_W

GQA_HEADS = D_MODEL // HEAD
GQA_KV_HEADS = GQA_HEADS // 4
GQA_QW = GQA_HEADS * HEAD
GQA_KVW = GQA_KV_HEADS * HEAD

POOL_WINDOWS = (2, 4, 8, 16)
POOL_GW = D_MODEL // len(POOL_WINDOWS)
POOL_HALO = 16

DIFF_DH = 64
DIFF_HEADS = D_MODEL // (2 * DIFF_DH)
DIFF_W = DIFF_HEADS * 2 * DIFF_DH

VMEM_LIMIT_BYTES = 56 * 1024 * 1024
LANES = 128
BF16_SUBLANES = 16
CONV_HALO = 16


def _params(sem):
    return pltpu.CompilerParams(dimension_semantics=sem, vmem_limit_bytes=VMEM_LIMIT_BYTES)


def _tile(n, pref):
    t = min(n, pref)
    while n % t or t % BF16_SUBLANES:
        t -= 1
    return t


def _silu(x):
    return x * jax.nn.sigmoid(x)


def _softplus(x):
    return jnp.maximum(x, 0.0) + jnp.log(1.0 + jnp.exp(-jnp.abs(x)))


def _dot(a, b):
    return jnp.dot(a, b, preferred_element_type=F32)


def _dot_nt(a, b):
    return lax.dot_general(a, b, (((1,), (1,)), ((), ())), preferred_element_type=F32)


def _swap_pairs(x):
    lane = lax.broadcasted_iota(jnp.int32, x.shape, 1)
    return jnp.where(lane % 2 == 0, pltpu.roll(x, LANES - 1, 1), pltpu.roll(x, 1, 1))


def _group_mean_sq(x, group):
    x2 = x * x
    if group == LANES:
        return jnp.mean(x2, axis=-1, keepdims=True)
    lo = lax.broadcasted_iota(jnp.int32, x.shape, 1) < group
    s_lo = jnp.sum(jnp.where(lo, x2, 0.0), axis=-1, keepdims=True)
    s_hi = jnp.sum(jnp.where(lo, 0.0, x2), axis=-1, keepdims=True)
    return jnp.where(lo, s_lo, s_hi) * (1.0 / group)


def _mod_kernel(c_ref, w_ref, b_ref, o_ref):
    a = _silu(c_ref[...]).astype(BF16)
    o_ref[0] = _dot(a, w_ref[0].astype(BF16)) + b_ref[0]


def _mod_call(c_all, mod_w, mod_b):
    depth, d, n = mod_w.shape
    rows = c_all.shape[0]
    tn = _tile(n, 768)
    return pl.pallas_call(
        _mod_kernel,
        grid=(depth, n // tn),
        in_specs=[
            pl.BlockSpec((rows, d), lambda l, j: (0, 0)),
            pl.BlockSpec((1, d, tn), lambda l, j: (l, 0, j)),
            pl.BlockSpec((1, 1, tn), lambda l, j: (l, 0, j)),
        ],
        out_specs=pl.BlockSpec((1, rows, tn), lambda l, j: (l, 0, j)),
        out_shape=jax.ShapeDtypeStruct((depth, rows, n), F32),
        compiler_params=_params(("arbitrary", "arbitrary")),
        name="mod_vectors",
    )(c_all, mod_w, mod_b.reshape(depth, 1, n))


def _head_epilogue(acc, o_ref, gain, cs, scale, group):
    for c in range(acc.shape[1] // LANES):
        xc = acc[:, c * LANES:(c + 1) * LANES]
        y = xc * lax.rsqrt(_group_mean_sq(xc, group) + RMS_EPS) * gain
        if cs is not None:
            y = y * cs[0] + _swap_pairs(y) * cs[1]
        if scale != 1.0:
            y = y * scale
        o_ref[:, c * LANES:(c + 1) * LANES] = y.astype(o_ref.dtype)


def _proj_kernel(*refs, kind, rope, j0, q_tiles, k_tiles, group, q_scale):
    x_ref, g_ref, sh_ref, sc_ref, w_ref = refs[:5]
    rest = list(refs[5:])
    h_ref = rest.pop()
    qn_ref = kn_ref = c_ref = s_ref = wab_ref = ab_ref = None
    if kind == "heads":
        qn_ref, kn_ref = rest.pop(0), rest.pop(0)
        if rope:
            c_ref, s_ref = rest.pop(0), rest.pop(0)
    if kind == "gdn":
        wab_ref = rest.pop(0)
        o_ref, ab_ref = rest
    else:
        (o_ref,) = rest
    j = pl.program_id(1)

    @pl.when(j == 0)
    def _():
        x = x_ref[...]
        ms = jnp.mean(x * x, axis=-1, keepdims=True)
        y = x * lax.rsqrt(ms + RMS_EPS) * g_ref[...]
        h = y * (1.0 + sc_ref[0]) + sh_ref[0]
        h_ref[...] = h.astype(BF16)
        if kind == "gdn":
            ab_ref[...] = _dot(h_ref[...], wab_ref[...])

    acc = _dot(h_ref[...], w_ref[...])
    if kind != "heads":
        o_ref[...] = acc.astype(o_ref.dtype)
        return
    jg = j + j0
    cs = (c_ref[...], s_ref[...]) if rope else None

    @pl.when(jg < q_tiles)
    def _():
        _head_epilogue(acc, o_ref, qn_ref[...], cs, q_scale, group)

    @pl.when(jnp.logical_and(jg >= q_tiles, jg < q_tiles + k_tiles))
    def _():
        _head_epilogue(acc, o_ref, kn_ref[...], cs, 1.0, group)

    @pl.when(jg >= q_tiles + k_tiles)
    def _():
        o_ref[...] = acc.astype(o_ref.dtype)


def _proj(xs, mod3, mod_base, batch_rows, g_row, w, *, kind="plain", heads=None, rope_tabs=None,
          wab=None, j0=0, nj=None, tn=512, tm_pref=512):
    m, d = xs.shape
    seq = batch_rows if batch_rows is not None else m
    tm = _tile(seq if rope_tabs is not None or batch_rows is not None else m, tm_pref)
    n_total = w.shape[1] // tn
    nj = n_total - j0 if nj is None else nj
    tiles_per_batch = (batch_rows // tm) if batch_rows is not None else None

    def mod_idx(which):
        if tiles_per_batch is None:
            return lambda i, j: (mod_base * 3 + which, 0, 0)
        return lambda i, j: ((mod_base + i // tiles_per_batch) * 3 + which, 0, 0)

    in_specs = [
        pl.BlockSpec((tm, d), lambda i, j: (i, 0)),
        pl.BlockSpec((1, d), lambda i, j: (0, 0)),
        pl.BlockSpec((1, 1, d), mod_idx(0)),
        pl.BlockSpec((1, 1, d), mod_idx(1)),
        pl.BlockSpec((d, tn), lambda i, j: (0, j + j0)),
    ]
    args = [xs, g_row, mod3, mod3, w]
    kw = dict(kind=kind, rope=rope_tabs is not None, j0=j0, q_tiles=0, k_tiles=0, group=LANES, q_scale=1.0)
    if kind == "heads":
        qn, kn, q_cols, k_cols, group, q_scale = heads
        kw.update(q_tiles=q_cols // tn, k_tiles=k_cols // tn, group=group, q_scale=q_scale)
        in_specs += [pl.BlockSpec((1, LANES), lambda i, j: (0, 0))] * 2
        args += [qn, kn]
        if rope_tabs is not None:
            tps = seq // tm
            in_specs += [pl.BlockSpec((tm, LANES), lambda i, j: (i % tps, 0))] * 2
            args += list(rope_tabs)
    out_specs = pl.BlockSpec((tm, tn), lambda i, j: (i, j))
    out_shape = jax.ShapeDtypeStruct((m, nj * tn), BF16)
    if kind == "gdn":
        in_specs.append(pl.BlockSpec((d, LANES), lambda i, j: (0, 0)))
        args.append(wab)
        out_specs = [out_specs, pl.BlockSpec((tm, LANES), lambda i, j: (i, 0))]
        out_shape = [out_shape, jax.ShapeDtypeStruct((m, LANES), F32)]
    return pl.pallas_call(
        functools.partial(_proj_kernel, **kw),
        grid=(m // tm, nj),
        in_specs=in_specs,
        out_specs=out_specs,
        out_shape=out_shape,
        scratch_shapes=[pltpu.VMEM((tm, d), BF16)],
        compiler_params=_params(("arbitrary", "arbitrary")),
        name="proj_" + kind,
    )(*args)


def _out_kernel(*refs, gdn):
    if gdn:
        o_ref, z_ref, n_ref, w_ref, x_ref, gate_ref, out_ref, y_ref = refs

        @pl.when(pl.program_id(1) == 0)
        def _():
            gain = n_ref[...]
            for c in range(o_ref.shape[1] // LANES):
                sl = slice(c * LANES, (c + 1) * LANES)
                o = o_ref[:, sl]
                y = o * lax.rsqrt(jnp.mean(o * o, axis=-1, keepdims=True) + RMS_EPS) * gain
                y_ref[:, sl] = (y * _silu(z_ref[:, sl].astype(F32))).astype(BF16)
    else:
        y_ref, w_ref, x_ref, gate_ref, out_ref = refs
    out_ref[...] = x_ref[...] + gate_ref[0] * _dot(y_ref[...], w_ref[...])


def _out_proj(y, w, xs, mod3, mod_base, batch_rows, *, gdn_zp=None, gdn_norm=None, tn=512, tm_pref=512):
    m, d = xs.shape
    kdim = w.shape[0]
    tm = _tile(batch_rows if batch_rows is not None else m, tm_pref)
    tiles_per_batch = (batch_rows // tm) if batch_rows is not None else None
    if tiles_per_batch is None:
        gate_idx = lambda i, j: (mod_base * 3 + 2, 0, 0)
    else:
        gate_idx = lambda i, j: ((mod_base + i // tiles_per_batch) * 3 + 2, 0, 0)
    in_specs = [pl.BlockSpec((tm, kdim), lambda i, j: (i, 0))]
    args = [y]
    scratch = []
    if gdn_zp is not None:
        z_blk = GDN_CONV_CH // kdim
        in_specs += [pl.BlockSpec((tm, kdim), lambda i, j: (i, z_blk)),
                     pl.BlockSpec((1, LANES), lambda i, j: (0, 0))]
        args += [gdn_zp, gdn_norm]
        scratch = [pltpu.VMEM((tm, kdim), BF16)]
    in_specs += [
        pl.BlockSpec((kdim, tn), lambda i, j: (0, j)),
        pl.BlockSpec((tm, tn), lambda i, j: (i, j)),
        pl.BlockSpec((1, 1, tn), lambda i, j: gate_idx(i, j)[:2] + (j,)),
    ]
    args += [w, xs, mod3]
    return pl.pallas_call(
        functools.partial(_out_kernel, gdn=gdn_zp is not None),
        grid=(m // tm, d // tn),
        in_specs=in_specs,
        out_specs=pl.BlockSpec((tm, tn), lambda i, j: (i, j)),
        out_shape=jax.ShapeDtypeStruct((m, d), F32),
        scratch_shapes=scratch,
        compiler_params=_params(("arbitrary", "arbitrary")),
        name="out_proj_gdn" if gdn_zp is not None else "out_proj",
    )(*args)


def _softmax_parts(q, keys):
    s = [_dot_nt(q, k) for k in keys]
    m = functools.reduce(jnp.maximum, [jnp.max(x, axis=-1, keepdims=True) for x in s])
    p = [jnp.exp(x - m) for x in s]
    l = functools.reduce(jnp.add, [jnp.sum(x, axis=-1, keepdims=True) for x in p])
    return p, 1.0 / l


def _gqa_attn_kernel(*refs, has_lat):
    if has_lat:
        q_ref, z_ref, kc_ref, vc_ref, kl_ref, vl_ref, y_ref = refs
        keys, vals = [kc_ref[...], kl_ref[...]], [vc_ref[...], vl_ref[...]]
    else:
        q_ref, z_ref, kc_ref, vc_ref, y_ref = refs
        keys, vals = [kc_ref[...]], [vc_ref[...]]
    for g in range(q_ref.shape[1] // HEAD):
        sl = slice(g * HEAD, (g + 1) * HEAD)
        p, inv_l = _softmax_parts(q_ref[:, sl], keys)
        o = functools.reduce(jnp.add, [_dot(pi.astype(BF16), vi) for pi, vi in zip(p, vals)]) * inv_l
        y_ref[:, sl] = (o * _silu(z_ref[:, sl].astype(F32))).astype(BF16)


def _gqa_attention(p_q, p_c, p_l, batch, *, tq_pref=256):
    has_lat = p_l is not None
    m = p_q.shape[0]
    rows_q = m // batch
    tq = _tile(rows_q, tq_pref)
    nq = rows_q // tq
    lc = p_c.shape[0] // batch
    grp = GQA_QW // GQA_KV_HEADS
    k_blk, v_blk, z_blk = GQA_QW // HEAD, (GQA_QW + GQA_KVW) // HEAD, (GQA_QW + 2 * GQA_KVW) // grp
    in_specs = [
        pl.BlockSpec((tq, grp), lambda b, h, i: (b * nq + i, h)),
        pl.BlockSpec((tq, grp), lambda b, h, i: (b * nq + i, z_blk + h)),
        pl.BlockSpec((lc, HEAD), lambda b, h, i: (b, k_blk + h)),
        pl.BlockSpec((lc, HEAD), lambda b, h, i: (b, v_blk + h)),
    ]
    args = [p_q, p_q, p_c, p_c]
    if has_lat:
        ls = p_l.shape[0] // batch
        in_specs += [pl.BlockSpec((ls, HEAD), lambda b, h, i: (b, k_blk + h)),
                     pl.BlockSpec((ls, HEAD), lambda b, h, i: (b, v_blk + h))]
        args += [p_l, p_l]
    return pl.pallas_call(
        functools.partial(_gqa_attn_kernel, has_lat=has_lat),
        grid=(batch, GQA_KV_HEADS, nq),
        in_specs=in_specs,
        out_specs=pl.BlockSpec((tq, grp), lambda b, h, i: (b * nq + i, h)),
        out_shape=jax.ShapeDtypeStruct((m, GQA_QW), BF16),
        compiler_params=_params(("arbitrary", "arbitrary", "arbitrary")),
        name="gqa_attention",
    )(*args)


def _diff_attn_kernel(lq1_ref, lk1_ref, lq2_ref, lk2_ref, n_ref, q_ref, z_ref, kc_ref, vc_ref, kl_ref, vl_ref,
                      y_ref, *, lam_init):
    lam = (jnp.exp(jnp.sum(lq1_ref[...] * lk1_ref[...], axis=-1, keepdims=True))
           - jnp.exp(jnp.sum(lq2_ref[...] * lk2_ref[...], axis=-1, keepdims=True)) + lam_init)
    q = q_ref[...]
    lo = lax.broadcasted_iota(jnp.int32, q.shape, 1) < DIFF_DH
    zero = jnp.zeros_like(q)
    keys, vals = [kc_ref[...], kl_ref[...]], [vc_ref[...], vl_ref[...]]
    p0, inv0 = _softmax_parts(jnp.where(lo, q, zero), keys)
    p1, inv1 = _softmax_parts(jnp.where(lo, zero, q), keys)
    o = functools.reduce(jnp.add, [
        _dot((a * inv0 - lam * (b * inv1)).astype(BF16), v) for a, b, v in zip(p0, p1, vals)])
    y = o * lax.rsqrt(jnp.mean(o * o, axis=-1, keepdims=True) + RMS_EPS) * n_ref[...] * (1.0 - lam_init)
    y_ref[...] = (y * _silu(z_ref[...].astype(F32))).astype(BF16)


def _diff_attention(p_l, p_c, lams, sub_norm, batch, lam_init, *, tq_pref=256):
    m = p_l.shape[0]
    ls = m // batch
    lc = p_c.shape[0] // batch
    tq = _tile(ls, tq_pref)
    nq = ls // tq
    nh = DIFF_HEADS
    small = pl.BlockSpec((1, DIFF_DH), lambda b, h, i: (0, 0))
    in_specs = [small] * 4 + [
        pl.BlockSpec((1, HEAD), lambda b, h, i: (0, 0)),
        pl.BlockSpec((tq, HEAD), lambda b, h, i: (b * nq + i, h)),
        pl.BlockSpec((tq, HEAD), lambda b, h, i: (b * nq + i, 3 * nh + h)),
        pl.BlockSpec((lc, HEAD), lambda b, h, i: (b, h)),
        pl.BlockSpec((lc, HEAD), lambda b, h, i: (b, nh + h)),
        pl.BlockSpec((ls, HEAD), lambda b, h, i: (b, nh + h)),
        pl.BlockSpec((ls, HEAD), lambda b, h, i: (b, 2 * nh + h)),
    ]
    return pl.pallas_call(
        functools.partial(_diff_attn_kernel, lam_init=lam_init),
        grid=(batch, nh, nq),
        in_specs=in_specs,
        out_specs=pl.BlockSpec((tq, HEAD), lambda b, h, i: (b * nq + i, h)),
        out_shape=jax.ShapeDtypeStruct((m, DIFF_W), BF16),
        compiler_params=_params(("arbitrary", "arbitrary", "arbitrary")),
        name="diff_attention",
    )(*lams, sub_norm, p_l, p_l, p_c, p_c, p_l, p_l)


def _pool_kernel(u_ref, up_ref, un_ref, z_ref, wg_ref, sc_ref, y_ref, ext_ref, *, seq):
    tm = u_ref.shape[0]
    tps = seq // tm
    pos = pl.program_id(0) % tps
    has_prev = (pos != 0).astype(F32)
    has_next = (pos != tps - 1).astype(F32)
    ext_ref[0:POOL_HALO, :] = up_ref[...].astype(F32) * has_prev
    ext_ref[POOL_HALO:POOL_HALO + tm, :] = u_ref[...].astype(F32)
    ext_ref[POOL_HALO + tm:, :] = un_ref[...].astype(F32) * has_next
    t = pos * tm + lax.broadcasted_iota(jnp.int32, (tm, 1), 0)
    for gi, w in enumerate(POOL_WINDOWS):
        cols = slice(gi * POOL_GW, (gi + 1) * POOL_GW)
        tot = ext_ref[POOL_HALO - w // 2:POOL_HALO - w // 2 + tm, cols]
        for k in range(1, w):
            tot = tot + ext_ref[POOL_HALO - w // 2 + k:POOL_HALO - w // 2 + k + tm, cols]
        cnt = jnp.minimum(t - w // 2 + w, seq) - jnp.maximum(t - w // 2, 0)
        dgrp = tot / cnt.astype(F32) - ext_ref[POOL_HALO:POOL_HALO + tm, cols]
        r = _dot(dgrp.astype(BF16), wg_ref[gi]) * sc_ref[:, cols]
        y_ref[:, cols] = (r * _silu(z_ref[:, cols].astype(F32))).astype(BF16)


def _pool_mix(p, w_grp, ch_scale, seq):
    m = p.shape[0]
    w_u = D_MODEL
    tm = _tile(seq, 256)
    hb = tm // POOL_HALO
    last = m // POOL_HALO - 1
    return pl.pallas_call(
        functools.partial(_pool_kernel, seq=seq),
        grid=(m // tm,),
        in_specs=[
            pl.BlockSpec((tm, w_u), lambda i: (i, 0)),
            pl.BlockSpec((POOL_HALO, w_u), lambda i: (jnp.maximum(i * hb - 1, 0), 0)),
            pl.BlockSpec((POOL_HALO, w_u), lambda i: (jnp.minimum((i + 1) * hb, last), 0)),
            pl.BlockSpec((tm, w_u), lambda i: (i, 1)),
            pl.BlockSpec(w_grp.shape, lambda i: (0, 0, 0)),
            pl.BlockSpec((1, w_u), lambda i: (0, 0)),
        ],
        out_specs=pl.BlockSpec((tm, w_u), lambda i: (i, 0)),
        out_shape=jax.ShapeDtypeStruct((m, w_u), BF16),
        scratch_shapes=[pltpu.VMEM((tm + 2 * POOL_HALO, w_u), F32)],
        compiler_params=_params(("arbitrary",)),
        name="pool_mix",
    )(p, p, p, p, w_grp, ch_scale)


def _split3(x):
    hi = x.astype(BF16)
    r1 = x - hi.astype(F32)
    mid = r1.astype(BF16)
    lo = (r1 - mid.astype(F32)).astype(BF16)
    return hi, mid, lo


def _gdn_prep_kernel(p_ref, pp_ref, pn_ref, cw_ref, ab_ref, alog_ref, dtb_ref, o_ref, gb_ref, gt_ref, ext_ref, *, seq):
    tm = p_ref.shape[0]
    tps = seq // tm
    pos = pl.program_id(0) % tps
    j = pl.program_id(1)
    has_prev = (pos != 0).astype(F32)
    has_next = (pos != tps - 1).astype(F32)
    ext_ref[0:CONV_HALO, :] = pp_ref[...].astype(F32) * has_prev
    ext_ref[CONV_HALO:CONV_HALO + tm, :] = p_ref[...].astype(F32)
    ext_ref[CONV_HALO + tm:, :] = pn_ref[...].astype(F32) * has_next
    base = CONV_HALO - GDN_CONV // 2
    acc = cw_ref[0:1, :] * ext_ref[base:base + tm, :]
    for k in range(1, GDN_CONV):
        acc = acc + cw_ref[k:k + 1, :] * ext_ref[base + k:base + k + tm, :]
    y = _silu(acc)
    q_tiles = GDN_QK_W // o_ref.shape[1]

    def l2(scale):
        for c in range(o_ref.shape[1] // HEAD):
            sl = slice(c * HEAD, (c + 1) * HEAD)
            yc = y[:, sl]
            yn = yc * lax.rsqrt(jnp.sum(yc * yc, axis=-1, keepdims=True) + L2_EPS)
            o_ref[:, sl] = (yn * scale).astype(BF16)

    @pl.when(j < q_tiles)
    def _():
        l2(HEAD ** -0.5)

    @pl.when(jnp.logical_and(j >= q_tiles, j < 2 * q_tiles))
    def _():
        l2(1.0)

    @pl.when(j >= 2 * q_tiles)
    def _():
        o_ref[...] = y.astype(BF16)

    @pl.when(j == 0)
    def _():
        ab = ab_ref[...]
        lane = lax.broadcasted_iota(jnp.int32, (1, LANES), 1)
        is_g = (lane % (2 * GDN_V_HEADS)) < GDN_V_HEADS
        gval = -jnp.exp(alog_ref[...]) * _softplus(ab + dtb_ref[...])
        g = jnp.where(is_g, gval, 0.0)
        r = lax.broadcasted_iota(jnp.int32, (tm, tm), 0)
        c = lax.broadcasted_iota(jnp.int32, (tm, tm), 1)
        same = (r // GDN_CHUNK) == (c // GDN_CHUNK)
        tri_f = jnp.where(jnp.logical_and(same, r >= c), 1.0, 0.0).astype(BF16)
        tri_b = jnp.where(jnp.logical_and(same, r <= c), 1.0, 0.0).astype(BF16)
        pieces = _split3(g)
        gc_f = functools.reduce(jnp.add, [_dot(tri_f, x) for x in pieces])
        gc_b = functools.reduce(jnp.add, [_dot(tri_b, x) for x in pieces])
        gc = jnp.where(lane < 2 * GDN_V_HEADS, gc_f, gc_b)
        gb_ref[...] = jnp.where(is_g, gc, jax.nn.sigmoid(ab))
        for ci in range(tm // GDN_CHUNK):
            tile = gc[ci * GDN_CHUNK:(ci + 1) * GDN_CHUNK, :]
            tt = jnp.concatenate([tile, tile], axis=0).T
            gt_ref[ci, 0:GDN_V_HEADS, :] = tt[0:GDN_V_HEADS, :]
            gt_ref[ci, GDN_V_HEADS:, :] = tt[2 * GDN_V_HEADS:3 * GDN_V_HEADS, :]


def _gdn_prep(p, ab, conv_w, alog_row, dtb_row, seq, *, tn=512):
    m = p.shape[0]
    tm = _tile(seq, 512)
    hb = tm // CONV_HALO
    last = m // CONV_HALO - 1
    return pl.pallas_call(
        functools.partial(_gdn_prep_kernel, seq=seq),
        grid=(m // tm, GDN_CONV_CH // tn),
        in_specs=[
            pl.BlockSpec((tm, tn), lambda i, j: (i, j)),
            pl.BlockSpec((CONV_HALO, tn), lambda i, j: (jnp.maximum(i * hb - 1, 0), j)),
            pl.BlockSpec((CONV_HALO, tn), lambda i, j: (jnp.minimum((i + 1) * hb, last), j)),
            pl.BlockSpec((GDN_CONV, tn), lambda i, j: (0, j)),
            pl.BlockSpec((tm, LANES), lambda i, j: (i, 0)),
            pl.BlockSpec((1, LANES), lambda i, j: (0, 0)),
            pl.BlockSpec((1, LANES), lambda i, j: (0, 0)),
        ],
        out_specs=[
            pl.BlockSpec((tm, tn), lambda i, j: (i, j)),
            pl.BlockSpec((tm, LANES), lambda i, j: (i, 0)),
            pl.BlockSpec((tm // GDN_CHUNK, 2 * GDN_V_HEADS, LANES), lambda i, j: (i, 0, 0)),
        ],
        out_shape=[
            jax.ShapeDtypeStruct((m, GDN_CONV_CH), BF16),
            jax.ShapeDtypeStruct((m, LANES), F32),
            jax.ShapeDtypeStruct((m // GDN_CHUNK, 2 * GDN_V_HEADS, LANES), F32),
        ],
        scratch_shapes=[pltpu.VMEM((tm + 2 * CONV_HALO, tn), F32)],
        compiler_params=_params(("arbitrary", "arbitrary")),
        name="gdn_prep",
    )(p, p, p, conv_w, ab, alog_row, dtb_row)


def _mm3(a, b_hi, b_lo):
    a_hi = a.astype(BF16)
    a_lo = (a - a_hi.astype(F32)).astype(BF16)
    return _dot(a_hi, b_hi) + _dot(a_lo, b_hi) + _dot(a_hi, b_lo)


def _gdn_core_kernel(qc_ref, kc_ref, vc_ref, gbc_ref, gtc_ref, ql_ref, kl_ref, vl_ref, gbl_ref, gtl_ref,
                     oc_ref, ol_ref, s_ref):
    C = GDN_CHUNK
    jh = pl.program_id(1)
    s_ref[...] = jnp.zeros_like(s_ref)
    oc_ref[...] = jnp.zeros_like(oc_ref)
    ol_ref[...] = jnp.zeros_like(ol_ref)

    lane_w = lax.broadcasted_iota(jnp.int32, (C, 2 * LANES), 1)
    row_w = lax.broadcasted_iota(jnp.int32, (C, 2 * LANES), 0)
    col_w = lane_w % C
    fwd_w = lane_w < LANES
    ahead_w = jnp.where(fwd_w, row_w - col_w, col_w - row_w)
    incl_w = ahead_w >= 0
    strict_w = ahead_w > 0
    prob_w = lane_w // C
    lane_p = lax.broadcasted_iota(jnp.int32, (C, LANES), 1)
    row_p = lax.broadcasted_iota(jnp.int32, (C, LANES), 0)
    first_p = lane_p < C
    lane_s = lax.broadcasted_iota(jnp.int32, (1, 2 * LANES), 1)
    lo_s = lane_s < LANES

    def block_diag(a):
        return jnp.concatenate([jnp.where(prob_w == p, a, 0.0) for p in range(4)], axis=0)

    def pair_blocks(a):
        lo = lax.broadcasted_iota(jnp.int32, a.shape, 1) < LANES
        return jnp.concatenate([jnp.where(lo, a, 0.0), jnp.where(lo, 0.0, a)], axis=0)

    def phase(q_ref, k_ref, v_ref, gb_ref, gt_ref, o_ref):
        nchunks = q_ref.shape[0] // C

        def body(n, carry):
            per_dir = []
            for d in range(2):
                ci = n if d == 0 else nchunks - 1 - n
                r0 = pl.multiple_of(ci * C, C)
                q = q_ref[pl.ds(r0, C), :]
                k = k_ref[pl.ds(r0, C), :]
                v = v_ref[pl.ds(r0, C), :].astype(F32)
                gb = gb_ref[pl.ds(r0, C), :]
                gt = gt_ref[ci]
                g_lane = d * 2 * GDN_V_HEADS + 2 * jh
                cols = []
                for lane0 in (g_lane, g_lane + GDN_V_HEADS):
                    e0 = jnp.sum(jnp.where(lane_p == lane0, gb, 0.0), axis=1, keepdims=True)
                    e1 = jnp.sum(jnp.where(lane_p == lane0 + 1, gb, 0.0), axis=1, keepdims=True)
                    cols.append((e0, e1))
                (gc0, gc1), (be0, be1) = cols
                g_row = d * GDN_V_HEADS + 2 * jh
                pick = row_p == jnp.where(first_p, g_row, g_row + 1)
                gc_row = jnp.sum(jnp.where(pick, gt, 0.0), axis=0, keepdims=True)
                kq = _dot_nt(jnp.concatenate([k, q], axis=0), jnp.concatenate([k, k], axis=0))
                last = C - 1 if d == 0 else 0
                per_dir.append(dict(q=q.astype(F32), k=k.astype(F32), v=v, gc=(gc0, gc1), be=(be0, be1),
                                    gc_col=jnp.where(first_p, gc0, gc1), be_col=jnp.where(first_p, be0, be1),
                                    gc_row=gc_row, kk=kq[0:C], qk=kq[C:2 * C],
                                    tot=(gc0[last:last + 1], gc1[last:last + 1]), r0=r0))
            wide = lambda key: jnp.concatenate([per_dir[0][key], per_dir[1][key]], axis=1)
            dm = wide("gc_col") - wide("gc_row")
            decay = jnp.where(incl_w, jnp.exp(jnp.where(incl_w, dm, 0.0)), 0.0)
            mj = -jnp.where(strict_w, wide("be_col") * wide("kk") * decay, 0.0)
            qkm = wide("qk") * decay

            x = mj
            for lvl in range(6):
                bd = block_diag(mj)
                bd_hi = bd.astype(BF16)
                bd_lo = (bd - bd_hi.astype(F32)).astype(BF16)
                if lvl == 0:
                    mj = _mm3(mj, bd_hi, bd_lo)
                elif lvl < 5:
                    both = _mm3(jnp.concatenate([mj, x], axis=0), bd_hi, bd_lo)
                    x = x + mj + both[C:2 * C]
                    mj = both[0:C]
                else:
                    x = x + mj + _mm3(x, bd_hi, bd_lo)

            rhs = []
            for d in range(2):
                pd = per_dir[d]
                for e in range(2):
                    be, egc = pd["be"][e], jnp.exp(pd["gc"][e])
                    rhs.append(jnp.concatenate([be * pd["v"][:, e * HEAD:(e + 1) * HEAD], (be * egc) * pd["k"]], axis=1))
            rhs = jnp.concatenate(rhs, axis=0)
            uw = rhs + _dot(block_diag(x).astype(BF16), rhs.astype(BF16))

            for d in range(2):
                pd = per_dir[d]
                u = jnp.concatenate([uw[(2 * d + e) * C:(2 * d + e + 1) * C, 0:HEAD] for e in range(2)], axis=1)
                w = jnp.concatenate([uw[(2 * d + e) * C:(2 * d + e + 1) * C, HEAD:] for e in range(2)], axis=1)
                qdec = jnp.concatenate([pd["q"] * jnp.exp(pd["gc"][e]) for e in range(2)], axis=1)
                state = s_ref[d]
                ws_qs = _dot(jnp.concatenate([w, qdec], axis=0).astype(BF16), pair_blocks(state).astype(BF16))
                v_new = u - ws_qs[0:C]
                o = ws_qs[C:2 * C] + _dot(qkm[:, d * LANES:(d + 1) * LANES].astype(BF16),
                                          pair_blocks(v_new).astype(BF16))
                o_ref[pl.ds(pd["r0"], C), :] += o
                tot0, tot1 = pd["tot"]
                kscale = jnp.concatenate([jnp.broadcast_to(jnp.exp(tot0 - pd["gc"][0]), (C, HEAD)),
                                          jnp.broadcast_to(jnp.exp(tot1 - pd["gc"][1]), (C, HEAD))], axis=1)
                ds = _dot(pd["k"].T.astype(BF16), (v_new * kscale).astype(BF16))
                gtot = jnp.where(lo_s, jnp.exp(tot0), jnp.exp(tot1))
                s_ref[d] = state * gtot + ds
            return carry

        lax.fori_loop(0, nchunks, body, 0)

    phase(qc_ref, kc_ref, vc_ref, gbc_ref, gtc_ref, oc_ref)
    phase(ql_ref, kl_ref, vl_ref, gbl_ref, gtl_ref, ol_ref)


def _gdn_core(qkv_c, gb_c, gt_c, qkv_l, gb_l, gt_l, batch):
    lc = qkv_c.shape[0] // batch
    ls = qkv_l.shape[0] // batch
    nqk = GDN_QK_HEADS
    pair = 2 * HEAD

    def specs(rows):
        return [
            pl.BlockSpec((rows, HEAD), lambda b, j: (b, j)),
            pl.BlockSpec((rows, HEAD), lambda b, j: (b, nqk + j)),
            pl.BlockSpec((rows, pair), lambda b, j: (b, nqk + j)),
            pl.BlockSpec((rows, LANES), lambda b, j: (b, 0)),
            pl.BlockSpec((rows // GDN_CHUNK, 2 * GDN_V_HEADS, LANES), lambda b, j: (b, 0, 0)),
        ]

    return pl.pallas_call(
        _gdn_core_kernel,
        grid=(batch, nqk),
        in_specs=specs(lc) + specs(ls),
        out_specs=[pl.BlockSpec((lc, pair), lambda b, j: (b, j)),
                   pl.BlockSpec((ls, pair), lambda b, j: (b, j))],
        out_shape=[jax.ShapeDtypeStruct((batch * lc, GDN_V_W), F32),
                   jax.ShapeDtypeStruct((batch * ls, GDN_V_W), F32)],
        scratch_shapes=[pltpu.VMEM((2, HEAD, pair), F32)],
        compiler_params=_params(("arbitrary", "arbitrary")),
        name="gdn_core",
    )(qkv_c, qkv_c, qkv_c, gb_c, gt_c, qkv_l, qkv_l, qkv_l, gb_l, gt_l)


def _rope_tables(n_tokens, head_dim):
    rows = n_tokens // GRID_W
    r = jnp.repeat(jnp.arange(rows, dtype=F32), GRID_W)
    col = jnp.tile(jnp.arange(GRID_W, dtype=F32), rows)
    d_axis = head_dim // 2
    inv = ROPE_THETA ** (-jnp.arange(0, d_axis, 2, dtype=F32) / d_axis)
    ang = jnp.concatenate([r[:, None] * inv, col[:, None] * inv], axis=-1)
    cos = jnp.repeat(jnp.cos(ang), 2, axis=-1)
    sin = jnp.repeat(jnp.sin(ang), 2, axis=-1) * jnp.tile(jnp.array([-1.0, 1.0], F32), head_dim // 2)
    reps = LANES // head_dim
    return jnp.tile(cos, (1, reps)), jnp.tile(sin, (1, reps))


def _lane_row(v, reps=1):
    return jnp.tile(v.astype(F32), reps).reshape(1, -1)


def kernel(x, c, ctx, c_ctx, norm_g, mod_w, mod_b, gdn_w_in, gdn_conv_w, gdn_a_log, gdn_dt_bias, gdn_out_norm, gdn_w_out, gqa_w_in, gqa_q_norm, gqa_k_norm, gqa_w_out, pool_w_in, pool_w_grp, pool_scale, pool_w_out, diff_w_in, diff_q_norm, diff_k_norm, diff_lambda_q1, diff_lambda_k1, diff_lambda_q2, diff_lambda_k2, diff_sub_norm, diff_w_out):
    batch, seq, d = x.shape
    lctx = ctx.shape[1]
    depth = norm_g.shape[0]
    lat = x.reshape(batch * seq, d)
    cx = ctx.reshape(batch * lctx, d)

    mod_rows = -(-(batch + 1) // 8) * 8
    c_all = jnp.concatenate([c, c_ctx[None, :], jnp.zeros((mod_rows - batch - 1, d), F32)], axis=0)
    mod3 = _mod_call(c_all, mod_w, mod_b).reshape(depth * mod_rows * 3, 1, d)

    rope_gqa = _rope_tables(seq, HEAD)
    rope_diff = _rope_tables(seq, DIFF_DH)

    for i in range(depth):
        m, jl = i % N_MIXERS, i // N_MIXERS
        need_ctx = i < depth - 1
        base_l, base_c = i * mod_rows, i * mod_rows + batch
        g_row = norm_g[i].reshape(1, d)
        lat_args = (lat, mod3, base_l, seq, g_row)
        ctx_args = (cx, mod3, base_c, None, g_row)
        if m == 0:
            w_in = gdn_w_in[jl]
            w_main = w_in[:, :GDN_CONV_CH + GDN_V_W].astype(BF16)
            w_ab = w_in[:, GDN_CONV_CH + GDN_V_W:].astype(BF16)
            conv_w = gdn_conv_w[jl]
            zeros = jnp.zeros((GDN_V_HEADS,), F32)
            alog_row = jnp.concatenate([gdn_a_log[jl, 0], zeros, gdn_a_log[jl, 1], zeros]).reshape(1, LANES)
            dtb_row = jnp.concatenate([gdn_dt_bias[jl, 0], zeros, gdn_dt_bias[jl, 1], zeros]).reshape(1, LANES)
            p_l, ab_l = _proj(*lat_args, w_main, kind="gdn", wab=w_ab)
            p_c, ab_c = _proj(*ctx_args, w_main, kind="gdn", wab=w_ab)
            qkv_l, gb_l, gt_l = _gdn_prep(p_l, ab_l, conv_w, alog_row, dtb_row, seq)
            qkv_c, gb_c, gt_c = _gdn_prep(p_c, ab_c, conv_w, alog_row, dtb_row, lctx)
            o_c, o_l = _gdn_core(qkv_c, gb_c, gt_c, qkv_l, gb_l, gt_l, batch)
            w_out = gdn_w_out[jl].astype(BF16)
            norm_row = _lane_row(gdn_out_norm[jl])
            lat = _out_proj(o_l, w_out, lat, mod3, base_l, seq, gdn_zp=p_l, gdn_norm=norm_row)
            if need_ctx:
                cx = _out_proj(o_c, w_out, cx, mod3, base_c, None, gdn_zp=p_c, gdn_norm=norm_row)
        elif m == 1:
            w_in = gqa_w_in[jl].astype(BF16)
            heads = (_lane_row(gqa_q_norm[jl]), _lane_row(gqa_k_norm[jl]), GQA_QW, GQA_KVW, HEAD, HEAD ** -0.5)
            p_l = _proj(*lat_args, w_in, kind="heads", heads=heads, rope_tabs=rope_gqa)
            p_c = _proj(*ctx_args, w_in, kind="heads", heads=heads)
            w_out = gqa_w_out[jl].astype(BF16)
            y_l = _gqa_attention(p_l, p_c, p_l, batch)
            lat = _out_proj(y_l, w_out, lat, mod3, base_l, seq)
            if need_ctx:
                y_c = _gqa_attention(p_c, p_c, None, batch)
                cx = _out_proj(y_c, w_out, cx, mod3, base_c, None)
        elif m == 2:
            w_in = pool_w_in[jl].astype(BF16)
            w_grp = pool_w_grp[jl].astype(BF16)
            sc_row = pool_scale[jl].reshape(1, d)
            w_out = pool_w_out[jl].astype(BF16)
            y_l = _pool_mix(_proj(*lat_args, w_in), w_grp, sc_row, seq)
            lat = _out_proj(y_l, w_out, lat, mod3, base_l, seq)
            if need_ctx:
                y_c = _pool_mix(_proj(*ctx_args, w_in), w_grp, sc_row, lctx)
                cx = _out_proj(y_c, w_out, cx, mod3, base_c, None)
        else:
            lam_init = 0.8 - 0.6 * math.exp(-0.3 * i)
            w_in = diff_w_in[jl].astype(BF16)
            heads = (_lane_row(diff_q_norm[jl], 2), _lane_row(diff_k_norm[jl], 2), DIFF_W, DIFF_W, DIFF_DH,
                     DIFF_DH ** -0.5)
            p_l = _proj(*lat_args, w_in, kind="heads", heads=heads, rope_tabs=rope_diff)
            p_c = _proj(*ctx_args, w_in, kind="heads", heads=heads, j0=DIFF_W // 512, nj=2 * DIFF_W // 512)
            lams = [v[jl].reshape(1, DIFF_DH) for v in (diff_lambda_q1, diff_lambda_k1, diff_lambda_q2, diff_lambda_k2)]
            y_l = _diff_attention(p_l, p_c, lams, _lane_row(diff_sub_norm[jl]), batch, lam_init)
            lat = _out_proj(y_l, diff_w_out[jl].astype(BF16), lat, mod3, base_l, seq)
    return lat.reshape(batch, seq, d)
```

```python
import functools
import math

import jax
import jax.numpy as jnp
from jax import lax
from jax.experimental import pallas as pl
from jax.experimental.pallas import tpu as pltpu

F32 = jnp.float32
BF16 = jnp.bfloat16

D_MODEL = 2048
N_MIXERS = 4
RMS_EPS = 1e-6
L2_EPS = 1e-6
ROPE_THETA = 10000.0
GRID_W = 64

HEAD = 128
KV_TILE = 2048
LOG2E = math.log2(math.e)
GDN_QK_HEADS = D_MODEL // HEAD
GDN_V_HEADS = 2 * GDN_QK_HEADS
GDN_CONV = 5
GDN_CHUNK = 64
GDN_UNROLL = 4
GDN_INV_PASSES = 3
GDN_QK_W = GDN_QK_HEADS * HEAD
GDN_V_W = GDN_V_HEADS * HEAD
GDN_CONV_CH = 2 * GDN_QK_W + GDN_V_W

GQA_HEADS = D_MODEL // HEAD
GQA_KV_HEADS = GQA_HEADS // 4
GQA_QW = GQA_HEADS * HEAD
GQA_KVW = GQA_KV_HEADS * HEAD

POOL_WINDOWS = (2, 4, 8, 16)
POOL_GW = D_MODEL // len(POOL_WINDOWS)
POOL_HALO = 16

DIFF_DH = 64
DIFF_HEADS = D_MODEL // (2 * DIFF_DH)
DIFF_W = DIFF_HEADS * 2 * DIFF_DH

VMEM_LIMIT_BYTES = 56 * 1024 * 1024
LANES = 128
BF16_SUBLANES = 16
CONV_HALO = 16


def _params(sem):
    return pltpu.CompilerParams(dimension_semantics=sem, vmem_limit_bytes=VMEM_LIMIT_BYTES)


def _tile(n, pref):
    t = min(n, pref)
    while n % t or t % BF16_SUBLANES:
        t -= 1
    return t


def _silu(x):
    return x * jax.nn.sigmoid(x)


def _softplus(x):
    return jnp.maximum(x, 0.0) + jnp.log(1.0 + jnp.exp(-jnp.abs(x)))


def _dot(a, b):
    return jnp.dot(a, b, preferred_element_type=F32)


def _dot_nt(a, b):
    return lax.dot_general(a, b, (((1,), (1,)), ((), ())), preferred_element_type=F32)


def _swap_pairs(x):
    lane = lax.broadcasted_iota(jnp.int32, x.shape, 1)
    return jnp.where(lane % 2 == 0, pltpu.roll(x, LANES - 1, 1), pltpu.roll(x, 1, 1))


def _group_mean_sq(x, group):
    x2 = x * x
    if group == LANES:
        return jnp.mean(x2, axis=-1, keepdims=True)
    lo = lax.broadcasted_iota(jnp.int32, x.shape, 1) < group
    s_lo = jnp.sum(jnp.where(lo, x2, 0.0), axis=-1, keepdims=True)
    s_hi = jnp.sum(jnp.where(lo, 0.0, x2), axis=-1, keepdims=True)
    return jnp.where(lo, s_lo, s_hi) * (1.0 / group)


def _mod_kernel(c_ref, w_ref, b_ref, o_ref):
    a = _silu(c_ref[...]).astype(BF16)
    o_ref[0] = _dot(a, w_ref[0].astype(BF16)) + b_ref[0]


def _mod_call(c_all, mod_w, mod_b):
    depth, d, n = mod_w.shape
    rows = c_all.shape[0]
    tn = _tile(n, 768)
    return pl.pallas_call(
        _mod_kernel,
        grid=(depth, n // tn),
        in_specs=[
            pl.BlockSpec((rows, d), lambda l, j: (0, 0)),
            pl.BlockSpec((1, d, tn), lambda l, j: (l, 0, j)),
            pl.BlockSpec((1, 1, tn), lambda l, j: (l, 0, j)),
        ],
        out_specs=pl.BlockSpec((1, rows, tn), lambda l, j: (l, 0, j)),
        out_shape=jax.ShapeDtypeStruct((depth, rows, n), F32),
        compiler_params=_params(("arbitrary", "arbitrary")),
        name="mod_vectors",
    )(c_all, mod_w, mod_b.reshape(depth, 1, n))


def _head_epilogue(acc, o_ref, gain, cs, scale, group):
    for c in range(acc.shape[1] // LANES):
        xc = acc[:, c * LANES:(c + 1) * LANES]
        y = xc * lax.rsqrt(_group_mean_sq(xc, group) + RMS_EPS) * gain
        if cs is not None:
            y = y * cs[0] + _swap_pairs(y) * cs[1]
        if scale != 1.0:
            y = y * scale
        o_ref[:, c * LANES:(c + 1) * LANES] = y.astype(o_ref.dtype)


def _proj_kernel(*refs, kind, rope, j0, q_tiles, k_tiles, group, q_scale):
    x_ref, g_ref, sh_ref, sc_ref, w_ref = refs[:5]
    rest = list(refs[5:])
    h_ref = rest.pop()
    qn_ref = kn_ref = c_ref = s_ref = wab_ref = ab_ref = None
    if kind == "heads":
        qn_ref, kn_ref = rest.pop(0), rest.pop(0)
        if rope:
            c_ref, s_ref = rest.pop(0), rest.pop(0)
    if kind == "gdn":
        wab_ref = rest.pop(0)
        o_ref, ab_ref = rest
    else:
        (o_ref,) = rest
    j = pl.program_id(1)

    @pl.when(j == 0)
    def _():
        x = x_ref[...]
        ms = jnp.mean(x * x, axis=-1, keepdims=True)
        y = x * lax.rsqrt(ms + RMS_EPS) * g_ref[...]
        h = y * (1.0 + sc_ref[0]) + sh_ref[0]
        h_ref[...] = h.astype(BF16)
        if kind == "gdn":
            ab_ref[...] = _dot(h_ref[...], wab_ref[...])

    acc = _dot(h_ref[...], w_ref[...])
    if kind != "heads":
        o_ref[...] = acc.astype(o_ref.dtype)
        return
    jg = j + j0
    cs = (c_ref[...], s_ref[...]) if rope else None

    @pl.when(jg < q_tiles)
    def _():
        _head_epilogue(acc, o_ref, qn_ref[...], cs, q_scale, group)

    @pl.when(jnp.logical_and(jg >= q_tiles, jg < q_tiles + k_tiles))
    def _():
        _head_epilogue(acc, o_ref, kn_ref[...], cs, 1.0, group)

    @pl.when(jg >= q_tiles + k_tiles)
    def _():
        o_ref[...] = acc.astype(o_ref.dtype)


def _proj(xs, mod3, mod_base, batch_rows, g_row, w, *, kind="plain", heads=None, rope_tabs=None,
          wab=None, j0=0, nj=None, tn=512, tm_pref=512):
    m, d = xs.shape
    seq = batch_rows if batch_rows is not None else m
    tm = _tile(seq if rope_tabs is not None or batch_rows is not None else m, tm_pref)
    n_total = w.shape[1] // tn
    nj = n_total - j0 if nj is None else nj
    tiles_per_batch = (batch_rows // tm) if batch_rows is not None else None

    def mod_idx(which):
        if tiles_per_batch is None:
            return lambda i, j: (mod_base * 3 + which, 0, 0)
        return lambda i, j: ((mod_base + i // tiles_per_batch) * 3 + which, 0, 0)

    in_specs = [
        pl.BlockSpec((tm, d), lambda i, j: (i, 0)),
        pl.BlockSpec((1, d), lambda i, j: (0, 0)),
        pl.BlockSpec((1, 1, d), mod_idx(0)),
        pl.BlockSpec((1, 1, d), mod_idx(1)),
        pl.BlockSpec((d, tn), lambda i, j: (0, j + j0)),
    ]
    args = [xs, g_row, mod3, mod3, w]
    kw = dict(kind=kind, rope=rope_tabs is not None, j0=j0, q_tiles=0, k_tiles=0, group=LANES, q_scale=1.0)
    if kind == "heads":
        qn, kn, q_cols, k_cols, group, q_scale = heads
        kw.update(q_tiles=q_cols // tn, k_tiles=k_cols // tn, group=group, q_scale=q_scale)
        in_specs += [pl.BlockSpec((1, LANES), lambda i, j: (0, 0))] * 2
        args += [qn, kn]
        if rope_tabs is not None:
            tps = seq // tm
            in_specs += [pl.BlockSpec((tm, LANES), lambda i, j: (i % tps, 0))] * 2
            args += list(rope_tabs)
    out_specs = pl.BlockSpec((tm, tn), lambda i, j: (i, j))
    out_shape = jax.ShapeDtypeStruct((m, nj * tn), BF16)
    if kind == "gdn":
        in_specs.append(pl.BlockSpec((d, LANES), lambda i, j: (0, 0)))
        args.append(wab)
        out_specs = [out_specs, pl.BlockSpec((tm, LANES), lambda i, j: (i, 0))]
        out_shape = [out_shape, jax.ShapeDtypeStruct((m, LANES), F32)]
    return pl.pallas_call(
        functools.partial(_proj_kernel, **kw),
        grid=(m // tm, nj),
        in_specs=in_specs,
        out_specs=out_specs,
        out_shape=out_shape,
        scratch_shapes=[pltpu.VMEM((tm, d), BF16)],
        compiler_params=_params(("arbitrary", "arbitrary")),
        name="proj_" + kind,
    )(*args)


def _out_kernel(*refs, gdn):
    if gdn:
        o_ref, z_ref, n_ref, w_ref, x_ref, gate_ref, out_ref, y_ref = refs

        @pl.when(pl.program_id(1) == 0)
        def _():
            gain = n_ref[...]
            for c in range(o_ref.shape[1] // LANES):
                sl = slice(c * LANES, (c + 1) * LANES)
                o = o_ref[:, sl]
                y = o * lax.rsqrt(jnp.mean(o * o, axis=-1, keepdims=True) + RMS_EPS) * gain
                y_ref[:, sl] = (y * _silu(z_ref[:, sl].astype(F32))).astype(BF16)
    else:
        y_ref, w_ref, x_ref, gate_ref, out_ref = refs
    out_ref[...] = x_ref[...] + gate_ref[0] * _dot(y_ref[...], w_ref[...])


def _out_proj(y, w, xs, mod3, mod_base, batch_rows, *, gdn_zp=None, gdn_norm=None, tn=512, tm_pref=512):
    m, d = xs.shape
    kdim = w.shape[0]
    tm = _tile(batch_rows if batch_rows is not None else m, tm_pref)
    tiles_per_batch = (batch_rows // tm) if batch_rows is not None else None
    if tiles_per_batch is None:
        gate_idx = lambda i, j: (mod_base * 3 + 2, 0, 0)
    else:
        gate_idx = lambda i, j: ((mod_base + i // tiles_per_batch) * 3 + 2, 0, 0)
    in_specs = [pl.BlockSpec((tm, kdim), lambda i, j: (i, 0))]
    args = [y]
    scratch = []
    if gdn_zp is not None:
        z_blk = GDN_CONV_CH // kdim
        in_specs += [pl.BlockSpec((tm, kdim), lambda i, j: (i, z_blk)),
                     pl.BlockSpec((1, LANES), lambda i, j: (0, 0))]
        args += [gdn_zp, gdn_norm]
        scratch = [pltpu.VMEM((tm, kdim), BF16)]
    in_specs += [
        pl.BlockSpec((kdim, tn), lambda i, j: (0, j)),
        pl.BlockSpec((tm, tn), lambda i, j: (i, j)),
        pl.BlockSpec((1, 1, tn), lambda i, j: gate_idx(i, j)[:2] + (j,)),
    ]
    args += [w, xs, mod3]
    return pl.pallas_call(
        functools.partial(_out_kernel, gdn=gdn_zp is not None),
        grid=(m // tm, d // tn),
        in_specs=in_specs,
        out_specs=pl.BlockSpec((tm, tn), lambda i, j: (i, j)),
        out_shape=jax.ShapeDtypeStruct((m, d), F32),
        scratch_shapes=scratch,
        compiler_params=_params(("arbitrary", "arbitrary")),
        name="out_proj_gdn" if gdn_zp is not None else "out_proj",
    )(*args)


def _flash(queries, kv_refs):
    tq = queries[0].shape[0]
    m = [jnp.full((tq, 1), -1e30, F32) for _ in queries]
    l = [jnp.zeros((tq, 1), F32) for _ in queries]
    acc = [jnp.zeros((tq, HEAD), F32) for _ in queries]
    for k_ref, v_ref in kv_refs:
        rows = k_ref.shape[0]
        tk = math.gcd(rows, KV_TILE)
        for t in range(rows // tk):
            k = k_ref[t * tk:(t + 1) * tk, :]
            v = v_ref[t * tk:(t + 1) * tk, :]
            for i, q in enumerate(queries):
                s = _dot_nt(q, k)
                m_new = jnp.maximum(m[i], jnp.max(s, axis=-1, keepdims=True))
                alpha = jnp.exp2(m[i] - m_new)
                p = jnp.exp2(s - m_new)
                l[i] = alpha * l[i] + jnp.sum(p, axis=-1, keepdims=True)
                acc[i] = alpha * acc[i] + _dot(p.astype(BF16), v)
                m[i] = m_new
    return [a / li for a, li in zip(acc, l)]


def _gqa_attn_kernel(*refs, has_lat):
    if has_lat:
        q_ref, z_ref, kc_ref, vc_ref, kl_ref, vl_ref, y_ref = refs
        kv = [(kc_ref, vc_ref), (kl_ref, vl_ref)]
    else:
        q_ref, z_ref, kc_ref, vc_ref, y_ref = refs
        kv = [(kc_ref, vc_ref)]
    tq = q_ref.shape[0]
    heads = [slice(g * HEAD, (g + 1) * HEAD) for g in range(q_ref.shape[1] // HEAD)]
    (o,) = _flash([jnp.concatenate([q_ref[:, sl] for sl in heads], axis=0)], kv)
    for g, sl in enumerate(heads):
        y_ref[:, sl] = (o[g * tq:(g + 1) * tq] * _silu(z_ref[:, sl].astype(F32))).astype(BF16)


def _gqa_attention(p_q, p_c, p_l, batch, *, tq_pref=256):
    has_lat = p_l is not None
    m = p_q.shape[0]
    rows_q = m // batch
    tq = _tile(rows_q, tq_pref)
    nq = rows_q // tq
    lc = p_c.shape[0] // batch
    grp = GQA_QW // GQA_KV_HEADS
    k_blk, v_blk, z_blk = GQA_QW // HEAD, (GQA_QW + GQA_KVW) // HEAD, (GQA_QW + 2 * GQA_KVW) // grp
    in_specs = [
        pl.BlockSpec((tq, grp), lambda b, h, i: (b * nq + i, h)),
        pl.BlockSpec((tq, grp), lambda b, h, i: (b * nq + i, z_blk + h)),
        pl.BlockSpec((lc, HEAD), lambda b, h, i: (b, k_blk + h)),
        pl.BlockSpec((lc, HEAD), lambda b, h, i: (b, v_blk + h)),
    ]
    args = [p_q, p_q, p_c, p_c]
    if has_lat:
        ls = p_l.shape[0] // batch
        in_specs += [pl.BlockSpec((ls, HEAD), lambda b, h, i: (b, k_blk + h)),
                     pl.BlockSpec((ls, HEAD), lambda b, h, i: (b, v_blk + h))]
        args += [p_l, p_l]
    return pl.pallas_call(
        functools.partial(_gqa_attn_kernel, has_lat=has_lat),
        grid=(batch, GQA_KV_HEADS, nq),
        in_specs=in_specs,
        out_specs=pl.BlockSpec((tq, grp), lambda b, h, i: (b * nq + i, h)),
        out_shape=jax.ShapeDtypeStruct((m, GQA_QW), BF16),
        compiler_params=_params(("arbitrary", "arbitrary", "arbitrary")),
        name="gqa_attention",
    )(*args)


def _diff_attn_kernel(lq1_ref, lk1_ref, lq2_ref, lk2_ref, n_ref, q_ref, z_ref, kc_ref, vc_ref, kl_ref, vl_ref,
                      y_ref, *, lam_init):
    lam = (jnp.exp(jnp.sum(lq1_ref[...] * lk1_ref[...], axis=-1, keepdims=True))
           - jnp.exp(jnp.sum(lq2_ref[...] * lk2_ref[...], axis=-1, keepdims=True)) + lam_init)
    q = q_ref[...]
    lo = lax.broadcasted_iota(jnp.int32, q.shape, 1) < DIFF_DH
    zero = jnp.zeros_like(q)
    tq = q.shape[0]
    (o01,) = _flash([jnp.concatenate([jnp.where(lo, q, zero), jnp.where(lo, zero, q)], axis=0)],
                    [(kc_ref, vc_ref), (kl_ref, vl_ref)])
    o = o01[0:tq] - lam * o01[tq:2 * tq]
    y = o * lax.rsqrt(jnp.mean(o * o, axis=-1, keepdims=True) + RMS_EPS) * n_ref[...] * (1.0 - lam_init)
    y_ref[...] = (y * _silu(z_ref[...].astype(F32))).astype(BF16)


def _diff_attention(p_l, p_c, lams, sub_norm, batch, lam_init, *, tq_pref=256):
    m = p_l.shape[0]
    ls = m // batch
    lc = p_c.shape[0] // batch
    tq = _tile(ls, tq_pref)
    nq = ls // tq
    nh = DIFF_HEADS
    small = pl.BlockSpec((1, DIFF_DH), lambda b, h, i: (0, 0))
    in_specs = [small] * 4 + [
        pl.BlockSpec((1, HEAD), lambda b, h, i: (0, 0)),
        pl.BlockSpec((tq, HEAD), lambda b, h, i: (b * nq + i, h)),
        pl.BlockSpec((tq, HEAD), lambda b, h, i: (b * nq + i, 3 * nh + h)),
        pl.BlockSpec((lc, HEAD), lambda b, h, i: (b, h)),
        pl.BlockSpec((lc, HEAD), lambda b, h, i: (b, nh + h)),
        pl.BlockSpec((ls, HEAD), lambda b, h, i: (b, nh + h)),
        pl.BlockSpec((ls, HEAD), lambda b, h, i: (b, 2 * nh + h)),
    ]
    return pl.pallas_call(
        functools.partial(_diff_attn_kernel, lam_init=lam_init),
        grid=(batch, nh, nq),
        in_specs=in_specs,
        out_specs=pl.BlockSpec((tq, HEAD), lambda b, h, i: (b * nq + i, h)),
        out_shape=jax.ShapeDtypeStruct((m, DIFF_W), BF16),
        compiler_params=_params(("arbitrary", "arbitrary", "arbitrary")),
        name="diff_attention",
    )(*lams, sub_norm, p_l, p_l, p_c, p_c, p_l, p_l)


def _pool_kernel(u_ref, up_ref, un_ref, z_ref, wg_ref, sc_ref, y_ref, ext_ref, *, seq):
    tm = u_ref.shape[0]
    tps = seq // tm
    pos = pl.program_id(0) % tps
    has_prev = (pos != 0).astype(F32)
    has_next = (pos != tps - 1).astype(F32)
    ext_ref[0:POOL_HALO, :] = up_ref[...].astype(F32) * has_prev
    ext_ref[POOL_HALO:POOL_HALO + tm, :] = u_ref[...].astype(F32)
    ext_ref[POOL_HALO + tm:, :] = un_ref[...].astype(F32) * has_next
    t = pos * tm + lax.broadcasted_iota(jnp.int32, (tm, 1), 0)
    for gi, w in enumerate(POOL_WINDOWS):
        cols = slice(gi * POOL_GW, (gi + 1) * POOL_GW)
        tot = ext_ref[POOL_HALO - w // 2:POOL_HALO - w // 2 + tm, cols]
        for k in range(1, w):
            tot = tot + ext_ref[POOL_HALO - w // 2 + k:POOL_HALO - w // 2 + k + tm, cols]
        cnt = jnp.minimum(t - w // 2 + w, seq) - jnp.maximum(t - w // 2, 0)
        dgrp = tot / cnt.astype(F32) - ext_ref[POOL_HALO:POOL_HALO + tm, cols]
        r = _dot(dgrp.astype(BF16), wg_ref[gi]) * sc_ref[:, cols]
        y_ref[:, cols] = (r * _silu(z_ref[:, cols].astype(F32))).astype(BF16)


def _pool_mix(p, w_grp, ch_scale, seq):
    m = p.shape[0]
    w_u = D_MODEL
    tm = _tile(seq, 256)
    hb = tm // POOL_HALO
    last = m // POOL_HALO - 1
    return pl.pallas_call(
        functools.partial(_pool_kernel, seq=seq),
        grid=(m // tm,),
        in_specs=[
            pl.BlockSpec((tm, w_u), lambda i: (i, 0)),
            pl.BlockSpec((POOL_HALO, w_u), lambda i: (jnp.maximum(i * hb - 1, 0), 0)),
            pl.BlockSpec((POOL_HALO, w_u), lambda i: (jnp.minimum((i + 1) * hb, last), 0)),
            pl.BlockSpec((tm, w_u), lambda i: (i, 1)),
            pl.BlockSpec(w_grp.shape, lambda i: (0, 0, 0)),
            pl.BlockSpec((1, w_u), lambda i: (0, 0)),
        ],
        out_specs=pl.BlockSpec((tm, w_u), lambda i: (i, 0)),
        out_shape=jax.ShapeDtypeStruct((m, w_u), BF16),
        scratch_shapes=[pltpu.VMEM((tm + 2 * POOL_HALO, w_u), F32)],
        compiler_params=_params(("arbitrary",)),
        name="pool_mix",
    )(p, p, p, p, w_grp, ch_scale)


def _split3(x):
    hi = x.astype(BF16)
    r1 = x - hi.astype(F32)
    mid = r1.astype(BF16)
    lo = (r1 - mid.astype(F32)).astype(BF16)
    return hi, mid, lo


def _gdn_prep_kernel(p_ref, pp_ref, pn_ref, cw_ref, ab_ref, alog_ref, dtb_ref, o_ref, gb_ref, gt_ref, ext_ref, *, seq):
    tm = p_ref.shape[0]
    tps = seq // tm
    pos = pl.program_id(0) % tps
    j = pl.program_id(1)
    has_prev = (pos != 0).astype(F32)
    has_next = (pos != tps - 1).astype(F32)
    ext_ref[0:CONV_HALO, :] = pp_ref[...].astype(F32) * has_prev
    ext_ref[CONV_HALO:CONV_HALO + tm, :] = p_ref[...].astype(F32)
    ext_ref[CONV_HALO + tm:, :] = pn_ref[...].astype(F32) * has_next
    base = CONV_HALO - GDN_CONV // 2
    acc = cw_ref[0:1, :] * ext_ref[base:base + tm, :]
    for k in range(1, GDN_CONV):
        acc = acc + cw_ref[k:k + 1, :] * ext_ref[base + k:base + k + tm, :]
    y = _silu(acc)
    q_tiles = GDN_QK_W // o_ref.shape[1]

    def l2(scale):
        for c in range(o_ref.shape[1] // HEAD):
            sl = slice(c * HEAD, (c + 1) * HEAD)
            yc = y[:, sl]
            yn = yc * lax.rsqrt(jnp.sum(yc * yc, axis=-1, keepdims=True) + L2_EPS)
            o_ref[:, sl] = (yn * scale).astype(BF16)

    @pl.when(j < q_tiles)
    def _():
        l2(HEAD ** -0.5)

    @pl.when(jnp.logical_and(j >= q_tiles, j < 2 * q_tiles))
    def _():
        l2(1.0)

    @pl.when(j >= 2 * q_tiles)
    def _():
        o_ref[...] = y.astype(BF16)

    @pl.when(j == 0)
    def _():
        ab = ab_ref[...]
        lane = lax.broadcasted_iota(jnp.int32, (1, LANES), 1)
        is_g = (lane % (2 * GDN_V_HEADS)) < GDN_V_HEADS
        gval = -jnp.exp(alog_ref[...]) * _softplus(ab + dtb_ref[...])
        g = jnp.where(is_g, gval, 0.0)
        r = lax.broadcasted_iota(jnp.int32, (tm, tm), 0)
        c = lax.broadcasted_iota(jnp.int32, (tm, tm), 1)
        same = (r // GDN_CHUNK) == (c // GDN_CHUNK)
        tri_f = jnp.where(jnp.logical_and(same, r >= c), 1.0, 0.0).astype(BF16)
        tri_b = jnp.where(jnp.logical_and(same, r <= c), 1.0, 0.0).astype(BF16)
        pieces = _split3(g)
        gc_f = functools.reduce(jnp.add, [_dot(tri_f, x) for x in pieces])
        gc_b = functools.reduce(jnp.add, [_dot(tri_b, x) for x in pieces])
        gc = jnp.where(lane < 2 * GDN_V_HEADS, gc_f, gc_b)
        gb_ref[...] = jnp.where(is_g, gc, jax.nn.sigmoid(ab))
        for ci in range(tm // GDN_CHUNK):
            tile = gc[ci * GDN_CHUNK:(ci + 1) * GDN_CHUNK, :]
            tt = jnp.concatenate([tile, tile], axis=0).T
            gt_ref[ci, 0:GDN_V_HEADS, :] = tt[0:GDN_V_HEADS, :]
            gt_ref[ci, GDN_V_HEADS:, :] = tt[2 * GDN_V_HEADS:3 * GDN_V_HEADS, :]


def _gdn_prep(p, ab, conv_w, alog_row, dtb_row, seq, *, tn=512):
    m = p.shape[0]
    tm = _tile(seq, 512)
    hb = tm // CONV_HALO
    last = m // CONV_HALO - 1
    return pl.pallas_call(
        functools.partial(_gdn_prep_kernel, seq=seq),
        grid=(m // tm, GDN_CONV_CH // tn),
        in_specs=[
            pl.BlockSpec((tm, tn), lambda i, j: (i, j)),
            pl.BlockSpec((CONV_HALO, tn), lambda i, j: (jnp.maximum(i * hb - 1, 0), j)),
            pl.BlockSpec((CONV_HALO, tn), lambda i, j: (jnp.minimum((i + 1) * hb, last), j)),
            pl.BlockSpec((GDN_CONV, tn), lambda i, j: (0, j)),
            pl.BlockSpec((tm, LANES), lambda i, j: (i, 0)),
            pl.BlockSpec((1, LANES), lambda i, j: (0, 0)),
            pl.BlockSpec((1, LANES), lambda i, j: (0, 0)),
        ],
        out_specs=[
            pl.BlockSpec((tm, tn), lambda i, j: (i, j)),
            pl.BlockSpec((tm, LANES), lambda i, j: (i, 0)),
            pl.BlockSpec((tm // GDN_CHUNK, 2 * GDN_V_HEADS, LANES), lambda i, j: (i, 0, 0)),
        ],
        out_shape=[
            jax.ShapeDtypeStruct((m, GDN_CONV_CH), BF16),
            jax.ShapeDtypeStruct((m, LANES), F32),
            jax.ShapeDtypeStruct((m // GDN_CHUNK, 2 * GDN_V_HEADS, LANES), F32),
        ],
        scratch_shapes=[pltpu.VMEM((tm + 2 * CONV_HALO, tn), F32)],
        compiler_params=_params(("arbitrary", "arbitrary")),
        name="gdn_prep",
    )(p, p, p, conv_w, ab, alog_row, dtb_row)


def _gdn_core_kernel(qc_ref, kc_ref, vc_ref, gbc_ref, gtc_ref, ql_ref, kl_ref, vl_ref, gbl_ref, gtl_ref,
                     oc_ref, ol_ref, s_ref, wq_ref, u_ref, qkm_ref, kt_ref, ksc_ref, gtot_ref):
    C = GDN_CHUNK
    jh = pl.program_id(1)
    s_ref[...] = jnp.zeros_like(s_ref)
    oc_ref[...] = jnp.zeros_like(oc_ref)
    ol_ref[...] = jnp.zeros_like(ol_ref)

    lane_w = lax.broadcasted_iota(jnp.int32, (C, 2 * LANES), 1)
    row_w = lax.broadcasted_iota(jnp.int32, (C, 2 * LANES), 0)
    col_w = lane_w % C
    fwd_w = lane_w < LANES
    ahead_w = jnp.where(fwd_w, row_w - col_w, col_w - row_w)
    incl_w = ahead_w >= 0
    strict_w = ahead_w > 0
    prob_w = lane_w // C
    lane_p = lax.broadcasted_iota(jnp.int32, (C, LANES), 1)
    row_p = lax.broadcasted_iota(jnp.int32, (C, LANES), 0)
    first_p = lane_p < C
    lane_s = lax.broadcasted_iota(jnp.int32, (1, 2 * LANES), 1)
    lo_s = lane_s < LANES

    def block_diag(a):
        return jnp.concatenate([jnp.where(prob_w == p, a, 0.0) for p in range(4)], axis=0)

    def chunk_rows(ci):
        return ci * C if isinstance(ci, int) else pl.multiple_of(ci * C, C)

    def pair_blocks(a):
        lo = lax.broadcasted_iota(jnp.int32, a.shape, 1) < LANES
        return jnp.concatenate([jnp.where(lo, a, 0.0), jnp.where(lo, 0.0, a)], axis=0)

    def split2(a):
        hi = a.astype(BF16)
        return hi, (a - hi.astype(F32)).astype(BF16)

    def precompute(src, steps):
        q_ref, k_ref, v_ref, gb_ref, gt_ref = src
        nchunks = q_ref.shape[0] // C
        todo = range(len(steps))
        st = []
        for n, _ in steps:
            per_dir = []
            for d in range(2):
                ci = n if d == 0 else nchunks - 1 - n
                r0 = chunk_rows(ci)
                q = q_ref[pl.ds(r0, C), :]
                k = k_ref[pl.ds(r0, C), :]
                v = v_ref[pl.ds(r0, C), :].astype(F32)
                gb = gb_ref[pl.ds(r0, C), :]
                gt = gt_ref[ci]
                g_lane = d * 2 * GDN_V_HEADS + 2 * jh
                cols = []
                for lane0 in (g_lane, g_lane + GDN_V_HEADS):
                    e0 = jnp.sum(jnp.where(lane_p == lane0, gb, 0.0), axis=1, keepdims=True)
                    e1 = jnp.sum(jnp.where(lane_p == lane0 + 1, gb, 0.0), axis=1, keepdims=True)
                    cols.append((e0, e1))
                (gc0, gc1), (be0, be1) = cols
                g_row = d * GDN_V_HEADS + 2 * jh
                pick = row_p == jnp.where(first_p, g_row, g_row + 1)
                gc_row = jnp.sum(jnp.where(pick, gt, 0.0), axis=0, keepdims=True)
                kq = _dot_nt(jnp.concatenate([k, q], axis=0), jnp.concatenate([k, k], axis=0))
                last = C - 1 if d == 0 else 0
                per_dir.append(dict(q=q.astype(F32), k=k.astype(F32), v=v, gc=(gc0, gc1), be=(be0, be1),
                                    gc_col=jnp.where(first_p, gc0, gc1), be_col=jnp.where(first_p, be0, be1),
                                    gc_row=gc_row, kk=kq[0:C], qk=kq[C:2 * C],
                                    tot=(gc0[last:last + 1], gc1[last:last + 1])))
            st.append(per_dir)
        yield

        ms, qkms = [], []
        for per_dir in st:
            wide = lambda key: jnp.concatenate([per_dir[0][key], per_dir[1][key]], axis=1)
            dm = wide("gc_col") - wide("gc_row")
            decay = jnp.where(incl_w, jnp.exp(jnp.where(incl_w, dm, 0.0)), 0.0)
            ms.append(-jnp.where(strict_w, wide("be_col") * wide("kk") * decay, 0.0))
            qkms.append(wide("qk") * decay)

        xs = list(ms)
        for lvl in range(6):
            lhs = [ms[i] if lvl == 0 else xs[i] if lvl == 5 else jnp.concatenate([ms[i], xs[i]], axis=0)
                   for i in todo]
            if GDN_INV_PASSES == 3:
                lhs = [split2(a) for a in lhs]
                rhs = [split2(block_diag(ms[i])) for i in todo]
                prod = [_dot(lhs[i][0], rhs[i][0]) + _dot(lhs[i][1], rhs[i][0]) + _dot(lhs[i][0], rhs[i][1])
                        for i in todo]
            else:
                prod = [_dot(lhs[i].astype(BF16), block_diag(ms[i].astype(BF16))) for i in todo]
            yield
            for i in todo:
                if lvl == 0:
                    ms[i] = prod[i]
                elif lvl < 5:
                    xs[i] = xs[i] + ms[i] + prod[i][C:2 * C]
                    ms[i] = prod[i][0:C]
                else:
                    xs[i] = xs[i] + ms[i] + prod[i]

        rhs_all = []
        for per_dir in st:
            rhs = []
            for d in range(2):
                pd = per_dir[d]
                for e in range(2):
                    be, egc = pd["be"][e], jnp.exp(pd["gc"][e])
                    rhs.append(jnp.concatenate([be * pd["v"][:, e * HEAD:(e + 1) * HEAD], (be * egc) * pd["k"]],
                                               axis=1))
            rhs_all.append(jnp.concatenate(rhs, axis=0))
        xr = [_dot(block_diag(xs[i]).astype(BF16), rhs_all[i].astype(BF16)) for i in todo]
        yield

        for i in todo:
            uw = rhs_all[i] + xr[i]
            slot = steps[i][1]
            for d in range(2):
                pd = st[i][d]
                u = jnp.concatenate([uw[(2 * d + e) * C:(2 * d + e + 1) * C, 0:HEAD] for e in range(2)], axis=1)
                w = jnp.concatenate([uw[(2 * d + e) * C:(2 * d + e + 1) * C, HEAD:] for e in range(2)], axis=1)
                qdec = jnp.concatenate([pd["q"] * jnp.exp(pd["gc"][e]) for e in range(2)], axis=1)
                tot0, tot1 = pd["tot"]
                kscale = jnp.concatenate([jnp.broadcast_to(jnp.exp(tot0 - pd["gc"][0]), (C, HEAD)),
                                          jnp.broadcast_to(jnp.exp(tot1 - pd["gc"][1]), (C, HEAD))], axis=1)
                wq_ref[slot, d] = jnp.concatenate([w, qdec], axis=0).astype(BF16)
                u_ref[slot, d] = u
                qkm_ref[slot, d] = qkms[i][:, d * LANES:(d + 1) * LANES].astype(BF16)
                kt_ref[slot, d] = pd["k"].T.astype(BF16)
                ksc_ref[slot, d] = kscale
                gtot_ref[slot, d] = jnp.where(lo_s, jnp.exp(tot0), jnp.exp(tot1))
        yield

    def recur(o_ref, steps):
        nchunks = o_ref.shape[0] // C
        for n, slot in steps:
            state = [s_ref[d] for d in range(2)]
            ws_qs = [_dot(wq_ref[slot, d], pair_blocks(state[d]).astype(BF16)) for d in range(2)]
            yield
            v_new = [u_ref[slot, d] - ws_qs[d][0:C] for d in range(2)]
            for d in range(2):
                ci = n if d == 0 else nchunks - 1 - n
                o = ws_qs[d][C:2 * C] + _dot(qkm_ref[slot, d], pair_blocks(v_new[d]).astype(BF16))
                o_ref[pl.ds(chunk_rows(ci), C), :] += o
                ds = _dot(kt_ref[slot, d], (v_new[d] * ksc_ref[slot, d]).astype(BF16))
                s_ref[d] = state[d] * gtot_ref[slot, d] + ds
            yield

    def interleave(*gens):
        gens = list(gens)
        while gens:
            for g in list(gens):
                if next(g, gens) is gens:
                    gens.remove(g)

    def phase(q_ref, k_ref, v_ref, gb_ref, gt_ref, o_ref):
        nchunks = q_ref.shape[0] // C
        unroll = math.gcd(nchunks, GDN_UNROLL)
        groups = nchunks // unroll
        src = (q_ref, k_ref, v_ref, gb_ref, gt_ref)
        interleave(precompute(src, [(t, t) for t in range(unroll)]))

        def body(g, carry):
            base = (g % 2) * unroll
            nxt = unroll - base
            ahead = [(jnp.minimum((g + 1) * unroll + t, nchunks - 1), nxt + t) for t in range(unroll)]
            interleave(recur(o_ref, [(g * unroll + t, base + t) for t in range(unroll)]), precompute(src, ahead))
            return carry

        if groups > 1:
            lax.fori_loop(0, groups, body, 0)
        else:
            interleave(recur(o_ref, [(t, t) for t in range(unroll)]))

    phase(qc_ref, kc_ref, vc_ref, gbc_ref, gtc_ref, oc_ref)
    phase(ql_ref, kl_ref, vl_ref, gbl_ref, gtl_ref, ol_ref)


def _gdn_core(qkv_c, gb_c, gt_c, qkv_l, gb_l, gt_l, batch):
    lc = qkv_c.shape[0] // batch
    ls = qkv_l.shape[0] // batch
    nqk = GDN_QK_HEADS
    pair = 2 * HEAD
    slots = 2 * GDN_UNROLL

    def specs(rows):
        return [
            pl.BlockSpec((rows, HEAD), lambda b, j: (b, j)),
            pl.BlockSpec((rows, HEAD), lambda b, j: (b, nqk + j)),
            pl.BlockSpec((rows, pair), lambda b, j: (b, nqk + j)),
            pl.BlockSpec((rows, LANES), lambda b, j: (b, 0)),
            pl.BlockSpec((rows // GDN_CHUNK, 2 * GDN_V_HEADS, LANES), lambda b, j: (b, 0, 0)),
        ]

    return pl.pallas_call(
        _gdn_core_kernel,
        grid=(batch, nqk),
        in_specs=specs(lc) + specs(ls),
        out_specs=[pl.BlockSpec((lc, pair), lambda b, j: (b, j)),
                   pl.BlockSpec((ls, pair), lambda b, j: (b, j))],
        out_shape=[jax.ShapeDtypeStruct((batch * lc, GDN_V_W), F32),
                   jax.ShapeDtypeStruct((batch * ls, GDN_V_W), F32)],
        scratch_shapes=[
            pltpu.VMEM((2, HEAD, pair), F32),
            pltpu.VMEM((slots, 2, 2 * GDN_CHUNK, pair), BF16),
            pltpu.VMEM((slots, 2, GDN_CHUNK, pair), F32),
            pltpu.VMEM((slots, 2, GDN_CHUNK, HEAD), BF16),
            pltpu.VMEM((slots, 2, HEAD, GDN_CHUNK), BF16),
            pltpu.VMEM((slots, 2, GDN_CHUNK, pair), F32),
            pltpu.VMEM((slots, 2, 1, pair), F32),
        ],
        compiler_params=_params(("arbitrary", "arbitrary")),
        name="gdn_core",
    )(qkv_c, qkv_c, qkv_c, gb_c, gt_c, qkv_l, qkv_l, qkv_l, gb_l, gt_l)


def _rope_tables(n_tokens, head_dim):
    rows = n_tokens // GRID_W
    r = jnp.repeat(jnp.arange(rows, dtype=F32), GRID_W)
    col = jnp.tile(jnp.arange(GRID_W, dtype=F32), rows)
    d_axis = head_dim // 2
    inv = ROPE_THETA ** (-jnp.arange(0, d_axis, 2, dtype=F32) / d_axis)
    ang = jnp.concatenate([r[:, None] * inv, col[:, None] * inv], axis=-1)
    cos = jnp.repeat(jnp.cos(ang), 2, axis=-1)
    sin = jnp.repeat(jnp.sin(ang), 2, axis=-1) * jnp.tile(jnp.array([-1.0, 1.0], F32), head_dim // 2)
    reps = LANES // head_dim
    return jnp.tile(cos, (1, reps)), jnp.tile(sin, (1, reps))


def _lane_row(v, reps=1):
    return jnp.tile(v.astype(F32), reps).reshape(1, -1)


def kernel(x, c, ctx, c_ctx, norm_g, mod_w, mod_b, gdn_w_in, gdn_conv_w, gdn_a_log, gdn_dt_bias, gdn_out_norm, gdn_w_out, gqa_w_in, gqa_q_norm, gqa_k_norm, gqa_w_out, pool_w_in, pool_w_grp, pool_scale, pool_w_out, diff_w_in, diff_q_norm, diff_k_norm, diff_lambda_q1, diff_lambda_k1, diff_lambda_q2, diff_lambda_k2, diff_sub_norm, diff_w_out):
    batch, seq, d = x.shape
    lctx = ctx.shape[1]
    depth = norm_g.shape[0]
    lat = x.reshape(batch * seq, d)
    cx = ctx.reshape(batch * lctx, d)

    mod_rows = -(-(batch + 1) // 8) * 8
    c_all = jnp.concatenate([c, c_ctx[None, :], jnp.zeros((mod_rows - batch - 1, d), F32)], axis=0)
    mod3 = _mod_call(c_all, mod_w, mod_b).reshape(depth * mod_rows * 3, 1, d)

    rope_gqa = _rope_tables(seq, HEAD)
    rope_diff = _rope_tables(seq, DIFF_DH)

    for i in range(depth):
        m, jl = i % N_MIXERS, i // N_MIXERS
        need_ctx = i < depth - 1
        base_l, base_c = i * mod_rows, i * mod_rows + batch
        g_row = norm_g[i].reshape(1, d)
        lat_args = (lat, mod3, base_l, seq, g_row)
        ctx_args = (cx, mod3, base_c, None, g_row)
        if m == 0:
            w_in = gdn_w_in[jl]
            w_main = w_in[:, :GDN_CONV_CH + GDN_V_W].astype(BF16)
            w_ab = w_in[:, GDN_CONV_CH + GDN_V_W:].astype(BF16)
            conv_w = gdn_conv_w[jl]
            zeros = jnp.zeros((GDN_V_HEADS,), F32)
            alog_row = jnp.concatenate([gdn_a_log[jl, 0], zeros, gdn_a_log[jl, 1], zeros]).reshape(1, LANES)
            dtb_row = jnp.concatenate([gdn_dt_bias[jl, 0], zeros, gdn_dt_bias[jl, 1], zeros]).reshape(1, LANES)
            p_l, ab_l = _proj(*lat_args, w_main, kind="gdn", wab=w_ab)
            p_c, ab_c = _proj(*ctx_args, w_main, kind="gdn", wab=w_ab)
            qkv_l, gb_l, gt_l = _gdn_prep(p_l, ab_l, conv_w, alog_row, dtb_row, seq)
            qkv_c, gb_c, gt_c = _gdn_prep(p_c, ab_c, conv_w, alog_row, dtb_row, lctx)
            o_c, o_l = _gdn_core(qkv_c, gb_c, gt_c, qkv_l, gb_l, gt_l, batch)
            w_out = gdn_w_out[jl].astype(BF16)
            norm_row = _lane_row(gdn_out_norm[jl])
            lat = _out_proj(o_l, w_out, lat, mod3, base_l, seq, gdn_zp=p_l, gdn_norm=norm_row)
            if need_ctx:
                cx = _out_proj(o_c, w_out, cx, mod3, base_c, None, gdn_zp=p_c, gdn_norm=norm_row)
        elif m == 1:
            w_in = gqa_w_in[jl].astype(BF16)
            heads = (_lane_row(gqa_q_norm[jl]), _lane_row(gqa_k_norm[jl]), GQA_QW, GQA_KVW, HEAD,
                     HEAD ** -0.5 * LOG2E)
            p_l = _proj(*lat_args, w_in, kind="heads", heads=heads, rope_tabs=rope_gqa)
            p_c = _proj(*ctx_args, w_in, kind="heads", heads=heads)
            w_out = gqa_w_out[jl].astype(BF16)
            y_l = _gqa_attention(p_l, p_c, p_l, batch)
            lat = _out_proj(y_l, w_out, lat, mod3, base_l, seq)
            if need_ctx:
                y_c = _gqa_attention(p_c, p_c, None, batch)
                cx = _out_proj(y_c, w_out, cx, mod3, base_c, None)
        elif m == 2:
            w_in = pool_w_in[jl].astype(BF16)
            w_grp = pool_w_grp[jl].astype(BF16)
            sc_row = pool_scale[jl].reshape(1, d)
            w_out = pool_w_out[jl].astype(BF16)
            y_l = _pool_mix(_proj(*lat_args, w_in), w_grp, sc_row, seq)
            lat = _out_proj(y_l, w_out, lat, mod3, base_l, seq)
            if need_ctx:
                y_c = _pool_mix(_proj(*ctx_args, w_in), w_grp, sc_row, lctx)
                cx = _out_proj(y_c, w_out, cx, mod3, base_c, None)
        else:
            lam_init = 0.8 - 0.6 * math.exp(-0.3 * i)
            w_in = diff_w_in[jl].astype(BF16)
            heads = (_lane_row(diff_q_norm[jl], 2), _lane_row(diff_k_norm[jl], 2), DIFF_W, DIFF_W, DIFF_DH,
                     DIFF_DH ** -0.5 * LOG2E)
            p_l = _proj(*lat_args, w_in, kind="heads", heads=heads, rope_tabs=rope_diff)
            p_c = _proj(*ctx_args, w_in, kind="heads", heads=heads, j0=DIFF_W // 512, nj=2 * DIFF_W // 512)
            lams = [v[jl].reshape(1, DIFF_DH) for v in (diff_lambda_q1, diff_lambda_k1, diff_lambda_q2, diff_lambda_k2)]
            y_l = _diff_attention(p_l, p_c, lams, _lane_row(diff_sub_norm[jl]), batch, lam_init)
            lat = _out_proj(y_l, diff_w_out[jl].astype(BF16), lat, mod3, base_l, seq)
    return lat.reshape(batch, seq, d)
```

```python
import functools
import math

import jax
import jax.numpy as jnp
from jax import lax
from jax.experimental import pallas as pl
from jax.experimental.pallas import tpu as pltpu

F32 = jnp.float32
BF16 = jnp.bfloat16

D_MODEL = 2048
N_MIXERS = 4
RMS_EPS = 1e-6
L2_EPS = 1e-6
ROPE_THETA = 10000.0
GRID_W = 64

HEAD = 128
KV_TILE = 2048
EPILOGUE_ROWS = 256
LOG2E = math.log2(math.e)
GDN_QK_HEADS = D_MODEL // HEAD
GDN_V_HEADS = 2 * GDN_QK_HEADS
GDN_CONV = 5
GDN_CHUNK = 64
GDN_UNROLL = 4
GDN_INV_PASSES = 3
GDN_QK_W = GDN_QK_HEADS * HEAD
GDN_V_W = GDN_V_HEADS * HEAD
GDN_CONV_CH = 2 * GDN_QK_W + GDN_V_W

GQA_HEADS = D_MODEL // HEAD
GQA_KV_HEADS = GQA_HEADS // 4
GQA_QW = GQA_HEADS * HEAD
GQA_KVW = GQA_KV_HEADS * HEAD

POOL_WINDOWS = (2, 4, 8, 16)
POOL_GW = D_MODEL // len(POOL_WINDOWS)
POOL_HALO = 16

DIFF_DH = 64
DIFF_HEADS = D_MODEL // (2 * DIFF_DH)
DIFF_W = DIFF_HEADS * 2 * DIFF_DH

VMEM_LIMIT_BYTES = 56 * 1024 * 1024
LANES = 128
BF16_SUBLANES = 16
CONV_HALO = 16


def _params(sem):
    return pltpu.CompilerParams(dimension_semantics=sem, vmem_limit_bytes=VMEM_LIMIT_BYTES)


def _tile(n, pref):
    t = min(n, pref)
    while n % t or t % BF16_SUBLANES:
        t -= 1
    return t


def _silu(x):
    return x * jax.nn.sigmoid(x)


def _softplus(x):
    return jnp.maximum(x, 0.0) + jnp.log(1.0 + jnp.exp(-jnp.abs(x)))


def _dot(a, b):
    return jnp.dot(a, b, preferred_element_type=F32)


def _dot_nt(a, b):
    return lax.dot_general(a, b, (((1,), (1,)), ((), ())), preferred_element_type=F32)


def _swap_pairs(x):
    lane = lax.broadcasted_iota(jnp.int32, x.shape, 1)
    return jnp.where(lane % 2 == 0, pltpu.roll(x, LANES - 1, 1), pltpu.roll(x, 1, 1))


def _group_mean_sq(x, group):
    x2 = x * x
    if group == LANES:
        return jnp.mean(x2, axis=-1, keepdims=True)
    lo = lax.broadcasted_iota(jnp.int32, x.shape, 1) < group
    s_lo = jnp.sum(jnp.where(lo, x2, 0.0), axis=-1, keepdims=True)
    s_hi = jnp.sum(jnp.where(lo, 0.0, x2), axis=-1, keepdims=True)
    return jnp.where(lo, s_lo, s_hi) * (1.0 / group)


def _mod_kernel(c_ref, w_ref, b_ref, o_ref):
    a = _silu(c_ref[...]).astype(BF16)
    o_ref[0] = _dot(a, w_ref[0].astype(BF16)) + b_ref[0]


def _mod_call(c_all, mod_w, mod_b):
    depth, d, n = mod_w.shape
    rows = c_all.shape[0]
    tn = _tile(n, 768)
    return pl.pallas_call(
        _mod_kernel,
        grid=(depth, n // tn),
        in_specs=[
            pl.BlockSpec((rows, d), lambda l, j: (0, 0)),
            pl.BlockSpec((1, d, tn), lambda l, j: (l, 0, j)),
            pl.BlockSpec((1, 1, tn), lambda l, j: (l, 0, j)),
        ],
        out_specs=pl.BlockSpec((1, rows, tn), lambda l, j: (l, 0, j)),
        out_shape=jax.ShapeDtypeStruct((depth, rows, n), F32),
        compiler_params=_params(("arbitrary", "arbitrary")),
        name="mod_vectors",
    )(c_all, mod_w, mod_b.reshape(depth, 1, n))


def _head_epilogue(acc, o_ref, gain, cs, scale, group):
    for c in range(acc.shape[1] // LANES):
        xc = acc[:, c * LANES:(c + 1) * LANES]
        y = xc * lax.rsqrt(_group_mean_sq(xc, group) + RMS_EPS) * gain
        if cs is not None:
            y = y * cs[0] + _swap_pairs(y) * cs[1]
        if scale != 1.0:
            y = y * scale
        o_ref[:, c * LANES:(c + 1) * LANES] = y.astype(o_ref.dtype)


def _proj_kernel(*refs, kind, rope, j0, q_tiles, k_tiles, group, q_scale):
    x_ref, g_ref, sh_ref, sc_ref, w_ref = refs[:5]
    rest = list(refs[5:])
    h_ref = rest.pop()
    qn_ref = kn_ref = c_ref = s_ref = wab_ref = ab_ref = None
    if kind == "heads":
        qn_ref, kn_ref = rest.pop(0), rest.pop(0)
        if rope:
            c_ref, s_ref = rest.pop(0), rest.pop(0)
    if kind == "gdn":
        wab_ref = rest.pop(0)
        o_ref, ab_ref = rest
    else:
        (o_ref,) = rest
    j = pl.program_id(1)

    @pl.when(j == 0)
    def _():
        x = x_ref[...]
        ms = jnp.mean(x * x, axis=-1, keepdims=True)
        y = x * lax.rsqrt(ms + RMS_EPS) * g_ref[...]
        h = y * (1.0 + sc_ref[0]) + sh_ref[0]
        h_ref[...] = h.astype(BF16)
        if kind == "gdn":
            ab_ref[...] = _dot(h_ref[...], wab_ref[...])

    if kind != "heads":
        o_ref[...] = _dot(h_ref[...], w_ref[...]).astype(o_ref.dtype)
        return
    jg = j + j0
    tm = h_ref.shape[0]
    rs = math.gcd(tm, EPILOGUE_ROWS)

    def by_rows(gain_ref, scale):
        pending = None
        for r in range(0, tm + rs, rs):
            acc = _dot(h_ref[r:r + rs, :], w_ref[...]) if r < tm else None
            if pending is not None:
                p0, p_acc = pending
                cs = (c_ref[p0:p0 + rs, :], s_ref[p0:p0 + rs, :]) if rope else None
                _head_epilogue(p_acc, o_ref.at[p0:p0 + rs, :], gain_ref[...], cs, scale, group)
            pending = (r, acc)

    @pl.when(jg < q_tiles)
    def _():
        by_rows(qn_ref, q_scale)

    @pl.when(jnp.logical_and(jg >= q_tiles, jg < q_tiles + k_tiles))
    def _():
        by_rows(kn_ref, 1.0)

    @pl.when(jg >= q_tiles + k_tiles)
    def _():
        o_ref[...] = _dot(h_ref[...], w_ref[...]).astype(o_ref.dtype)


def _proj(xs, mod3, mod_base, batch_rows, g_row, w, *, kind="plain", heads=None, rope_tabs=None,
          wab=None, j0=0, nj=None, tn=512, tm_pref=1024):
    m, d = xs.shape
    seq = batch_rows if batch_rows is not None else m
    tm = _tile(seq if rope_tabs is not None or batch_rows is not None else m, tm_pref)
    n_total = w.shape[1] // tn
    nj = n_total - j0 if nj is None else nj
    tiles_per_batch = (batch_rows // tm) if batch_rows is not None else None

    def mod_idx(which):
        if tiles_per_batch is None:
            return lambda i, j: (mod_base * 3 + which, 0, 0)
        return lambda i, j: ((mod_base + i // tiles_per_batch) * 3 + which, 0, 0)

    in_specs = [
        pl.BlockSpec((tm, d), lambda i, j: (i, 0)),
        pl.BlockSpec((1, d), lambda i, j: (0, 0)),
        pl.BlockSpec((1, 1, d), mod_idx(0)),
        pl.BlockSpec((1, 1, d), mod_idx(1)),
        pl.BlockSpec((d, tn), lambda i, j: (0, j + j0)),
    ]
    args = [xs, g_row, mod3, mod3, w]
    kw = dict(kind=kind, rope=rope_tabs is not None, j0=j0, q_tiles=0, k_tiles=0, group=LANES, q_scale=1.0)
    if kind == "heads":
        qn, kn, q_cols, k_cols, group, q_scale = heads
        kw.update(q_tiles=q_cols // tn, k_tiles=k_cols // tn, group=group, q_scale=q_scale)
        in_specs += [pl.BlockSpec((1, LANES), lambda i, j: (0, 0))] * 2
        args += [qn, kn]
        if rope_tabs is not None:
            tps = seq // tm
            in_specs += [pl.BlockSpec((tm, LANES), lambda i, j: (i % tps, 0))] * 2
            args += list(rope_tabs)
    out_specs = pl.BlockSpec((tm, tn), lambda i, j: (i, j))
    out_shape = jax.ShapeDtypeStruct((m, nj * tn), BF16)
    if kind == "gdn":
        in_specs.append(pl.BlockSpec((d, LANES), lambda i, j: (0, 0)))
        args.append(wab)
        out_specs = [out_specs, pl.BlockSpec((tm, LANES), lambda i, j: (i, 0))]
        out_shape = [out_shape, jax.ShapeDtypeStruct((m, LANES), F32)]
    return pl.pallas_call(
        functools.partial(_proj_kernel, **kw),
        grid=(m // tm, nj),
        in_specs=in_specs,
        out_specs=out_specs,
        out_shape=out_shape,
        scratch_shapes=[pltpu.VMEM((tm, d), BF16)],
        compiler_params=_params(("arbitrary", "arbitrary")),
        name="proj_" + kind,
    )(*args)


def _out_kernel(*refs, gdn):
    if gdn:
        o_ref, z_ref, n_ref, w_ref, x_ref, gate_ref, out_ref, y_ref = refs

        @pl.when(pl.program_id(1) == 0)
        def _():
            gain = n_ref[...]
            for c in range(o_ref.shape[1] // LANES):
                sl = slice(c * LANES, (c + 1) * LANES)
                o = o_ref[:, sl]
                y = o * lax.rsqrt(jnp.mean(o * o, axis=-1, keepdims=True) + RMS_EPS) * gain
                y_ref[:, sl] = (y * _silu(z_ref[:, sl].astype(F32))).astype(BF16)
    else:
        y_ref, w_ref, x_ref, gate_ref, out_ref = refs
    step = math.gcd(out_ref.shape[1], 512)
    for c in range(0, out_ref.shape[1], step):
        sl = slice(c, c + step)
        out_ref[:, sl] = x_ref[:, sl] + gate_ref[0][:, sl] * _dot(y_ref[...], w_ref[:, sl])


def _out_proj(y, w, xs, mod3, mod_base, batch_rows, *, gdn_zp=None, gdn_norm=None, tm_pref=512):
    m, d = xs.shape
    kdim = w.shape[0]
    tn = d if gdn_zp is None and 2 * w.size * w.dtype.itemsize <= VMEM_LIMIT_BYTES // 3 else 512
    tm = _tile(batch_rows if batch_rows is not None else m, tm_pref)
    tiles_per_batch = (batch_rows // tm) if batch_rows is not None else None
    if tiles_per_batch is None:
        gate_idx = lambda i, j: (mod_base * 3 + 2, 0, 0)
    else:
        gate_idx = lambda i, j: ((mod_base + i // tiles_per_batch) * 3 + 2, 0, 0)
    in_specs = [pl.BlockSpec((tm, kdim), lambda i, j: (i, 0))]
    args = [y]
    scratch = []
    if gdn_zp is not None:
        z_blk = GDN_CONV_CH // kdim
        in_specs += [pl.BlockSpec((tm, kdim), lambda i, j: (i, z_blk)),
                     pl.BlockSpec((1, LANES), lambda i, j: (0, 0))]
        args += [gdn_zp, gdn_norm]
        scratch = [pltpu.VMEM((tm, kdim), BF16)]
    in_specs += [
        pl.BlockSpec((kdim, tn), lambda i, j: (0, j)),
        pl.BlockSpec((tm, tn), lambda i, j: (i, j)),
        pl.BlockSpec((1, 1, tn), lambda i, j: gate_idx(i, j)[:2] + (j,)),
    ]
    args += [w, xs, mod3]
    return pl.pallas_call(
        functools.partial(_out_kernel, gdn=gdn_zp is not None),
        grid=(m // tm, d // tn),
        in_specs=in_specs,
        out_specs=pl.BlockSpec((tm, tn), lambda i, j: (i, j)),
        out_shape=jax.ShapeDtypeStruct((m, d), F32),
        scratch_shapes=scratch,
        compiler_params=_params(("arbitrary", "arbitrary")),
        name="out_proj_gdn" if gdn_zp is not None else "out_proj",
    )(*args)


def _flash(queries, kv_refs):
    tq = queries[0].shape[0]
    m = [jnp.full((tq, 1), -1e30, F32) for _ in queries]
    l = [jnp.zeros((tq, 1), F32) for _ in queries]
    acc = [jnp.zeros((tq, HEAD), F32) for _ in queries]
    for k_ref, v_ref in kv_refs:
        rows = k_ref.shape[0]
        tk = math.gcd(rows, KV_TILE)
        for t in range(rows // tk):
            k = k_ref[t * tk:(t + 1) * tk, :]
            v = v_ref[t * tk:(t + 1) * tk, :]
            for i, q in enumerate(queries):
                s = _dot_nt(q, k)
                m_new = jnp.maximum(m[i], jnp.max(s, axis=-1, keepdims=True))
                alpha = jnp.exp2(m[i] - m_new)
                p = jnp.exp2(s - m_new)
                l[i] = alpha * l[i] + jnp.sum(p, axis=-1, keepdims=True)
                acc[i] = alpha * acc[i] + _dot(p.astype(BF16), v)
                m[i] = m_new
    return [a / li for a, li in zip(acc, l)]


def _gqa_attn_kernel(*refs, has_lat):
    if has_lat:
        q_ref, z_ref, kc_ref, vc_ref, kl_ref, vl_ref, y_ref = refs
        kv = [(kc_ref, vc_ref), (kl_ref, vl_ref)]
    else:
        q_ref, z_ref, kc_ref, vc_ref, y_ref = refs
        kv = [(kc_ref, vc_ref)]
    tq = q_ref.shape[0]
    heads = [slice(g * HEAD, (g + 1) * HEAD) for g in range(q_ref.shape[1] // HEAD)]
    (o,) = _flash([jnp.concatenate([q_ref[:, sl] for sl in heads], axis=0)], kv)
    for g, sl in enumerate(heads):
        y_ref[:, sl] = (o[g * tq:(g + 1) * tq] * _silu(z_ref[:, sl].astype(F32))).astype(BF16)


def _gqa_attention(p_q, p_c, p_l, batch, *, tq_pref=256):
    has_lat = p_l is not None
    m = p_q.shape[0]
    rows_q = m // batch
    tq = _tile(rows_q, tq_pref)
    nq = rows_q // tq
    lc = p_c.shape[0] // batch
    grp = GQA_QW // GQA_KV_HEADS
    k_blk, v_blk, z_blk = GQA_QW // HEAD, (GQA_QW + GQA_KVW) // HEAD, (GQA_QW + 2 * GQA_KVW) // grp
    in_specs = [
        pl.BlockSpec((tq, grp), lambda b, h, i: (b * nq + i, h)),
        pl.BlockSpec((tq, grp), lambda b, h, i: (b * nq + i, z_blk + h)),
        pl.BlockSpec((lc, HEAD), lambda b, h, i: (b, k_blk + h)),
        pl.BlockSpec((lc, HEAD), lambda b, h, i: (b, v_blk + h)),
    ]
    args = [p_q, p_q, p_c, p_c]
    if has_lat:
        ls = p_l.shape[0] // batch
        in_specs += [pl.BlockSpec((ls, HEAD), lambda b, h, i: (b, k_blk + h)),
                     pl.BlockSpec((ls, HEAD), lambda b, h, i: (b, v_blk + h))]
        args += [p_l, p_l]
    return pl.pallas_call(
        functools.partial(_gqa_attn_kernel, has_lat=has_lat),
        grid=(batch, GQA_KV_HEADS, nq),
        in_specs=in_specs,
        out_specs=pl.BlockSpec((tq, grp), lambda b, h, i: (b * nq + i, h)),
        out_shape=jax.ShapeDtypeStruct((m, GQA_QW), BF16),
        compiler_params=_params(("arbitrary", "arbitrary", "arbitrary")),
        name="gqa_attention",
    )(*args)


def _diff_attn_kernel(lq1_ref, lk1_ref, lq2_ref, lk2_ref, n_ref, q_ref, z_ref, kc_ref, vc_ref, kl_ref, vl_ref,
                      y_ref, *, lam_init):
    lam = (jnp.exp(jnp.sum(lq1_ref[...] * lk1_ref[...], axis=-1, keepdims=True))
           - jnp.exp(jnp.sum(lq2_ref[...] * lk2_ref[...], axis=-1, keepdims=True)) + lam_init)
    q = q_ref[...]
    lo = lax.broadcasted_iota(jnp.int32, q.shape, 1) < DIFF_DH
    zero = jnp.zeros_like(q)
    tq = q.shape[0]
    (o01,) = _flash([jnp.concatenate([jnp.where(lo, q, zero), jnp.where(lo, zero, q)], axis=0)],
                    [(kc_ref, vc_ref), (kl_ref, vl_ref)])
    o = o01[0:tq] - lam * o01[tq:2 * tq]
    y = o * lax.rsqrt(jnp.mean(o * o, axis=-1, keepdims=True) + RMS_EPS) * n_ref[...] * (1.0 - lam_init)
    y_ref[...] = (y * _silu(z_ref[...].astype(F32))).astype(BF16)


def _diff_attention(p_l, p_c, lams, sub_norm, batch, lam_init, *, tq_pref=512):
    m = p_l.shape[0]
    ls = m // batch
    lc = p_c.shape[0] // batch
    tq = _tile(ls, tq_pref)
    nq = ls // tq
    nh = DIFF_HEADS
    small = pl.BlockSpec((1, DIFF_DH), lambda b, h, i: (0, 0))
    in_specs = [small] * 4 + [
        pl.BlockSpec((1, HEAD), lambda b, h, i: (0, 0)),
        pl.BlockSpec((tq, HEAD), lambda b, h, i: (b * nq + i, h)),
        pl.BlockSpec((tq, HEAD), lambda b, h, i: (b * nq + i, 3 * nh + h)),
        pl.BlockSpec((lc, HEAD), lambda b, h, i: (b, h)),
        pl.BlockSpec((lc, HEAD), lambda b, h, i: (b, nh + h)),
        pl.BlockSpec((ls, HEAD), lambda b, h, i: (b, nh + h)),
        pl.BlockSpec((ls, HEAD), lambda b, h, i: (b, 2 * nh + h)),
    ]
    return pl.pallas_call(
        functools.partial(_diff_attn_kernel, lam_init=lam_init),
        grid=(batch, nh, nq),
        in_specs=in_specs,
        out_specs=pl.BlockSpec((tq, HEAD), lambda b, h, i: (b * nq + i, h)),
        out_shape=jax.ShapeDtypeStruct((m, DIFF_W), BF16),
        compiler_params=_params(("arbitrary", "arbitrary", "arbitrary")),
        name="diff_attention",
    )(*lams, sub_norm, p_l, p_l, p_c, p_c, p_l, p_l)


def _pool_kernel(u_ref, up_ref, un_ref, z_ref, wg_ref, sc_ref, y_ref, ext_ref, *, seq):
    tm = u_ref.shape[0]
    tps = seq // tm
    pos = pl.program_id(0) % tps
    has_prev = (pos != 0).astype(F32)
    has_next = (pos != tps - 1).astype(F32)
    ext_ref[0:POOL_HALO, :] = up_ref[...].astype(F32) * has_prev
    ext_ref[POOL_HALO:POOL_HALO + tm, :] = u_ref[...].astype(F32)
    ext_ref[POOL_HALO + tm:, :] = un_ref[...].astype(F32) * has_next
    t = pos * tm + lax.broadcasted_iota(jnp.int32, (tm, 1), 0)
    for gi, w in enumerate(POOL_WINDOWS):
        cols = slice(gi * POOL_GW, (gi + 1) * POOL_GW)
        tot = ext_ref[POOL_HALO - w // 2:POOL_HALO - w // 2 + tm, cols]
        for k in range(1, w):
            tot = tot + ext_ref[POOL_HALO - w // 2 + k:POOL_HALO - w // 2 + k + tm, cols]
        cnt = jnp.minimum(t - w // 2 + w, seq) - jnp.maximum(t - w // 2, 0)
        dgrp = tot / cnt.astype(F32) - ext_ref[POOL_HALO:POOL_HALO + tm, cols]
        r = _dot(dgrp.astype(BF16), wg_ref[gi]) * sc_ref[:, cols]
        y_ref[:, cols] = (r * _silu(z_ref[:, cols].astype(F32))).astype(BF16)


def _pool_mix(p, w_grp, ch_scale, seq):
    m = p.shape[0]
    w_u = D_MODEL
    tm = _tile(seq, 256)
    hb = tm // POOL_HALO
    last = m // POOL_HALO - 1
    return pl.pallas_call(
        functools.partial(_pool_kernel, seq=seq),
        grid=(m // tm,),
        in_specs=[
            pl.BlockSpec((tm, w_u), lambda i: (i, 0)),
            pl.BlockSpec((POOL_HALO, w_u), lambda i: (jnp.maximum(i * hb - 1, 0), 0)),
            pl.BlockSpec((POOL_HALO, w_u), lambda i: (jnp.minimum((i + 1) * hb, last), 0)),
            pl.BlockSpec((tm, w_u), lambda i: (i, 1)),
            pl.BlockSpec(w_grp.shape, lambda i: (0, 0, 0)),
            pl.BlockSpec((1, w_u), lambda i: (0, 0)),
        ],
        out_specs=pl.BlockSpec((tm, w_u), lambda i: (i, 0)),
        out_shape=jax.ShapeDtypeStruct((m, w_u), BF16),
        scratch_shapes=[pltpu.VMEM((tm + 2 * POOL_HALO, w_u), F32)],
        compiler_params=_params(("arbitrary",)),
        name="pool_mix",
    )(p, p, p, p, w_grp, ch_scale)


def _split3(x):
    hi = x.astype(BF16)
    r1 = x - hi.astype(F32)
    mid = r1.astype(BF16)
    lo = (r1 - mid.astype(F32)).astype(BF16)
    return hi, mid, lo


def _gdn_prep_kernel(p_ref, pp_ref, pn_ref, cw_ref, ab_ref, alog_ref, dtb_ref, o_ref, gb_ref, gt_ref, ext_ref, *, seq):
    tm = p_ref.shape[0]
    tps = seq // tm
    pos = pl.program_id(0) % tps
    j = pl.program_id(1)
    has_prev = (pos != 0).astype(F32)
    has_next = (pos != tps - 1).astype(F32)
    ext_ref[0:CONV_HALO, :] = pp_ref[...].astype(F32) * has_prev
    ext_ref[CONV_HALO:CONV_HALO + tm, :] = p_ref[...].astype(F32)
    ext_ref[CONV_HALO + tm:, :] = pn_ref[...].astype(F32) * has_next
    base = CONV_HALO - GDN_CONV // 2
    acc = cw_ref[0:1, :] * ext_ref[base:base + tm, :]
    for k in range(1, GDN_CONV):
        acc = acc + cw_ref[k:k + 1, :] * ext_ref[base + k:base + k + tm, :]
    y = _silu(acc)
    q_tiles = GDN_QK_W // o_ref.shape[1]

    def l2(scale):
        for c in range(o_ref.shape[1] // HEAD):
            sl = slice(c * HEAD, (c + 1) * HEAD)
            yc = y[:, sl]
            yn = yc * lax.rsqrt(jnp.sum(yc * yc, axis=-1, keepdims=True) + L2_EPS)
            o_ref[:, sl] = (yn * scale).astype(BF16)

    @pl.when(j < q_tiles)
    def _():
        l2(HEAD ** -0.5)

    @pl.when(jnp.logical_and(j >= q_tiles, j < 2 * q_tiles))
    def _():
        l2(1.0)

    @pl.when(j >= 2 * q_tiles)
    def _():
        o_ref[...] = y.astype(BF16)

    @pl.when(j == 0)
    def _():
        ab = ab_ref[...]
        lane = lax.broadcasted_iota(jnp.int32, (1, LANES), 1)
        is_g = (lane % (2 * GDN_V_HEADS)) < GDN_V_HEADS
        gval = -jnp.exp(alog_ref[...]) * _softplus(ab + dtb_ref[...])
        g = jnp.where(is_g, gval, 0.0)
        r = lax.broadcasted_iota(jnp.int32, (tm, tm), 0)
        c = lax.broadcasted_iota(jnp.int32, (tm, tm), 1)
        same = (r // GDN_CHUNK) == (c // GDN_CHUNK)
        tri_f = jnp.where(jnp.logical_and(same, r >= c), 1.0, 0.0).astype(BF16)
        tri_b = jnp.where(jnp.logical_and(same, r <= c), 1.0, 0.0).astype(BF16)
        pieces = _split3(g)
        gc_f = functools.reduce(jnp.add, [_dot(tri_f, x) for x in pieces])
        gc_b = functools.reduce(jnp.add, [_dot(tri_b, x) for x in pieces])
        gc = jnp.where(lane < 2 * GDN_V_HEADS, gc_f, gc_b)
        gb_ref[...] = jnp.where(is_g, gc, jax.nn.sigmoid(ab))
        for ci in range(tm // GDN_CHUNK):
            tile = gc[ci * GDN_CHUNK:(ci + 1) * GDN_CHUNK, :]
            tt = jnp.concatenate([tile, tile], axis=0).T
            gt_ref[ci, 0:GDN_V_HEADS, :] = tt[0:GDN_V_HEADS, :]
            gt_ref[ci, GDN_V_HEADS:, :] = tt[2 * GDN_V_HEADS:3 * GDN_V_HEADS, :]


def _gdn_prep(p, ab, conv_w, alog_row, dtb_row, seq, *, tn=512):
    m = p.shape[0]
    tm = _tile(seq, 512)
    hb = tm // CONV_HALO
    last = m // CONV_HALO - 1
    return pl.pallas_call(
        functools.partial(_gdn_prep_kernel, seq=seq),
        grid=(m // tm, GDN_CONV_CH // tn),
        in_specs=[
            pl.BlockSpec((tm, tn), lambda i, j: (i, j)),
            pl.BlockSpec((CONV_HALO, tn), lambda i, j: (jnp.maximum(i * hb - 1, 0), j)),
            pl.BlockSpec((CONV_HALO, tn), lambda i, j: (jnp.minimum((i + 1) * hb, last), j)),
            pl.BlockSpec((GDN_CONV, tn), lambda i, j: (0, j)),
            pl.BlockSpec((tm, LANES), lambda i, j: (i, 0)),
            pl.BlockSpec((1, LANES), lambda i, j: (0, 0)),
            pl.BlockSpec((1, LANES), lambda i, j: (0, 0)),
        ],
        out_specs=[
            pl.BlockSpec((tm, tn), lambda i, j: (i, j)),
            pl.BlockSpec((tm, LANES), lambda i, j: (i, 0)),
            pl.BlockSpec((tm // GDN_CHUNK, 2 * GDN_V_HEADS, LANES), lambda i, j: (i, 0, 0)),
        ],
        out_shape=[
            jax.ShapeDtypeStruct((m, GDN_CONV_CH), BF16),
            jax.ShapeDtypeStruct((m, LANES), F32),
            jax.ShapeDtypeStruct((m // GDN_CHUNK, 2 * GDN_V_HEADS, LANES), F32),
        ],
        scratch_shapes=[pltpu.VMEM((tm + 2 * CONV_HALO, tn), F32)],
        compiler_params=_params(("arbitrary", "arbitrary")),
        name="gdn_prep",
    )(p, p, p, conv_w, ab, alog_row, dtb_row)


def _gdn_core_kernel(qc_ref, kc_ref, vc_ref, gbc_ref, gtc_ref, ql_ref, kl_ref, vl_ref, gbl_ref, gtl_ref,
                     oc_ref, ol_ref, s_ref, wq_ref, u_ref, qkm_ref, kt_ref, ksc_ref, gtot_ref):
    C = GDN_CHUNK
    jh = pl.program_id(1)
    s_ref[...] = jnp.zeros_like(s_ref)
    oc_ref[...] = jnp.zeros_like(oc_ref)
    ol_ref[...] = jnp.zeros_like(ol_ref)

    lane_w = lax.broadcasted_iota(jnp.int32, (C, 2 * LANES), 1)
    row_w = lax.broadcasted_iota(jnp.int32, (C, 2 * LANES), 0)
    col_w = lane_w % C
    fwd_w = lane_w < LANES
    ahead_w = jnp.where(fwd_w, row_w - col_w, col_w - row_w)
    incl_w = ahead_w >= 0
    strict_w = ahead_w > 0
    prob_w = lane_w // C
    prob_masks = [jnp.where(prob_w == p, 1.0, 0.0).astype(BF16) for p in range(4)]
    lane_p = lax.broadcasted_iota(jnp.int32, (C, LANES), 1)
    row_p = lax.broadcasted_iota(jnp.int32, (C, LANES), 0)
    first_p = lane_p < C
    lane_s = lax.broadcasted_iota(jnp.int32, (1, 2 * LANES), 1)
    lo_s = lane_s < LANES

    def block_diag(a):
        return jnp.concatenate([a * mk for mk in prob_masks], axis=0)

    def chunk_rows(ci):
        return ci * C if isinstance(ci, int) else pl.multiple_of(ci * C, C)

    def pair_blocks(a):
        zero = jnp.zeros((a.shape[0], LANES), a.dtype)
        return jnp.concatenate([jnp.concatenate([a[:, :LANES], zero], axis=1),
                                jnp.concatenate([zero, a[:, LANES:]], axis=1)], axis=0)

    def split2(a):
        hi = a.astype(BF16)
        return hi, (a - hi.astype(F32)).astype(BF16)

    def precompute(src, steps):
        q_ref, k_ref, v_ref, gb_ref, gt_ref = src
        nchunks = q_ref.shape[0] // C
        todo = range(len(steps))
        st = []
        for n, _ in steps:
            per_dir = []
            for d in range(2):
                ci = n if d == 0 else nchunks - 1 - n
                r0 = chunk_rows(ci)
                q = q_ref[pl.ds(r0, C), :]
                k = k_ref[pl.ds(r0, C), :]
                v = v_ref[pl.ds(r0, C), :].astype(F32)
                gb = gb_ref[pl.ds(r0, C), :]
                gt = gt_ref[ci]
                g_lane = d * 2 * GDN_V_HEADS + 2 * jh
                cols = []
                for lane0 in (g_lane, g_lane + GDN_V_HEADS):
                    e0 = jnp.sum(jnp.where(lane_p == lane0, gb, 0.0), axis=1, keepdims=True)
                    e1 = jnp.sum(jnp.where(lane_p == lane0 + 1, gb, 0.0), axis=1, keepdims=True)
                    cols.append((e0, e1))
                (gc0, gc1), (be0, be1) = cols
                g_row = d * GDN_V_HEADS + 2 * jh
                pick = row_p == jnp.where(first_p, g_row, g_row + 1)
                gc_row = jnp.sum(jnp.where(pick, gt, 0.0), axis=0, keepdims=True)
                kq = _dot_nt(jnp.concatenate([k, q], axis=0), jnp.concatenate([k, k], axis=0))
                last = C - 1 if d == 0 else 0
                per_dir.append(dict(q=q.astype(F32), k=k.astype(F32), v=v, gc=(gc0, gc1), be=(be0, be1),
                                    gc_col=jnp.where(first_p, gc0, gc1), be_col=jnp.where(first_p, be0, be1),
                                    gc_row=gc_row, kk=kq[0:C], qk=kq[C:2 * C],
                                    tot=(gc0[last:last + 1], gc1[last:last + 1])))
            st.append(per_dir)
        yield

        ms, qkms = [], []
        for per_dir in st:
            wide = lambda key: jnp.concatenate([per_dir[0][key], per_dir[1][key]], axis=1)
            dm = wide("gc_col") - wide("gc_row")
            decay = jnp.where(incl_w, jnp.exp(jnp.where(incl_w, dm, 0.0)), 0.0)
            ms.append(-jnp.where(strict_w, wide("be_col") * wide("kk") * decay, 0.0))
            qkms.append(wide("qk") * decay)

        xs = list(ms)
        for lvl in range(6):
            lhs = [ms[i] if lvl == 0 else xs[i] if lvl == 5 else jnp.concatenate([ms[i], xs[i]], axis=0)
                   for i in todo]
            if GDN_INV_PASSES == 3:
                lhs = [split2(a) for a in lhs]
                rhs = [tuple(block_diag(piece) for piece in split2(ms[i])) for i in todo]
                prod = [_dot(lhs[i][0], rhs[i][0]) + _dot(lhs[i][1], rhs[i][0]) + _dot(lhs[i][0], rhs[i][1])
                        for i in todo]
            else:
                prod = [_dot(lhs[i].astype(BF16), block_diag(ms[i].astype(BF16))) for i in todo]
            yield
            for i in todo:
                if lvl == 0:
                    ms[i] = prod[i]
                elif lvl < 5:
                    xs[i] = xs[i] + ms[i] + prod[i][C:2 * C]
                    ms[i] = prod[i][0:C]
                else:
                    xs[i] = xs[i] + ms[i] + prod[i]

        rhs_all = []
        for per_dir in st:
            rhs = []
            for d in range(2):
                pd = per_dir[d]
                for e in range(2):
                    be, egc = pd["be"][e], jnp.exp(pd["gc"][e])
                    rhs.append(jnp.concatenate([be * pd["v"][:, e * HEAD:(e + 1) * HEAD], (be * egc) * pd["k"]],
                                               axis=1))
            rhs_all.append(jnp.concatenate(rhs, axis=0))
        xr = [_dot(block_diag(xs[i].astype(BF16)), rhs_all[i].astype(BF16)) for i in todo]
        yield

        for i in todo:
            uw = rhs_all[i] + xr[i]
            slot = steps[i][1]
            for d in range(2):
                pd = st[i][d]
                u = jnp.concatenate([uw[(2 * d + e) * C:(2 * d + e + 1) * C, 0:HEAD] for e in range(2)], axis=1)
                w = jnp.concatenate([uw[(2 * d + e) * C:(2 * d + e + 1) * C, HEAD:] for e in range(2)], axis=1)
                qdec = jnp.concatenate([pd["q"] * jnp.exp(pd["gc"][e]) for e in range(2)], axis=1)
                tot0, tot1 = pd["tot"]
                kscale = jnp.concatenate([jnp.broadcast_to(jnp.exp(tot0 - pd["gc"][0]), (C, HEAD)),
                                          jnp.broadcast_to(jnp.exp(tot1 - pd["gc"][1]), (C, HEAD))], axis=1)
                wq_ref[slot, d] = jnp.concatenate([w, qdec], axis=0).astype(BF16)
                u_ref[slot, d] = u
                qkm_ref[slot, d] = qkms[i][:, d * LANES:(d + 1) * LANES].astype(BF16)
                kt_ref[slot, d] = pd["k"].T.astype(BF16)
                ksc_ref[slot, d] = kscale
                gtot_ref[slot, d] = jnp.where(lo_s, jnp.exp(tot0), jnp.exp(tot1))
        yield

    def recur(o_ref, steps):
        nchunks = o_ref.shape[0] // C
        for n, slot in steps:
            state = [s_ref[d] for d in range(2)]
            ws_qs = [_dot(wq_ref[slot, d], pair_blocks(state[d].astype(BF16))) for d in range(2)]
            yield
            v_new = [u_ref[slot, d] - ws_qs[d][0:C] for d in range(2)]
            for d in range(2):
                ci = n if d == 0 else nchunks - 1 - n
                o = ws_qs[d][C:2 * C] + _dot(qkm_ref[slot, d], pair_blocks(v_new[d].astype(BF16)))
                o_ref[pl.ds(chunk_rows(ci), C), :] += o
                ds = _dot(kt_ref[slot, d], (v_new[d] * ksc_ref[slot, d]).astype(BF16))
                s_ref[d] = state[d] * gtot_ref[slot, d] + ds
            yield

    def interleave(*gens):
        gens = list(gens)
        while gens:
            for g in list(gens):
                if next(g, gens) is gens:
                    gens.remove(g)

    def phase(q_ref, k_ref, v_ref, gb_ref, gt_ref, o_ref):
        nchunks = q_ref.shape[0] // C
        unroll = math.gcd(nchunks, GDN_UNROLL)
        groups = nchunks // unroll
        src = (q_ref, k_ref, v_ref, gb_ref, gt_ref)
        interleave(precompute(src, [(t, t) for t in range(unroll)]))

        def body(g, carry):
            base = (g % 2) * unroll
            nxt = unroll - base
            ahead = [(jnp.minimum((g + 1) * unroll + t, nchunks - 1), nxt + t) for t in range(unroll)]
            interleave(recur(o_ref, [(g * unroll + t, base + t) for t in range(unroll)]), precompute(src, ahead))
            return carry

        if groups > 1:
            lax.fori_loop(0, groups, body, 0)
        else:
            interleave(recur(o_ref, [(t, t) for t in range(unroll)]))

    phase(qc_ref, kc_ref, vc_ref, gbc_ref, gtc_ref, oc_ref)
    phase(ql_ref, kl_ref, vl_ref, gbl_ref, gtl_ref, ol_ref)


def _gdn_core(qkv_c, gb_c, gt_c, qkv_l, gb_l, gt_l, batch):
    lc = qkv_c.shape[0] // batch
    ls = qkv_l.shape[0] // batch
    nqk = GDN_QK_HEADS
    pair = 2 * HEAD
    slots = 2 * GDN_UNROLL

    def specs(rows):
        return [
            pl.BlockSpec((rows, HEAD), lambda b, j: (b, j)),
            pl.BlockSpec((rows, HEAD), lambda b, j: (b, nqk + j)),
            pl.BlockSpec((rows, pair), lambda b, j: (b, nqk + j)),
            pl.BlockSpec((rows, LANES), lambda b, j: (b, 0)),
            pl.BlockSpec((rows // GDN_CHUNK, 2 * GDN_V_HEADS, LANES), lambda b, j: (b, 0, 0)),
        ]

    return pl.pallas_call(
        _gdn_core_kernel,
        grid=(batch, nqk),
        in_specs=specs(lc) + specs(ls),
        out_specs=[pl.BlockSpec((lc, pair), lambda b, j: (b, j)),
                   pl.BlockSpec((ls, pair), lambda b, j: (b, j))],
        out_shape=[jax.ShapeDtypeStruct((batch * lc, GDN_V_W), F32),
                   jax.ShapeDtypeStruct((batch * ls, GDN_V_W), F32)],
        scratch_shapes=[
            pltpu.VMEM((2, HEAD, pair), F32),
            pltpu.VMEM((slots, 2, 2 * GDN_CHUNK, pair), BF16),
            pltpu.VMEM((slots, 2, GDN_CHUNK, pair), F32),
            pltpu.VMEM((slots, 2, GDN_CHUNK, HEAD), BF16),
            pltpu.VMEM((slots, 2, HEAD, GDN_CHUNK), BF16),
            pltpu.VMEM((slots, 2, GDN_CHUNK, pair), F32),
            pltpu.VMEM((slots, 2, 1, pair), F32),
        ],
        compiler_params=_params(("arbitrary", "arbitrary")),
        name="gdn_core",
    )(qkv_c, qkv_c, qkv_c, gb_c, gt_c, qkv_l, qkv_l, qkv_l, gb_l, gt_l)


def _rope_tables(n_tokens, head_dim):
    rows = n_tokens // GRID_W
    r = jnp.repeat(jnp.arange(rows, dtype=F32), GRID_W)
    col = jnp.tile(jnp.arange(GRID_W, dtype=F32), rows)
    d_axis = head_dim // 2
    inv = ROPE_THETA ** (-jnp.arange(0, d_axis, 2, dtype=F32) / d_axis)
    ang = jnp.concatenate([r[:, None] * inv, col[:, None] * inv], axis=-1)
    cos = jnp.repeat(jnp.cos(ang), 2, axis=-1)
    sin = jnp.repeat(jnp.sin(ang), 2, axis=-1) * jnp.tile(jnp.array([-1.0, 1.0], F32), head_dim // 2)
    reps = LANES // head_dim
    return jnp.tile(cos, (1, reps)), jnp.tile(sin, (1, reps))


def _lane_row(v, reps=1):
    return jnp.tile(v.astype(F32), reps).reshape(1, -1)


def kernel(x, c, ctx, c_ctx, norm_g, mod_w, mod_b, gdn_w_in, gdn_conv_w, gdn_a_log, gdn_dt_bias, gdn_out_norm, gdn_w_out, gqa_w_in, gqa_q_norm, gqa_k_norm, gqa_w_out, pool_w_in, pool_w_grp, pool_scale, pool_w_out, diff_w_in, diff_q_norm, diff_k_norm, diff_lambda_q1, diff_lambda_k1, diff_lambda_q2, diff_lambda_k2, diff_sub_norm, diff_w_out):
    batch, seq, d = x.shape
    lctx = ctx.shape[1]
    depth = norm_g.shape[0]
    lat = x.reshape(batch * seq, d)
    cx = ctx.reshape(batch * lctx, d)

    mod_rows = -(-(batch + 1) // 8) * 8
    c_all = jnp.concatenate([c, c_ctx[None, :], jnp.zeros((mod_rows - batch - 1, d), F32)], axis=0)
    mod3 = _mod_call(c_all, mod_w, mod_b).reshape(depth * mod_rows * 3, 1, d)

    rope_gqa = _rope_tables(seq, HEAD)
    rope_diff = _rope_tables(seq, DIFF_DH)

    for i in range(depth):
        m, jl = i % N_MIXERS, i // N_MIXERS
        need_ctx = i < depth - 1
        base_l, base_c = i * mod_rows, i * mod_rows + batch
        g_row = norm_g[i].reshape(1, d)
        lat_args = (lat, mod3, base_l, seq, g_row)
        ctx_args = (cx, mod3, base_c, None, g_row)
        if m == 0:
            w_in = gdn_w_in[jl]
            w_main = w_in[:, :GDN_CONV_CH + GDN_V_W].astype(BF16)
            w_ab = w_in[:, GDN_CONV_CH + GDN_V_W:].astype(BF16)
            conv_w = gdn_conv_w[jl]
            zeros = jnp.zeros((GDN_V_HEADS,), F32)
            alog_row = jnp.concatenate([gdn_a_log[jl, 0], zeros, gdn_a_log[jl, 1], zeros]).reshape(1, LANES)
            dtb_row = jnp.concatenate([gdn_dt_bias[jl, 0], zeros, gdn_dt_bias[jl, 1], zeros]).reshape(1, LANES)
            p_l, ab_l = _proj(*lat_args, w_main, kind="gdn", wab=w_ab)
            p_c, ab_c = _proj(*ctx_args, w_main, kind="gdn", wab=w_ab)
            qkv_l, gb_l, gt_l = _gdn_prep(p_l, ab_l, conv_w, alog_row, dtb_row, seq)
            qkv_c, gb_c, gt_c = _gdn_prep(p_c, ab_c, conv_w, alog_row, dtb_row, lctx)
            o_c, o_l = _gdn_core(qkv_c, gb_c, gt_c, qkv_l, gb_l, gt_l, batch)
            w_out = gdn_w_out[jl].astype(BF16)
            norm_row = _lane_row(gdn_out_norm[jl])
            lat = _out_proj(o_l, w_out, lat, mod3, base_l, seq, gdn_zp=p_l, gdn_norm=norm_row)
            if need_ctx:
                cx = _out_proj(o_c, w_out, cx, mod3, base_c, None, gdn_zp=p_c, gdn_norm=norm_row)
        elif m == 1:
            w_in = gqa_w_in[jl].astype(BF16)
            heads = (_lane_row(gqa_q_norm[jl]), _lane_row(gqa_k_norm[jl]), GQA_QW, GQA_KVW, HEAD,
                     HEAD ** -0.5 * LOG2E)
            p_l = _proj(*lat_args, w_in, kind="heads", heads=heads, rope_tabs=rope_gqa)
            p_c = _proj(*ctx_args, w_in, kind="heads", heads=heads)
            w_out = gqa_w_out[jl].astype(BF16)
            y_l = _gqa_attention(p_l, p_c, p_l, batch)
            lat = _out_proj(y_l, w_out, lat, mod3, base_l, seq)
            if need_ctx:
                y_c = _gqa_attention(p_c, p_c, None, batch)
                cx = _out_proj(y_c, w_out, cx, mod3, base_c, None)
        elif m == 2:
            w_in = pool_w_in[jl].astype(BF16)
            w_grp = pool_w_grp[jl].astype(BF16)
            sc_row = pool_scale[jl].reshape(1, d)
            w_out = pool_w_out[jl].astype(BF16)
            y_l = _pool_mix(_proj(*lat_args, w_in), w_grp, sc_row, seq)
            lat = _out_proj(y_l, w_out, lat, mod3, base_l, seq)
            if need_ctx:
                y_c = _pool_mix(_proj(*ctx_args, w_in), w_grp, sc_row, lctx)
                cx = _out_proj(y_c, w_out, cx, mod3, base_c, None)
        else:
            lam_init = 0.8 - 0.6 * math.exp(-0.3 * i)
            w_in = diff_w_in[jl].astype(BF16)
            heads = (_lane_row(diff_q_norm[jl], 2), _lane_row(diff_k_norm[jl], 2), DIFF_W, DIFF_W, DIFF_DH,
                     DIFF_DH ** -0.5 * LOG2E)
            p_l = _proj(*lat_args, w_in, kind="heads", heads=heads, rope_tabs=rope_diff)
            p_c = _proj(*ctx_args, w_in, kind="heads", heads=heads, j0=DIFF_W // 512, nj=2 * DIFF_W // 512)
            lams = [v[jl].reshape(1, DIFF_DH) for v in (diff_lambda_q1, diff_lambda_k1, diff_lambda_q2, diff_lambda_k2)]
            y_l = _diff_attention(p_l, p_c, lams, _lane_row(diff_sub_norm[jl]), batch, lam_init)
            lat = _out_proj(y_l, diff_w_out[jl].astype(BF16), lat, mod3, base_l, seq)
    return lat.reshape(batch, seq, d)
```

```python
import functools
import math

import jax
import jax.numpy as jnp
from jax import lax
from jax.experimental import pallas as pl
from jax.experimental.pallas import tpu as pltpu

F32 = jnp.float32
BF16 = jnp.bfloat16

D_MODEL = 2048
N_MIXERS = 4
RMS_EPS = 1e-6
L2_EPS = 1e-6
ROPE_THETA = 10000.0
GRID_W = 64

HEAD = 128
KV_TILE = 2048
EPILOGUE_ROWS = 256
LOG2E = math.log2(math.e)
GDN_QK_HEADS = D_MODEL // HEAD
GDN_V_HEADS = 2 * GDN_QK_HEADS
GDN_CONV = 5
GDN_CHUNK = 64
GDN_UNROLL = 4
GDN_INV_EXACT_LEVELS = 6
GDN_QK_W = GDN_QK_HEADS * HEAD
GDN_V_W = GDN_V_HEADS * HEAD
GDN_CONV_CH = 2 * GDN_QK_W + GDN_V_W

GQA_HEADS = D_MODEL // HEAD
GQA_KV_HEADS = GQA_HEADS // 4
GQA_QW = GQA_HEADS * HEAD
GQA_KVW = GQA_KV_HEADS * HEAD

POOL_WINDOWS = (2, 4, 8, 16)
POOL_GW = D_MODEL // len(POOL_WINDOWS)
POOL_HALO = 16

DIFF_DH = 64
DIFF_HEADS = D_MODEL // (2 * DIFF_DH)
DIFF_W = DIFF_HEADS * 2 * DIFF_DH

VMEM_LIMIT_BYTES = 56 * 1024 * 1024
LANES = 128
BF16_SUBLANES = 16
CONV_HALO = 16


def _params(sem):
    return pltpu.CompilerParams(dimension_semantics=sem, vmem_limit_bytes=VMEM_LIMIT_BYTES)


def _tile(n, pref):
    t = min(n, pref)
    while n % t or t % BF16_SUBLANES:
        t -= 1
    return t


def _silu(x):
    return x * jax.nn.sigmoid(x)


def _softplus(x):
    return jnp.maximum(x, 0.0) + jnp.log(1.0 + jnp.exp(-jnp.abs(x)))


def _dot(a, b):
    return jnp.dot(a, b, preferred_element_type=F32)


def _dot_nt(a, b):
    return lax.dot_general(a, b, (((1,), (1,)), ((), ())), preferred_element_type=F32)


def _swap_pairs(x):
    lane = lax.broadcasted_iota(jnp.int32, x.shape, 1)
    return jnp.where(lane % 2 == 0, pltpu.roll(x, LANES - 1, 1), pltpu.roll(x, 1, 1))


def _group_mean_sq(x, group):
    x2 = x * x
    if group == LANES:
        return jnp.mean(x2, axis=-1, keepdims=True)
    lo = lax.broadcasted_iota(jnp.int32, x.shape, 1) < group
    s_lo = jnp.sum(jnp.where(lo, x2, 0.0), axis=-1, keepdims=True)
    s_hi = jnp.sum(jnp.where(lo, 0.0, x2), axis=-1, keepdims=True)
    return jnp.where(lo, s_lo, s_hi) * (1.0 / group)


def _mod_kernel(c_ref, w_ref, b_ref, o_ref):
    a = _silu(c_ref[...]).astype(BF16)
    o_ref[0] = _dot(a, w_ref[0].astype(BF16)) + b_ref[0]


def _mod_call(c_all, mod_w, mod_b):
    depth, d, n = mod_w.shape
    rows = c_all.shape[0]
    tn = _tile(n, 768)
    return pl.pallas_call(
        _mod_kernel,
        grid=(depth, n // tn),
        in_specs=[
            pl.BlockSpec((rows, d), lambda l, j: (0, 0)),
            pl.BlockSpec((1, d, tn), lambda l, j: (l, 0, j)),
            pl.BlockSpec((1, 1, tn), lambda l, j: (l, 0, j)),
        ],
        out_specs=pl.BlockSpec((1, rows, tn), lambda l, j: (l, 0, j)),
        out_shape=jax.ShapeDtypeStruct((depth, rows, n), F32),
        compiler_params=_params(("arbitrary", "arbitrary")),
        name="mod_vectors",
    )(c_all, mod_w, mod_b.reshape(depth, 1, n))


def _head_epilogue(acc, o_ref, gain, cs, scale, group):
    for c in range(acc.shape[1] // LANES):
        xc = acc[:, c * LANES:(c + 1) * LANES]
        y = xc * lax.rsqrt(_group_mean_sq(xc, group) + RMS_EPS) * gain
        if cs is not None:
            y = y * cs[0] + _swap_pairs(y) * cs[1]
        if scale != 1.0:
            y = y * scale
        o_ref[:, c * LANES:(c + 1) * LANES] = y.astype(o_ref.dtype)


def _proj_kernel(*refs, kind, rope, j0, q_tiles, k_tiles, group, q_scale):
    x_ref, g_ref, sh_ref, sc_ref, w_ref = refs[:5]
    rest = list(refs[5:])
    h_ref = rest.pop()
    qn_ref = kn_ref = c_ref = s_ref = wab_ref = ab_ref = None
    if kind == "heads":
        qn_ref, kn_ref = rest.pop(0), rest.pop(0)
        if rope:
            c_ref, s_ref = rest.pop(0), rest.pop(0)
    if kind == "gdn":
        wab_ref = rest.pop(0)
        o_ref, ab_ref = rest
    else:
        (o_ref,) = rest
    j = pl.program_id(1)

    @pl.when(j == 0)
    def _():
        x = x_ref[...]
        ms = jnp.mean(x * x, axis=-1, keepdims=True)
        y = x * lax.rsqrt(ms + RMS_EPS) * g_ref[...]
        h = y * (1.0 + sc_ref[0]) + sh_ref[0]
        h_ref[...] = h.astype(BF16)
        if kind == "gdn":
            ab_ref[...] = _dot(h_ref[...], wab_ref[...])

    if kind != "heads":
        o_ref[...] = _dot(h_ref[...], w_ref[...]).astype(o_ref.dtype)
        return
    jg = j + j0
    tm = h_ref.shape[0]
    rs = math.gcd(tm, EPILOGUE_ROWS)

    def by_rows(gain_ref, scale):
        pending = None
        for r in range(0, tm + rs, rs):
            acc = _dot(h_ref[r:r + rs, :], w_ref[...]) if r < tm else None
            if pending is not None:
                p0, p_acc = pending
                cs = (c_ref[p0:p0 + rs, :], s_ref[p0:p0 + rs, :]) if rope else None
                _head_epilogue(p_acc, o_ref.at[p0:p0 + rs, :], gain_ref[...], cs, scale, group)
            pending = (r, acc)

    @pl.when(jg < q_tiles)
    def _():
        by_rows(qn_ref, q_scale)

    @pl.when(jnp.logical_and(jg >= q_tiles, jg < q_tiles + k_tiles))
    def _():
        by_rows(kn_ref, 1.0)

    @pl.when(jg >= q_tiles + k_tiles)
    def _():
        o_ref[...] = _dot(h_ref[...], w_ref[...]).astype(o_ref.dtype)


def _proj(xs, mod3, mod_base, batch_rows, g_row, w, *, kind="plain", heads=None, rope_tabs=None,
          wab=None, j0=0, nj=None, tn=512, tm_pref=1024):
    m, d = xs.shape
    seq = batch_rows if batch_rows is not None else m
    tm = _tile(seq if rope_tabs is not None or batch_rows is not None else m, tm_pref)
    n_total = w.shape[1] // tn
    nj = n_total - j0 if nj is None else nj
    tiles_per_batch = (batch_rows // tm) if batch_rows is not None else None

    def mod_idx(which):
        if tiles_per_batch is None:
            return lambda i, j: (mod_base * 3 + which, 0, 0)
        return lambda i, j: ((mod_base + i // tiles_per_batch) * 3 + which, 0, 0)

    in_specs = [
        pl.BlockSpec((tm, d), lambda i, j: (i, 0)),
        pl.BlockSpec((1, d), lambda i, j: (0, 0)),
        pl.BlockSpec((1, 1, d), mod_idx(0)),
        pl.BlockSpec((1, 1, d), mod_idx(1)),
        pl.BlockSpec((d, tn), lambda i, j: (0, j + j0)),
    ]
    args = [xs, g_row, mod3, mod3, w]
    kw = dict(kind=kind, rope=rope_tabs is not None, j0=j0, q_tiles=0, k_tiles=0, group=LANES, q_scale=1.0)
    if kind == "heads":
        qn, kn, q_cols, k_cols, group, q_scale = heads
        kw.update(q_tiles=q_cols // tn, k_tiles=k_cols // tn, group=group, q_scale=q_scale)
        in_specs += [pl.BlockSpec((1, LANES), lambda i, j: (0, 0))] * 2
        args += [qn, kn]
        if rope_tabs is not None:
            tps = seq // tm
            in_specs += [pl.BlockSpec((tm, LANES), lambda i, j: (i % tps, 0))] * 2
            args += list(rope_tabs)
    out_specs = pl.BlockSpec((tm, tn), lambda i, j: (i, j))
    out_shape = jax.ShapeDtypeStruct((m, nj * tn), BF16)
    if kind == "gdn":
        in_specs.append(pl.BlockSpec((d, LANES), lambda i, j: (0, 0)))
        args.append(wab)
        out_specs = [out_specs, pl.BlockSpec((tm, LANES), lambda i, j: (i, 0))]
        out_shape = [out_shape, jax.ShapeDtypeStruct((m, LANES), F32)]
    return pl.pallas_call(
        functools.partial(_proj_kernel, **kw),
        grid=(m // tm, nj),
        in_specs=in_specs,
        out_specs=out_specs,
        out_shape=out_shape,
        scratch_shapes=[pltpu.VMEM((tm, d), BF16)],
        compiler_params=_params(("arbitrary", "arbitrary")),
        name="proj_" + kind,
    )(*args)


def _proj_gdn_kernel(x_ref, xp_ref, xn_ref, g_ref, sh_ref, sc_ref, w_ref, wab_ref, cw_ref, o_ref, ab_ref,
                     h_ref, acc_ref, *, seq):
    tm, tn = o_ref.shape
    j = pl.program_id(1)
    tps = seq // tm
    pos = pl.program_id(0) % tps
    conv_tiles = GDN_CONV_CH // tn
    qk_tiles = GDN_QK_W // tn

    @pl.when(j == 0)
    def _():
        def norm_mod(x):
            ms = jnp.mean(x * x, axis=-1, keepdims=True)
            y = x * lax.rsqrt(ms + RMS_EPS) * g_ref[...]
            return (y * (1.0 + sc_ref[0]) + sh_ref[0]).astype(BF16)

        h_ref[0:CONV_HALO, :] = norm_mod(xp_ref[...])
        h_ref[CONV_HALO:CONV_HALO + tm, :] = norm_mod(x_ref[...])
        h_ref[CONV_HALO + tm:, :] = norm_mod(xn_ref[...])
        ab_ref[...] = _dot(h_ref[CONV_HALO:CONV_HALO + tm, :], wab_ref[...])

    @pl.when(j >= conv_tiles)
    def _():
        o_ref[...] = _dot(h_ref[CONV_HALO:CONV_HALO + tm, :], w_ref[...]).astype(BF16)

    @pl.when(j < conv_tiles)
    def _():
        has_prev = (pos != 0).astype(F32)
        has_next = (pos != tps - 1).astype(F32)
        rs = math.gcd(tm, EPILOGUE_ROWS)
        nb = tm // rs
        rows = tm + 2 * CONV_HALO
        base = CONV_HALO - GDN_CONV // 2
        is_qk = j < 2 * qk_tiles
        out_scale = jnp.where(j < qk_tiles, HEAD ** -0.5, 1.0)

        def matmul_rows(b):
            r0, r1 = b * rs, ((b + 1) * rs if b < nb - 1 else rows)
            acc_ref[r0:r1, :] = _dot(h_ref[r0:r1, :], w_ref[...])
            if b == 0:
                acc_ref[0:CONV_HALO, :] = acc_ref[0:CONV_HALO, :] * has_prev
            if b == nb - 1:
                acc_ref[CONV_HALO + tm:, :] = acc_ref[CONV_HALO + tm:, :] * has_next

        def conv_rows(b):
            r0 = base + b * rs
            y = cw_ref[0:1, :] * acc_ref[r0:r0 + rs, :]
            for k in range(1, GDN_CONV):
                y = y + cw_ref[k:k + 1, :] * acc_ref[r0 + k:r0 + k + rs, :]
            y = _silu(y)
            for c in range(tn // HEAD):
                sl = slice(c * HEAD, (c + 1) * HEAD)
                yc = y[:, sl]
                inv = lax.rsqrt(jnp.sum(yc * yc, axis=-1, keepdims=True) + L2_EPS) * out_scale
                o_ref[b * rs:(b + 1) * rs, sl] = (yc * jnp.where(is_qk, inv, 1.0)).astype(BF16)

        matmul_rows(0)
        for b in range(1, nb):
            matmul_rows(b)
            conv_rows(b - 1)
        conv_rows(nb - 1)


def _proj_gdn(xs, mod3, mod_base, batch_rows, g_row, w, wab, conv_w, seq, *, tn=512, tm_pref=1024):
    m, d = xs.shape
    tm = _tile(seq, tm_pref)
    hb = tm // CONV_HALO
    last = m // CONV_HALO - 1
    tiles_per_batch = (batch_rows // tm) if batch_rows is not None else None
    conv_last = GDN_CONV_CH // tn - 1

    def mod_idx(which):
        if tiles_per_batch is None:
            return lambda i, j: (mod_base * 3 + which, 0, 0)
        return lambda i, j: ((mod_base + i // tiles_per_batch) * 3 + which, 0, 0)

    return pl.pallas_call(
        functools.partial(_proj_gdn_kernel, seq=seq),
        grid=(m // tm, w.shape[1] // tn),
        in_specs=[
            pl.BlockSpec((tm, d), lambda i, j: (i, 0)),
            pl.BlockSpec((CONV_HALO, d), lambda i, j: (jnp.maximum(i * hb - 1, 0), 0)),
            pl.BlockSpec((CONV_HALO, d), lambda i, j: (jnp.minimum((i + 1) * hb, last), 0)),
            pl.BlockSpec((1, d), lambda i, j: (0, 0)),
            pl.BlockSpec((1, 1, d), mod_idx(0)),
            pl.BlockSpec((1, 1, d), mod_idx(1)),
            pl.BlockSpec((d, tn), lambda i, j: (0, j)),
            pl.BlockSpec((d, LANES), lambda i, j: (0, 0)),
            pl.BlockSpec((GDN_CONV, tn), lambda i, j: (0, jnp.minimum(j, conv_last))),
        ],
        out_specs=[pl.BlockSpec((tm, tn), lambda i, j: (i, j)),
                   pl.BlockSpec((tm, LANES), lambda i, j: (i, 0))],
        out_shape=[jax.ShapeDtypeStruct((m, w.shape[1]), BF16), jax.ShapeDtypeStruct((m, LANES), F32)],
        scratch_shapes=[pltpu.VMEM((tm + 2 * CONV_HALO, d), BF16), pltpu.VMEM((tm + 2 * CONV_HALO, tn), F32)],
        compiler_params=_params(("arbitrary", "arbitrary")),
        name="proj_gdn_conv",
    )(xs, xs, xs, g_row, mod3, mod3, w, wab, conv_w)


def _out_kernel(y_ref, w_ref, x_ref, gate_ref, out_ref):
    step = math.gcd(out_ref.shape[1], 512)
    for c in range(0, out_ref.shape[1], step):
        sl = slice(c, c + step)
        out_ref[:, sl] = x_ref[:, sl] + gate_ref[0][:, sl] * _dot(y_ref[...], w_ref[:, sl])


def _out_proj(y, w, xs, mod3, mod_base, batch_rows):
    m, d = xs.shape
    kdim = w.shape[0]
    resident = 2 * w.size * w.dtype.itemsize <= VMEM_LIMIT_BYTES // 3
    tn = d if resident else 512
    tm = _tile(batch_rows if batch_rows is not None else m, 512 if resident else 1024)
    tiles_per_batch = (batch_rows // tm) if batch_rows is not None else None
    if tiles_per_batch is None:
        gate_idx = lambda i, j: (mod_base * 3 + 2, 0, j)
    else:
        gate_idx = lambda i, j: ((mod_base + i // tiles_per_batch) * 3 + 2, 0, j)
    return pl.pallas_call(
        _out_kernel,
        grid=(m // tm, d // tn),
        in_specs=[
            pl.BlockSpec((tm, kdim), lambda i, j: (i, 0)),
            pl.BlockSpec((kdim, tn), lambda i, j: (0, j)),
            pl.BlockSpec((tm, tn), lambda i, j: (i, j)),
            pl.BlockSpec((1, 1, tn), gate_idx),
        ],
        out_specs=pl.BlockSpec((tm, tn), lambda i, j: (i, j)),
        out_shape=jax.ShapeDtypeStruct((m, d), F32),
        compiler_params=_params(("arbitrary", "arbitrary")),
        name="out_proj",
    )(y, w, xs, mod3)


def _flash(queries, kv_refs):
    tq = queries[0].shape[0]
    m = [jnp.full((tq, 1), -1e30, F32) for _ in queries]
    l = [jnp.zeros((tq, 1), F32) for _ in queries]
    acc = [jnp.zeros((tq, HEAD), F32) for _ in queries]
    for k_ref, v_ref in kv_refs:
        rows = k_ref.shape[0]
        tk = math.gcd(rows, KV_TILE)
        for t in range(rows // tk):
            k = k_ref[t * tk:(t + 1) * tk, :]
            v = v_ref[t * tk:(t + 1) * tk, :]
            for i, q in enumerate(queries):
                s = _dot_nt(q, k)
                m_new = jnp.maximum(m[i], jnp.max(s, axis=-1, keepdims=True))
                alpha = jnp.exp2(m[i] - m_new)
                p = jnp.exp2(s - m_new)
                l[i] = alpha * l[i] + jnp.sum(p, axis=-1, keepdims=True)
                acc[i] = alpha * acc[i] + _dot(p.astype(BF16), v)
                m[i] = m_new
    return [a / li for a, li in zip(acc, l)]


def _gqa_attn_kernel(*refs, has_lat):
    if has_lat:
        q_ref, z_ref, kc_ref, vc_ref, kl_ref, vl_ref, y_ref = refs
        kv = [(kc_ref, vc_ref), (kl_ref, vl_ref)]
    else:
        q_ref, z_ref, kc_ref, vc_ref, y_ref = refs
        kv = [(kc_ref, vc_ref)]
    tq = q_ref.shape[0]
    heads = [slice(g * HEAD, (g + 1) * HEAD) for g in range(q_ref.shape[1] // HEAD)]
    (o,) = _flash([jnp.concatenate([q_ref[:, sl] for sl in heads], axis=0)], kv)
    for g, sl in enumerate(heads):
        y_ref[:, sl] = (o[g * tq:(g + 1) * tq] * _silu(z_ref[:, sl].astype(F32))).astype(BF16)


def _gqa_attention(p_q, p_c, p_l, batch, *, tq_pref=256):
    has_lat = p_l is not None
    m = p_q.shape[0]
    rows_q = m // batch
    tq = _tile(rows_q, tq_pref)
    nq = rows_q // tq
    lc = p_c.shape[0] // batch
    grp = GQA_QW // GQA_KV_HEADS
    k_blk, v_blk, z_blk = GQA_QW // HEAD, (GQA_QW + GQA_KVW) // HEAD, (GQA_QW + 2 * GQA_KVW) // grp
    in_specs = [
        pl.BlockSpec((tq, grp), lambda b, h, i: (b * nq + i, h)),
        pl.BlockSpec((tq, grp), lambda b, h, i: (b * nq + i, z_blk + h)),
        pl.BlockSpec((lc, HEAD), lambda b, h, i: (b, k_blk + h)),
        pl.BlockSpec((lc, HEAD), lambda b, h, i: (b, v_blk + h)),
    ]
    args = [p_q, p_q, p_c, p_c]
    if has_lat:
        ls = p_l.shape[0] // batch
        in_specs += [pl.BlockSpec((ls, HEAD), lambda b, h, i: (b, k_blk + h)),
                     pl.BlockSpec((ls, HEAD), lambda b, h, i: (b, v_blk + h))]
        args += [p_l, p_l]
    return pl.pallas_call(
        functools.partial(_gqa_attn_kernel, has_lat=has_lat),
        grid=(batch, GQA_KV_HEADS, nq),
        in_specs=in_specs,
        out_specs=pl.BlockSpec((tq, grp), lambda b, h, i: (b * nq + i, h)),
        out_shape=jax.ShapeDtypeStruct((m, GQA_QW), BF16),
        compiler_params=_params(("arbitrary", "arbitrary", "arbitrary")),
        name="gqa_attention",
    )(*args)


def _diff_attn_kernel(lq1_ref, lk1_ref, lq2_ref, lk2_ref, n_ref, q_ref, z_ref, kc_ref, vc_ref, kl_ref, vl_ref,
                      y_ref, *, lam_init):
    lam = (jnp.exp(jnp.sum(lq1_ref[...] * lk1_ref[...], axis=-1, keepdims=True))
           - jnp.exp(jnp.sum(lq2_ref[...] * lk2_ref[...], axis=-1, keepdims=True)) + lam_init)
    q = q_ref[...]
    lo = lax.broadcasted_iota(jnp.int32, q.shape, 1) < DIFF_DH
    zero = jnp.zeros_like(q)
    tq = q.shape[0]
    (o01,) = _flash([jnp.concatenate([jnp.where(lo, q, zero), jnp.where(lo, zero, q)], axis=0)],
                    [(kc_ref, vc_ref), (kl_ref, vl_ref)])
    o = o01[0:tq] - lam * o01[tq:2 * tq]
    y = o * lax.rsqrt(jnp.mean(o * o, axis=-1, keepdims=True) + RMS_EPS) * n_ref[...] * (1.0 - lam_init)
    y_ref[...] = (y * _silu(z_ref[...].astype(F32))).astype(BF16)


def _diff_attention(p_l, p_c, lams, sub_norm, batch, lam_init, *, tq_pref=512):
    m = p_l.shape[0]
    ls = m // batch
    lc = p_c.shape[0] // batch
    tq = _tile(ls, tq_pref)
    nq = ls // tq
    nh = DIFF_HEADS
    small = pl.BlockSpec((1, DIFF_DH), lambda b, h, i: (0, 0))
    in_specs = [small] * 4 + [
        pl.BlockSpec((1, HEAD), lambda b, h, i: (0, 0)),
        pl.BlockSpec((tq, HEAD), lambda b, h, i: (b * nq + i, h)),
        pl.BlockSpec((tq, HEAD), lambda b, h, i: (b * nq + i, 3 * nh + h)),
        pl.BlockSpec((lc, HEAD), lambda b, h, i: (b, h)),
        pl.BlockSpec((lc, HEAD), lambda b, h, i: (b, nh + h)),
        pl.BlockSpec((ls, HEAD), lambda b, h, i: (b, nh + h)),
        pl.BlockSpec((ls, HEAD), lambda b, h, i: (b, 2 * nh + h)),
    ]
    return pl.pallas_call(
        functools.partial(_diff_attn_kernel, lam_init=lam_init),
        grid=(batch, nh, nq),
        in_specs=in_specs,
        out_specs=pl.BlockSpec((tq, HEAD), lambda b, h, i: (b * nq + i, h)),
        out_shape=jax.ShapeDtypeStruct((m, DIFF_W), BF16),
        compiler_params=_params(("arbitrary", "arbitrary", "arbitrary")),
        name="diff_attention",
    )(*lams, sub_norm, p_l, p_l, p_c, p_c, p_l, p_l)


def _pool_kernel(u_ref, up_ref, un_ref, z_ref, wg_ref, sc_ref, y_ref, ext_ref, *, seq):
    tm = u_ref.shape[0]
    tps = seq // tm
    pos = pl.program_id(0) % tps
    has_prev = (pos != 0).astype(F32)
    has_next = (pos != tps - 1).astype(F32)
    ext_ref[0:POOL_HALO, :] = up_ref[...].astype(F32) * has_prev
    ext_ref[POOL_HALO:POOL_HALO + tm, :] = u_ref[...].astype(F32)
    ext_ref[POOL_HALO + tm:, :] = un_ref[...].astype(F32) * has_next
    t = pos * tm + lax.broadcasted_iota(jnp.int32, (tm, 1), 0)
    for gi, w in enumerate(POOL_WINDOWS):
        cols = slice(gi * POOL_GW, (gi + 1) * POOL_GW)
        tot = ext_ref[POOL_HALO - w // 2:POOL_HALO - w // 2 + tm, cols]
        for k in range(1, w):
            tot = tot + ext_ref[POOL_HALO - w // 2 + k:POOL_HALO - w // 2 + k + tm, cols]
        cnt = jnp.minimum(t - w // 2 + w, seq) - jnp.maximum(t - w // 2, 0)
        dgrp = tot / cnt.astype(F32) - ext_ref[POOL_HALO:POOL_HALO + tm, cols]
        r = _dot(dgrp.astype(BF16), wg_ref[gi]) * sc_ref[:, cols]
        y_ref[:, cols] = (r * _silu(z_ref[:, cols].astype(F32))).astype(BF16)


def _pool_mix(p, w_grp, ch_scale, seq):
    m = p.shape[0]
    w_u = D_MODEL
    tm = _tile(seq, 256)
    hb = tm // POOL_HALO
    last = m // POOL_HALO - 1
    return pl.pallas_call(
        functools.partial(_pool_kernel, seq=seq),
        grid=(m // tm,),
        in_specs=[
            pl.BlockSpec((tm, w_u), lambda i: (i, 0)),
            pl.BlockSpec((POOL_HALO, w_u), lambda i: (jnp.maximum(i * hb - 1, 0), 0)),
            pl.BlockSpec((POOL_HALO, w_u), lambda i: (jnp.minimum((i + 1) * hb, last), 0)),
            pl.BlockSpec((tm, w_u), lambda i: (i, 1)),
            pl.BlockSpec(w_grp.shape, lambda i: (0, 0, 0)),
            pl.BlockSpec((1, w_u), lambda i: (0, 0)),
        ],
        out_specs=pl.BlockSpec((tm, w_u), lambda i: (i, 0)),
        out_shape=jax.ShapeDtypeStruct((m, w_u), BF16),
        scratch_shapes=[pltpu.VMEM((tm + 2 * POOL_HALO, w_u), F32)],
        compiler_params=_params(("arbitrary",)),
        name="pool_mix",
    )(p, p, p, p, w_grp, ch_scale)


def _split3(x):
    hi = x.astype(BF16)
    r1 = x - hi.astype(F32)
    mid = r1.astype(BF16)
    lo = (r1 - mid.astype(F32)).astype(BF16)
    return hi, mid, lo


def _gdn_gates_kernel(ab_ref, alog_ref, dtb_ref, gb_ref, gt_ref):
    tm = ab_ref.shape[0]
    ab = ab_ref[...]
    lane = lax.broadcasted_iota(jnp.int32, (1, LANES), 1)
    is_g = (lane % (2 * GDN_V_HEADS)) < GDN_V_HEADS
    gval = -jnp.exp(alog_ref[...]) * _softplus(ab + dtb_ref[...])
    g = jnp.where(is_g, gval, 0.0)
    r = lax.broadcasted_iota(jnp.int32, (tm, tm), 0)
    c = lax.broadcasted_iota(jnp.int32, (tm, tm), 1)
    same = (r // GDN_CHUNK) == (c // GDN_CHUNK)
    tri_f = jnp.where(jnp.logical_and(same, r >= c), 1.0, 0.0).astype(BF16)
    tri_b = jnp.where(jnp.logical_and(same, r <= c), 1.0, 0.0).astype(BF16)
    pieces = _split3(g)
    gc_f = functools.reduce(jnp.add, [_dot(tri_f, x) for x in pieces])
    gc_b = functools.reduce(jnp.add, [_dot(tri_b, x) for x in pieces])
    gc = jnp.where(lane < 2 * GDN_V_HEADS, gc_f, gc_b)
    gb_ref[...] = jnp.where(is_g, gc, jax.nn.sigmoid(ab))
    for ci in range(tm // GDN_CHUNK):
        tile = gc[ci * GDN_CHUNK:(ci + 1) * GDN_CHUNK, :]
        tt = jnp.concatenate([tile, tile], axis=0).T
        gt_ref[ci, 0:GDN_V_HEADS, :] = tt[0:GDN_V_HEADS, :]
        gt_ref[ci, GDN_V_HEADS:, :] = tt[2 * GDN_V_HEADS:3 * GDN_V_HEADS, :]


def _gdn_gates(ab, alog_row, dtb_row, seq):
    m = ab.shape[0]
    tm = _tile(seq, 512)
    return pl.pallas_call(
        _gdn_gates_kernel,
        grid=(m // tm,),
        in_specs=[
            pl.BlockSpec((tm, LANES), lambda i: (i, 0)),
            pl.BlockSpec((1, LANES), lambda i: (0, 0)),
            pl.BlockSpec((1, LANES), lambda i: (0, 0)),
        ],
        out_specs=[
            pl.BlockSpec((tm, LANES), lambda i: (i, 0)),
            pl.BlockSpec((tm // GDN_CHUNK, 2 * GDN_V_HEADS, LANES), lambda i: (i, 0, 0)),
        ],
        out_shape=[
            jax.ShapeDtypeStruct((m, LANES), F32),
            jax.ShapeDtypeStruct((m // GDN_CHUNK, 2 * GDN_V_HEADS, LANES), F32),
        ],
        compiler_params=_params(("arbitrary",)),
        name="gdn_gates",
    )(ab, alog_row, dtb_row)


def _gdn_core_kernel(qc_ref, kc_ref, vc_ref, gbc_ref, gtc_ref, zc_ref, ql_ref, kl_ref, vl_ref, gbl_ref, gtl_ref,
                     zl_ref, n_ref, yc_ref, yl_ref, oc_ref, ol_ref, s_ref, wq_ref, u_ref, qkm_ref, kt_ref,
                     ksc_ref, gtot_ref):
    C = GDN_CHUNK
    jh = pl.program_id(1)
    s_ref[...] = jnp.zeros_like(s_ref)
    oc_ref[...] = jnp.zeros_like(oc_ref)
    ol_ref[...] = jnp.zeros_like(ol_ref)

    lane_w = lax.broadcasted_iota(jnp.int32, (C, 2 * LANES), 1)
    row_w = lax.broadcasted_iota(jnp.int32, (C, 2 * LANES), 0)
    col_w = lane_w % C
    fwd_w = lane_w < LANES
    ahead_w = jnp.where(fwd_w, row_w - col_w, col_w - row_w)
    incl_w = ahead_w >= 0
    strict_w = ahead_w > 0
    prob_w = lane_w // C
    prob_masks = [jnp.where(prob_w == p, 1.0, 0.0).astype(BF16) for p in range(4)]
    lane_p = lax.broadcasted_iota(jnp.int32, (C, LANES), 1)
    row_p = lax.broadcasted_iota(jnp.int32, (C, LANES), 0)
    first_p = lane_p < C
    lane_s = lax.broadcasted_iota(jnp.int32, (1, 2 * LANES), 1)
    lo_s = lane_s < LANES

    def block_diag(a):
        return jnp.concatenate([a * mk for mk in prob_masks], axis=0)

    def chunk_rows(ci):
        return ci * C if isinstance(ci, int) else pl.multiple_of(ci * C, C)

    def pair_blocks(a):
        zero = jnp.zeros((a.shape[0], LANES), a.dtype)
        return jnp.concatenate([jnp.concatenate([a[:, :LANES], zero], axis=1),
                                jnp.concatenate([zero, a[:, LANES:]], axis=1)], axis=0)

    def split2(a):
        hi = a.astype(BF16)
        return hi, (a - hi.astype(F32)).astype(BF16)

    def precompute(src, steps):
        q_ref, k_ref, v_ref, gb_ref, gt_ref = src
        nchunks = q_ref.shape[0] // C
        todo = range(len(steps))
        st = []
        for n, _ in steps:
            per_dir = []
            for d in range(2):
                ci = n if d == 0 else nchunks - 1 - n
                r0 = chunk_rows(ci)
                q = q_ref[pl.ds(r0, C), :]
                k = k_ref[pl.ds(r0, C), :]
                v = v_ref[pl.ds(r0, C), :].astype(F32)
                gb = gb_ref[pl.ds(r0, C), :]
                gt = gt_ref[ci]
                g_lane = d * 2 * GDN_V_HEADS + 2 * jh
                cols = []
                for lane0 in (g_lane, g_lane + GDN_V_HEADS):
                    e0 = jnp.sum(jnp.where(lane_p == lane0, gb, 0.0), axis=1, keepdims=True)
                    e1 = jnp.sum(jnp.where(lane_p == lane0 + 1, gb, 0.0), axis=1, keepdims=True)
                    cols.append((e0, e1))
                (gc0, gc1), (be0, be1) = cols
                g_row = d * GDN_V_HEADS + 2 * jh
                pick = row_p == jnp.where(first_p, g_row, g_row + 1)
                gc_row = jnp.sum(jnp.where(pick, gt, 0.0), axis=0, keepdims=True)
                kq = _dot_nt(jnp.concatenate([k, q], axis=0), jnp.concatenate([k, k], axis=0))
                last = C - 1 if d == 0 else 0
                per_dir.append(dict(q=q.astype(F32), k=k.astype(F32), v=v, gc=(gc0, gc1), be=(be0, be1),
                                    gc_col=jnp.where(first_p, gc0, gc1), be_col=jnp.where(first_p, be0, be1),
                                    gc_row=gc_row, kk=kq[0:C], qk=kq[C:2 * C],
                                    tot=(gc0[last:last + 1], gc1[last:last + 1])))
            st.append(per_dir)
        yield

        ms, qkms = [], []
        for per_dir in st:
            wide = lambda key: jnp.concatenate([per_dir[0][key], per_dir[1][key]], axis=1)
            dm = wide("gc_col") - wide("gc_row")
            decay = jnp.where(incl_w, jnp.exp(jnp.where(incl_w, dm, 0.0)), 0.0)
            ms.append(-jnp.where(strict_w, wide("be_col") * wide("kk") * decay, 0.0))
            qkms.append(wide("qk") * decay)

        xs = list(ms)
        for lvl in range(6):
            lhs = [ms[i] if lvl == 0 else xs[i] if lvl == 5 else jnp.concatenate([ms[i], xs[i]], axis=0)
                   for i in todo]
            if lvl < GDN_INV_EXACT_LEVELS:
                lhs = [split2(a) for a in lhs]
                rhs = [tuple(block_diag(piece) for piece in split2(ms[i])) for i in todo]
                prod = [_dot(lhs[i][0], rhs[i][0]) + _dot(lhs[i][1], rhs[i][0]) + _dot(lhs[i][0], rhs[i][1])
                        for i in todo]
            else:
                prod = [_dot(lhs[i].astype(BF16), block_diag(ms[i].astype(BF16))) for i in todo]
            yield
            for i in todo:
                if lvl == 0:
                    ms[i] = prod[i]
                elif lvl < 5:
                    xs[i] = xs[i] + ms[i] + prod[i][C:2 * C]
                    ms[i] = prod[i][0:C]
                else:
                    xs[i] = xs[i] + ms[i] + prod[i]

        rhs_all = []
        for per_dir in st:
            rhs = []
            for d in range(2):
                pd = per_dir[d]
                for e in range(2):
                    be, egc = pd["be"][e], jnp.exp(pd["gc"][e])
                    rhs.append(jnp.concatenate([be * pd["v"][:, e * HEAD:(e + 1) * HEAD], (be * egc) * pd["k"]],
                                               axis=1))
            rhs_all.append(jnp.concatenate(rhs, axis=0))
        xr = [_dot(block_diag(xs[i].astype(BF16)), rhs_all[i].astype(BF16)) for i in todo]
        yield

        for i in todo:
            uw = rhs_all[i] + xr[i]
            slot = steps[i][1]
            for d in range(2):
                pd = st[i][d]
                u = jnp.concatenate([uw[(2 * d + e) * C:(2 * d + e + 1) * C, 0:HEAD] for e in range(2)], axis=1)
                w = jnp.concatenate([uw[(2 * d + e) * C:(2 * d + e + 1) * C, HEAD:] for e in range(2)], axis=1)
                qdec = jnp.concatenate([pd["q"] * jnp.exp(pd["gc"][e]) for e in range(2)], axis=1)
                tot0, tot1 = pd["tot"]
                kscale = jnp.concatenate([jnp.broadcast_to(jnp.exp(tot0 - pd["gc"][0]), (C, HEAD)),
                                          jnp.broadcast_to(jnp.exp(tot1 - pd["gc"][1]), (C, HEAD))], axis=1)
                wq_ref[slot, d] = jnp.concatenate([w, qdec], axis=0).astype(BF16)
                u_ref[slot, d] = u
                qkm_ref[slot, d] = qkms[i][:, d * LANES:(d + 1) * LANES].astype(BF16)
                kt_ref[slot, d] = pd["k"].T.astype(BF16)
                ksc_ref[slot, d] = kscale
                gtot_ref[slot, d] = jnp.where(lo_s, jnp.exp(tot0), jnp.exp(tot1))
        yield

    def recur(o_ref, steps):
        nchunks = o_ref.shape[0] // C
        for n, slot in steps:
            state = [s_ref[d] for d in range(2)]
            ws_qs = [_dot(wq_ref[slot, d], pair_blocks(state[d].astype(BF16))) for d in range(2)]
            yield
            v_new = [u_ref[slot, d] - ws_qs[d][0:C] for d in range(2)]
            for d in range(2):
                ci = n if d == 0 else nchunks - 1 - n
                o = ws_qs[d][C:2 * C] + _dot(qkm_ref[slot, d], pair_blocks(v_new[d].astype(BF16)))
                o_ref[pl.ds(chunk_rows(ci), C), :] += o
                ds = _dot(kt_ref[slot, d], (v_new[d] * ksc_ref[slot, d]).astype(BF16))
                s_ref[d] = state[d] * gtot_ref[slot, d] + ds
            yield

    def interleave(*gens):
        gens = list(gens)
        while gens:
            for g in list(gens):
                if next(g, gens) is gens:
                    gens.remove(g)

    def phase(q_ref, k_ref, v_ref, gb_ref, gt_ref, o_ref):
        nchunks = q_ref.shape[0] // C
        unroll = math.gcd(nchunks, GDN_UNROLL)
        groups = nchunks // unroll
        src = (q_ref, k_ref, v_ref, gb_ref, gt_ref)
        interleave(precompute(src, [(t, t) for t in range(unroll)]))

        def body(g, carry):
            base = (g % 2) * unroll
            nxt = unroll - base
            ahead = [(jnp.minimum((g + 1) * unroll + t, nchunks - 1), nxt + t) for t in range(unroll)]
            interleave(recur(o_ref, [(g * unroll + t, base + t) for t in range(unroll)]), precompute(src, ahead))
            return carry

        if groups > 1:
            lax.fori_loop(0, groups, body, 0)
        else:
            interleave(recur(o_ref, [(t, t) for t in range(unroll)]))

    phase(qc_ref, kc_ref, vc_ref, gbc_ref, gtc_ref, oc_ref)
    phase(ql_ref, kl_ref, vl_ref, gbl_ref, gtl_ref, ol_ref)

    def finish(o_ref, z_ref, y_ref):
        rows = o_ref.shape[0]
        blk = math.gcd(rows, 512)
        gain = n_ref[...]

        def body(i, carry):
            r0 = pl.multiple_of(i * blk, blk)
            for e in range(2):
                sl = slice(e * HEAD, (e + 1) * HEAD)
                o = o_ref[pl.ds(r0, blk), sl]
                y = o * lax.rsqrt(jnp.mean(o * o, axis=-1, keepdims=True) + RMS_EPS) * gain
                y_ref[pl.ds(r0, blk), sl] = (y * _silu(z_ref[pl.ds(r0, blk), sl].astype(F32))).astype(BF16)
            return carry

        lax.fori_loop(0, rows // blk, body, 0)

    finish(oc_ref, zc_ref, yc_ref)
    finish(ol_ref, zl_ref, yl_ref)


def _gdn_core(qkv_c, gb_c, gt_c, qkv_l, gb_l, gt_l, norm_row, batch):
    lc = qkv_c.shape[0] // batch
    ls = qkv_l.shape[0] // batch
    nqk = GDN_QK_HEADS
    pair = 2 * HEAD
    slots = 2 * GDN_UNROLL

    def specs(rows):
        return [
            pl.BlockSpec((rows, HEAD), lambda b, j: (b, j)),
            pl.BlockSpec((rows, HEAD), lambda b, j: (b, nqk + j)),
            pl.BlockSpec((rows, pair), lambda b, j: (b, nqk + j)),
            pl.BlockSpec((rows, LANES), lambda b, j: (b, 0)),
            pl.BlockSpec((rows // GDN_CHUNK, 2 * GDN_V_HEADS, LANES), lambda b, j: (b, 0, 0)),
            pl.BlockSpec((rows, pair), lambda b, j: (b, GDN_CONV_CH // pair + j)),
        ]

    return pl.pallas_call(
        _gdn_core_kernel,
        grid=(batch, nqk),
        in_specs=specs(lc) + specs(ls) + [pl.BlockSpec((1, LANES), lambda b, j: (0, 0))],
        out_specs=[pl.BlockSpec((lc, pair), lambda b, j: (b, j)),
                   pl.BlockSpec((ls, pair), lambda b, j: (b, j))],
        out_shape=[jax.ShapeDtypeStruct((batch * lc, GDN_V_W), BF16),
                   jax.ShapeDtypeStruct((batch * ls, GDN_V_W), BF16)],
        scratch_shapes=[
            pltpu.VMEM((lc, pair), F32),
            pltpu.VMEM((ls, pair), F32),
            pltpu.VMEM((2, HEAD, pair), F32),
            pltpu.VMEM((slots, 2, 2 * GDN_CHUNK, pair), BF16),
            pltpu.VMEM((slots, 2, GDN_CHUNK, pair), F32),
            pltpu.VMEM((slots, 2, GDN_CHUNK, HEAD), BF16),
            pltpu.VMEM((slots, 2, HEAD, GDN_CHUNK), BF16),
            pltpu.VMEM((slots, 2, GDN_CHUNK, pair), F32),
            pltpu.VMEM((slots, 2, 1, pair), F32),
        ],
        compiler_params=_params(("arbitrary", "arbitrary")),
        name="gdn_core",
    )(qkv_c, qkv_c, qkv_c, gb_c, gt_c, qkv_c, qkv_l, qkv_l, qkv_l, gb_l, gt_l, qkv_l, norm_row)


def _rope_tables(n_tokens, head_dim):
    rows = n_tokens // GRID_W
    r = jnp.repeat(jnp.arange(rows, dtype=F32), GRID_W)
    col = jnp.tile(jnp.arange(GRID_W, dtype=F32), rows)
    d_axis = head_dim // 2
    inv = ROPE_THETA ** (-jnp.arange(0, d_axis, 2, dtype=F32) / d_axis)
    ang = jnp.concatenate([r[:, None] * inv, col[:, None] * inv], axis=-1)
    cos = jnp.repeat(jnp.cos(ang), 2, axis=-1)
    sin = jnp.repeat(jnp.sin(ang), 2, axis=-1) * jnp.tile(jnp.array([-1.0, 1.0], F32), head_dim // 2)
    reps = LANES // head_dim
    return jnp.tile(cos, (1, reps)), jnp.tile(sin, (1, reps))


def _lane_row(v, reps=1):
    return jnp.tile(v.astype(F32), reps).reshape(1, -1)


def kernel(x, c, ctx, c_ctx, norm_g, mod_w, mod_b, gdn_w_in, gdn_conv_w, gdn_a_log, gdn_dt_bias, gdn_out_norm, gdn_w_out, gqa_w_in, gqa_q_norm, gqa_k_norm, gqa_w_out, pool_w_in, pool_w_grp, pool_scale, pool_w_out, diff_w_in, diff_q_norm, diff_k_norm, diff_lambda_q1, diff_lambda_k1, diff_lambda_q2, diff_lambda_k2, diff_sub_norm, diff_w_out):
    batch, seq, d = x.shape
    lctx = ctx.shape[1]
    depth = norm_g.shape[0]
    lat = x.reshape(batch * seq, d)
    cx = ctx.reshape(batch * lctx, d)

    mod_rows = -(-(batch + 1) // 8) * 8
    c_all = jnp.concatenate([c, c_ctx[None, :], jnp.zeros((mod_rows - batch - 1, d), F32)], axis=0)
    mod3 = _mod_call(c_all, mod_w, mod_b).reshape(depth * mod_rows * 3, 1, d)

    rope_gqa = _rope_tables(seq, HEAD)
    rope_diff = _rope_tables(seq, DIFF_DH)

    for i in range(depth):
        m, jl = i % N_MIXERS, i // N_MIXERS
        need_ctx = i < depth - 1
        base_l, base_c = i * mod_rows, i * mod_rows + batch
        g_row = norm_g[i].reshape(1, d)
        lat_args = (lat, mod3, base_l, seq, g_row)
        ctx_args = (cx, mod3, base_c, None, g_row)
        if m == 0:
            w_in = gdn_w_in[jl]
            w_main = w_in[:, :GDN_CONV_CH + GDN_V_W].astype(BF16)
            w_ab = w_in[:, GDN_CONV_CH + GDN_V_W:].astype(BF16)
            conv_w = gdn_conv_w[jl]
            zeros = jnp.zeros((GDN_V_HEADS,), F32)
            alog_row = jnp.concatenate([gdn_a_log[jl, 0], zeros, gdn_a_log[jl, 1], zeros]).reshape(1, LANES)
            dtb_row = jnp.concatenate([gdn_dt_bias[jl, 0], zeros, gdn_dt_bias[jl, 1], zeros]).reshape(1, LANES)
            p_l, ab_l = _proj_gdn(*lat_args, w_main, w_ab, conv_w, seq)
            p_c, ab_c = _proj_gdn(*ctx_args, w_main, w_ab, conv_w, lctx)
            gb_l, gt_l = _gdn_gates(ab_l, alog_row, dtb_row, seq)
            gb_c, gt_c = _gdn_gates(ab_c, alog_row, dtb_row, lctx)
            y_c, y_l = _gdn_core(p_c, gb_c, gt_c, p_l, gb_l, gt_l, _lane_row(gdn_out_norm[jl]), batch)
            w_out = gdn_w_out[jl].astype(BF16)
            lat = _out_proj(y_l, w_out, lat, mod3, base_l, seq)
            if need_ctx:
                cx = _out_proj(y_c, w_out, cx, mod3, base_c, None)
        elif m == 1:
            w_in = gqa_w_in[jl].astype(BF16)
            heads = (_lane_row(gqa_q_norm[jl]), _lane_row(gqa_k_norm[jl]), GQA_QW, GQA_KVW, HEAD,
                     HEAD ** -0.5 * LOG2E)
            p_l = _proj(*lat_args, w_in, kind="heads", heads=heads, rope_tabs=rope_gqa)
            p_c = _proj(*ctx_args, w_in, kind="heads", heads=heads)
            w_out = gqa_w_out[jl].astype(BF16)
            y_l = _gqa_attention(p_l, p_c, p_l, batch)
            lat = _out_proj(y_l, w_out, lat, mod3, base_l, seq)
            if need_ctx:
                y_c = _gqa_attention(p_c, p_c, None, batch)
                cx = _out_proj(y_c, w_out, cx, mod3, base_c, None)
        elif m == 2:
            w_in = pool_w_in[jl].astype(BF16)
            w_grp = pool_w_grp[jl].astype(BF16)
            sc_row = pool_scale[jl].reshape(1, d)
            w_out = pool_w_out[jl].astype(BF16)
            y_l = _pool_mix(_proj(*lat_args, w_in), w_grp, sc_row, seq)
            lat = _out_proj(y_l, w_out, lat, mod3, base_l, seq)
            if need_ctx:
                y_c = _pool_mix(_proj(*ctx_args, w_in), w_grp, sc_row, lctx)
                cx = _out_proj(y_c, w_out, cx, mod3, base_c, None)
        else:
            lam_init = 0.8 - 0.6 * math.exp(-0.3 * i)
            w_in = diff_w_in[jl].astype(BF16)
            heads = (_lane_row(diff_q_norm[jl], 2), _lane_row(diff_k_norm[jl], 2), DIFF_W, DIFF_W, DIFF_DH,
                     DIFF_DH ** -0.5 * LOG2E)
            p_l = _proj(*lat_args, w_in, kind="heads", heads=heads, rope_tabs=rope_diff)
            p_c = _proj(*ctx_args, w_in, kind="heads", heads=heads, j0=DIFF_W // 512, nj=2 * DIFF_W // 512)
            lams = [v[jl].reshape(1, DIFF_DH) for v in (diff_lambda_q1, diff_lambda_k1, diff_lambda_q2, diff_lambda_k2)]
            y_l = _diff_attention(p_l, p_c, lams, _lane_row(diff_sub_norm[jl]), batch, lam_init)
            lat = _out_proj(y_l, diff_w_out[jl].astype(BF16), lat, mod3, base_l, seq)
    return lat.reshape(batch, seq, d)
```

```python
import functools
import math

import jax
import jax.numpy as jnp
from jax import lax
from jax.experimental import pallas as pl
from jax.experimental.pallas import tpu as pltpu

F32 = jnp.float32
BF16 = jnp.bfloat16

D_MODEL = 2048
N_MIXERS = 4
RMS_EPS = 1e-6
L2_EPS = 1e-6
ROPE_THETA = 10000.0
GRID_W = 64

HEAD = 128
KV_TILE = 2048
EPILOGUE_ROWS = 256
LOG2E = math.log2(math.e)
GDN_QK_HEADS = D_MODEL // HEAD
GDN_V_HEADS = 2 * GDN_QK_HEADS
GDN_CONV = 5
GDN_CHUNK = 64
GDN_UNROLL = 4
GDN_INV_EXACT_LEVELS = 6
GDN_QK_W = GDN_QK_HEADS * HEAD
GDN_V_W = GDN_V_HEADS * HEAD
GDN_CONV_CH = 2 * GDN_QK_W + GDN_V_W

GQA_HEADS = D_MODEL // HEAD
GQA_KV_HEADS = GQA_HEADS // 4
GQA_QW = GQA_HEADS * HEAD
GQA_KVW = GQA_KV_HEADS * HEAD

POOL_WINDOWS = (2, 4, 8, 16)
POOL_GW = D_MODEL // len(POOL_WINDOWS)
POOL_HALO = 16

DIFF_DH = 64
DIFF_HEADS = D_MODEL // (2 * DIFF_DH)
DIFF_W = DIFF_HEADS * 2 * DIFF_DH

VMEM_LIMIT_BYTES = 56 * 1024 * 1024
LANES = 128
BF16_SUBLANES = 16
CONV_HALO = 16


def _params(sem):
    return pltpu.CompilerParams(dimension_semantics=sem, vmem_limit_bytes=VMEM_LIMIT_BYTES)


def _tile(n, pref):
    t = min(n, pref)
    while n % t or t % BF16_SUBLANES:
        t -= 1
    return t


def _silu(x):
    return x * jax.nn.sigmoid(x)


def _softplus(x):
    return jnp.maximum(x, 0.0) + jnp.log(1.0 + jnp.exp(-jnp.abs(x)))


def _dot(a, b):
    return jnp.dot(a, b, preferred_element_type=F32)


def _dot_nt(a, b):
    return lax.dot_general(a, b, (((1,), (1,)), ((), ())), preferred_element_type=F32)


def _swap_pairs(x):
    lane = lax.broadcasted_iota(jnp.int32, x.shape, 1)
    return jnp.where(lane % 2 == 0, pltpu.roll(x, LANES - 1, 1), pltpu.roll(x, 1, 1))


def _group_mean_sq(x, group):
    x2 = x * x
    if group == LANES:
        return jnp.mean(x2, axis=-1, keepdims=True)
    lo = lax.broadcasted_iota(jnp.int32, x.shape, 1) < group
    s_lo = jnp.sum(jnp.where(lo, x2, 0.0), axis=-1, keepdims=True)
    s_hi = jnp.sum(jnp.where(lo, 0.0, x2), axis=-1, keepdims=True)
    return jnp.where(lo, s_lo, s_hi) * (1.0 / group)


def _mod_kernel(c_ref, w_ref, b_ref, o_ref):
    a = _silu(c_ref[...]).astype(BF16)
    o_ref[0] = _dot(a, w_ref[0].astype(BF16)) + b_ref[0]


def _mod_call(c_all, mod_w, mod_b):
    depth, d, n = mod_w.shape
    rows = c_all.shape[0]
    tn = _tile(n, 768)
    return pl.pallas_call(
        _mod_kernel,
        grid=(depth, n // tn),
        in_specs=[
            pl.BlockSpec((rows, d), lambda l, j: (0, 0)),
            pl.BlockSpec((1, d, tn), lambda l, j: (l, 0, j)),
            pl.BlockSpec((1, 1, tn), lambda l, j: (l, 0, j)),
        ],
        out_specs=pl.BlockSpec((1, rows, tn), lambda l, j: (l, 0, j)),
        out_shape=jax.ShapeDtypeStruct((depth, rows, n), F32),
        compiler_params=_params(("arbitrary", "arbitrary")),
        name="mod_vectors",
    )(c_all, mod_w, mod_b.reshape(depth, 1, n))


def _head_epilogue(acc, o_ref, gain, cs, scale, group):
    for c in range(acc.shape[1] // LANES):
        xc = acc[:, c * LANES:(c + 1) * LANES]
        y = xc * lax.rsqrt(_group_mean_sq(xc, group) + RMS_EPS) * gain
        if cs is not None:
            y = y * cs[0] + _swap_pairs(y) * cs[1]
        if scale != 1.0:
            y = y * scale
        o_ref[:, c * LANES:(c + 1) * LANES] = y.astype(o_ref.dtype)


def _proj_kernel(*refs, kind, rope, j0, q_tiles, k_tiles, group, q_scale):
    x_ref, g_ref, sh_ref, sc_ref, w_ref = refs[:5]
    rest = list(refs[5:])
    h_ref = rest.pop()
    qn_ref = kn_ref = c_ref = s_ref = None
    if kind == "heads":
        qn_ref, kn_ref = rest.pop(0), rest.pop(0)
        if rope:
            c_ref, s_ref = rest.pop(0), rest.pop(0)
    (o_ref,) = rest
    j = pl.program_id(1)
    jg = j + j0
    tm = h_ref.shape[0]
    rs = math.gcd(tm, EPILOGUE_ROWS)

    def norm_rows(r):
        x = x_ref[r:r + rs, :]
        ms = jnp.mean(x * x, axis=-1, keepdims=True)
        y = x * lax.rsqrt(ms + RMS_EPS) * g_ref[...]
        h_ref[r:r + rs, :] = (y * (1.0 + sc_ref[0]) + sh_ref[0]).astype(BF16)

    def tile(mode, first):
        if mode == "plain" and not first:
            o_ref[...] = _dot(h_ref[...], w_ref[...]).astype(o_ref.dtype)
            return
        gain_ref, scale = (qn_ref, q_scale) if mode == "q" else (kn_ref, 1.0)
        pending = None
        for r in range(0, tm + rs, rs):
            acc = None
            if r < tm:
                if first:
                    norm_rows(r)
                acc = _dot(h_ref[r:r + rs, :], w_ref[...])
            if pending is not None:
                p0, p_acc = pending
                if mode == "plain":
                    o_ref[p0:p0 + rs, :] = p_acc.astype(o_ref.dtype)
                else:
                    cs = (c_ref[p0:p0 + rs, :], s_ref[p0:p0 + rs, :]) if rope else None
                    _head_epilogue(p_acc, o_ref.at[p0:p0 + rs, :], gain_ref[...], cs, scale, group)
            pending = (r, acc)

    def mode_of(col_tile):
        return "q" if col_tile < q_tiles else "k" if col_tile < q_tiles + k_tiles else "plain"

    @pl.when(j == 0)
    def _():
        tile(mode_of(j0), True)

    later = j > 0
    if q_tiles > j0 + 1:
        @pl.when(jnp.logical_and(later, jg < q_tiles))
        def _():
            tile("q", False)

    if k_tiles:
        @pl.when(jnp.logical_and(later, jnp.logical_and(jg >= q_tiles, jg < q_tiles + k_tiles)))
        def _():
            tile("k", False)

    @pl.when(jnp.logical_and(later, jg >= q_tiles + k_tiles))
    def _():
        tile("plain", False)


def _proj(xs, mod3, mod_base, batch_rows, g_row, w, *, kind="plain", heads=None, rope_tabs=None,
          j0=0, nj=None, tn=512, tm_pref=1024):
    m, d = xs.shape
    seq = batch_rows if batch_rows is not None else m
    tm = _tile(seq if rope_tabs is not None or batch_rows is not None else m, tm_pref)
    n_total = w.shape[1] // tn
    nj = n_total - j0 if nj is None else nj
    tiles_per_batch = (batch_rows // tm) if batch_rows is not None else None

    def mod_idx(which):
        if tiles_per_batch is None:
            return lambda i, j: (mod_base * 3 + which, 0, 0)
        return lambda i, j: ((mod_base + i // tiles_per_batch) * 3 + which, 0, 0)

    in_specs = [
        pl.BlockSpec((tm, d), lambda i, j: (i, 0)),
        pl.BlockSpec((1, d), lambda i, j: (0, 0)),
        pl.BlockSpec((1, 1, d), mod_idx(0)),
        pl.BlockSpec((1, 1, d), mod_idx(1)),
        pl.BlockSpec((d, tn), lambda i, j: (0, j + j0)),
    ]
    args = [xs, g_row, mod3, mod3, w]
    kw = dict(kind=kind, rope=rope_tabs is not None, j0=j0, q_tiles=0, k_tiles=0, group=LANES, q_scale=1.0)
    if kind == "heads":
        qn, kn, q_cols, k_cols, group, q_scale = heads
        kw.update(q_tiles=q_cols // tn, k_tiles=k_cols // tn, group=group, q_scale=q_scale)
        in_specs += [pl.BlockSpec((1, LANES), lambda i, j: (0, 0))] * 2
        args += [qn, kn]
        if rope_tabs is not None:
            tps = seq // tm
            in_specs += [pl.BlockSpec((tm, LANES), lambda i, j: (i % tps, 0))] * 2
            args += list(rope_tabs)
    return pl.pallas_call(
        functools.partial(_proj_kernel, **kw),
        grid=(m // tm, nj),
        in_specs=in_specs,
        out_specs=pl.BlockSpec((tm, tn), lambda i, j: (i, j)),
        out_shape=jax.ShapeDtypeStruct((m, nj * tn), BF16),
        scratch_shapes=[pltpu.VMEM((tm, d), BF16)],
        compiler_params=_params(("arbitrary", "arbitrary")),
        name="proj_" + kind,
    )(*args)


def _proj_gdn_kernel(x_ref, xp_ref, xn_ref, g_ref, sh_ref, sc_ref, w_ref, wab_ref, cw_ref, o_ref, ab_ref,
                     h_ref, acc_ref, *, seq):
    tm, tn = o_ref.shape
    j = pl.program_id(1)
    tps = seq // tm
    pos = pl.program_id(0) % tps
    conv_tiles = GDN_CONV_CH // tn
    qk_tiles = GDN_QK_W // tn

    def norm_mod(x):
        ms = jnp.mean(x * x, axis=-1, keepdims=True)
        y = x * lax.rsqrt(ms + RMS_EPS) * g_ref[...]
        return (y * (1.0 + sc_ref[0]) + sh_ref[0]).astype(BF16)

    @pl.when(j >= conv_tiles)
    def _():
        o_ref[...] = _dot(h_ref[CONV_HALO:CONV_HALO + tm, :], w_ref[...]).astype(BF16)

    def conv_tile(first):
        has_prev = (pos != 0).astype(F32)
        has_next = (pos != tps - 1).astype(F32)
        rs = math.gcd(tm, EPILOGUE_ROWS)
        nb = tm // rs
        rows = tm + 2 * CONV_HALO
        base = CONV_HALO - GDN_CONV // 2
        is_qk = j < 2 * qk_tiles
        out_scale = jnp.where(j < qk_tiles, HEAD ** -0.5, 1.0)

        def matmul_rows(b):
            r0, r1 = b * rs, ((b + 1) * rs if b < nb - 1 else rows)
            if first:
                if b == 0:
                    h_ref[0:CONV_HALO, :] = norm_mod(xp_ref[...])
                x0, x1 = max(r0 - CONV_HALO, 0), min(r1 - CONV_HALO, tm)
                h_ref[x0 + CONV_HALO:x1 + CONV_HALO, :] = norm_mod(x_ref[x0:x1, :])
                if b == nb - 1:
                    h_ref[CONV_HALO + tm:, :] = norm_mod(xn_ref[...])
            acc_ref[r0:r1, :] = _dot(h_ref[r0:r1, :], w_ref[...])
            if b == 0:
                acc_ref[0:CONV_HALO, :] = acc_ref[0:CONV_HALO, :] * has_prev
            if b == nb - 1:
                acc_ref[CONV_HALO + tm:, :] = acc_ref[CONV_HALO + tm:, :] * has_next

        def conv_rows(b):
            r0 = base + b * rs
            y = cw_ref[0:1, :] * acc_ref[r0:r0 + rs, :]
            for k in range(1, GDN_CONV):
                y = y + cw_ref[k:k + 1, :] * acc_ref[r0 + k:r0 + k + rs, :]
            y = _silu(y)
            for c in range(tn // HEAD):
                sl = slice(c * HEAD, (c + 1) * HEAD)
                yc = y[:, sl]
                inv = lax.rsqrt(jnp.sum(yc * yc, axis=-1, keepdims=True) + L2_EPS) * out_scale
                o_ref[b * rs:(b + 1) * rs, sl] = (yc * jnp.where(is_qk, inv, 1.0)).astype(BF16)

        matmul_rows(0)
        for b in range(1, nb):
            matmul_rows(b)
            conv_rows(b - 1)
        conv_rows(nb - 1)
        if first:
            ab_ref[...] = _dot(h_ref[CONV_HALO:CONV_HALO + tm, :], wab_ref[...])

    @pl.when(j == 0)
    def _():
        conv_tile(True)

    @pl.when(jnp.logical_and(j > 0, j < conv_tiles))
    def _():
        conv_tile(False)


def _proj_gdn(xs, mod3, mod_base, batch_rows, g_row, w, wab, conv_w, seq, *, tn=512, tm_pref=1024):
    m, d = xs.shape
    tm = _tile(seq, tm_pref)
    hb = tm // CONV_HALO
    last = m // CONV_HALO - 1
    tiles_per_batch = (batch_rows // tm) if batch_rows is not None else None
    conv_last = GDN_CONV_CH // tn - 1

    def mod_idx(which):
        if tiles_per_batch is None:
            return lambda i, j: (mod_base * 3 + which, 0, 0)
        return lambda i, j: ((mod_base + i // tiles_per_batch) * 3 + which, 0, 0)

    return pl.pallas_call(
        functools.partial(_proj_gdn_kernel, seq=seq),
        grid=(m // tm, w.shape[1] // tn),
        in_specs=[
            pl.BlockSpec((tm, d), lambda i, j: (i, 0)),
            pl.BlockSpec((CONV_HALO, d), lambda i, j: (jnp.maximum(i * hb - 1, 0), 0)),
            pl.BlockSpec((CONV_HALO, d), lambda i, j: (jnp.minimum((i + 1) * hb, last), 0)),
            pl.BlockSpec((1, d), lambda i, j: (0, 0)),
            pl.BlockSpec((1, 1, d), mod_idx(0)),
            pl.BlockSpec((1, 1, d), mod_idx(1)),
            pl.BlockSpec((d, tn), lambda i, j: (0, j)),
            pl.BlockSpec((d, LANES), lambda i, j: (0, 0)),
            pl.BlockSpec((GDN_CONV, tn), lambda i, j: (0, jnp.minimum(j, conv_last))),
        ],
        out_specs=[pl.BlockSpec((tm, tn), lambda i, j: (i, j)),
                   pl.BlockSpec((tm, LANES), lambda i, j: (i, 0))],
        out_shape=[jax.ShapeDtypeStruct((m, w.shape[1]), BF16), jax.ShapeDtypeStruct((m, LANES), F32)],
        scratch_shapes=[pltpu.VMEM((tm + 2 * CONV_HALO, d), BF16), pltpu.VMEM((tm + 2 * CONV_HALO, tn), F32)],
        compiler_params=_params(("arbitrary", "arbitrary")),
        name="proj_gdn_conv",
    )(xs, xs, xs, g_row, mod3, mod3, w, wab, conv_w)


def _out_kernel(y_ref, w_ref, x_ref, gate_ref, out_ref):
    step = math.gcd(out_ref.shape[1], 512)
    for c in range(0, out_ref.shape[1], step):
        sl = slice(c, c + step)
        out_ref[:, sl] = x_ref[:, sl] + gate_ref[0][:, sl] * _dot(y_ref[...], w_ref[:, sl])


def _out_proj(y, w, xs, mod3, mod_base, batch_rows):
    m, d = xs.shape
    kdim = w.shape[0]
    resident = 2 * w.size * w.dtype.itemsize <= VMEM_LIMIT_BYTES // 3
    tn = d if resident else 512
    tm = _tile(batch_rows if batch_rows is not None else m, 512 if resident else 1024)
    tiles_per_batch = (batch_rows // tm) if batch_rows is not None else None
    if tiles_per_batch is None:
        gate_idx = lambda i, j: (mod_base * 3 + 2, 0, j)
    else:
        gate_idx = lambda i, j: ((mod_base + i // tiles_per_batch) * 3 + 2, 0, j)
    return pl.pallas_call(
        _out_kernel,
        grid=(m // tm, d // tn),
        in_specs=[
            pl.BlockSpec((tm, kdim), lambda i, j: (i, 0)),
            pl.BlockSpec((kdim, tn), lambda i, j: (0, j)),
            pl.BlockSpec((tm, tn), lambda i, j: (i, j)),
            pl.BlockSpec((1, 1, tn), gate_idx),
        ],
        out_specs=pl.BlockSpec((tm, tn), lambda i, j: (i, j)),
        out_shape=jax.ShapeDtypeStruct((m, d), F32),
        compiler_params=_params(("arbitrary", "arbitrary")),
        name="out_proj",
    )(y, w, xs, mod3)


def _flash(queries, kv_refs):
    tq = queries[0].shape[0]
    m = [jnp.full((tq, 1), -1e30, F32) for _ in queries]
    l = [jnp.zeros((tq, 1), F32) for _ in queries]
    acc = [jnp.zeros((tq, HEAD), F32) for _ in queries]
    for k_ref, v_ref in kv_refs:
        rows = k_ref.shape[0]
        tk = math.gcd(rows, KV_TILE)
        for t in range(rows // tk):
            k = k_ref[t * tk:(t + 1) * tk, :]
            v = v_ref[t * tk:(t + 1) * tk, :]
            for i, q in enumerate(queries):
                s = _dot_nt(q, k)
                m_new = jnp.maximum(m[i], jnp.max(s, axis=-1, keepdims=True))
                alpha = jnp.exp2(m[i] - m_new)
                p = jnp.exp2(s - m_new)
                l[i] = alpha * l[i] + jnp.sum(p, axis=-1, keepdims=True)
                acc[i] = alpha * acc[i] + _dot(p.astype(BF16), v)
                m[i] = m_new
    return [a / li for a, li in zip(acc, l)]


def _gqa_attn_kernel(*refs, has_lat):
    if has_lat:
        q_ref, z_ref, kc_ref, vc_ref, kl_ref, vl_ref, y_ref = refs
        kv = [(kc_ref, vc_ref), (kl_ref, vl_ref)]
    else:
        q_ref, z_ref, kc_ref, vc_ref, y_ref = refs
        kv = [(kc_ref, vc_ref)]
    tq = q_ref.shape[0]
    heads = [slice(g * HEAD, (g + 1) * HEAD) for g in range(q_ref.shape[1] // HEAD)]
    (o,) = _flash([jnp.concatenate([q_ref[:, sl] for sl in heads], axis=0)], kv)
    for g, sl in enumerate(heads):
        y_ref[:, sl] = (o[g * tq:(g + 1) * tq] * _silu(z_ref[:, sl].astype(F32))).astype(BF16)


def _gqa_attention(p_q, p_c, p_l, batch, *, tq_pref=256):
    has_lat = p_l is not None
    m = p_q.shape[0]
    rows_q = m // batch
    tq = _tile(rows_q, tq_pref)
    nq = rows_q // tq
    lc = p_c.shape[0] // batch
    grp = GQA_QW // GQA_KV_HEADS
    k_blk, v_blk, z_blk = GQA_QW // HEAD, (GQA_QW + GQA_KVW) // HEAD, (GQA_QW + 2 * GQA_KVW) // grp
    in_specs = [
        pl.BlockSpec((tq, grp), lambda b, h, i: (b * nq + i, h)),
        pl.BlockSpec((tq, grp), lambda b, h, i: (b * nq + i, z_blk + h)),
        pl.BlockSpec((lc, HEAD), lambda b, h, i: (b, k_blk + h)),
        pl.BlockSpec((lc, HEAD), lambda b, h, i: (b, v_blk + h)),
    ]
    args = [p_q, p_q, p_c, p_c]
    if has_lat:
        ls = p_l.shape[0] // batch
        in_specs += [pl.BlockSpec((ls, HEAD), lambda b, h, i: (b, k_blk + h)),
                     pl.BlockSpec((ls, HEAD), lambda b, h, i: (b, v_blk + h))]
        args += [p_l, p_l]
    return pl.pallas_call(
        functools.partial(_gqa_attn_kernel, has_lat=has_lat),
        grid=(batch, GQA_KV_HEADS, nq),
        in_specs=in_specs,
        out_specs=pl.BlockSpec((tq, grp), lambda b, h, i: (b * nq + i, h)),
        out_shape=jax.ShapeDtypeStruct((m, GQA_QW), BF16),
        compiler_params=_params(("arbitrary", "arbitrary", "arbitrary")),
        name="gqa_attention",
    )(*args)


def _diff_attn_kernel(lq1_ref, lk1_ref, lq2_ref, lk2_ref, n_ref, q_ref, z_ref, kc_ref, vc_ref, kl_ref, vl_ref,
                      y_ref, *, lam_init):
    lam = (jnp.exp(jnp.sum(lq1_ref[...] * lk1_ref[...], axis=-1, keepdims=True))
           - jnp.exp(jnp.sum(lq2_ref[...] * lk2_ref[...], axis=-1, keepdims=True)) + lam_init)
    q = q_ref[...]
    lo = lax.broadcasted_iota(jnp.int32, q.shape, 1) < DIFF_DH
    zero = jnp.zeros_like(q)
    tq = q.shape[0]
    (o01,) = _flash([jnp.concatenate([jnp.where(lo, q, zero), jnp.where(lo, zero, q)], axis=0)],
                    [(kc_ref, vc_ref), (kl_ref, vl_ref)])
    o = o01[0:tq] - lam * o01[tq:2 * tq]
    y = o * lax.rsqrt(jnp.mean(o * o, axis=-1, keepdims=True) + RMS_EPS) * n_ref[...] * (1.0 - lam_init)
    y_ref[...] = (y * _silu(z_ref[...].astype(F32))).astype(BF16)


def _diff_attention(p_l, p_c, lams, sub_norm, batch, lam_init, *, tq_pref=512):
    m = p_l.shape[0]
    ls = m // batch
    lc = p_c.shape[0] // batch
    tq = _tile(ls, tq_pref)
    nq = ls // tq
    nh = DIFF_HEADS
    small = pl.BlockSpec((1, DIFF_DH), lambda b, h, i: (0, 0))
    in_specs = [small] * 4 + [
        pl.BlockSpec((1, HEAD), lambda b, h, i: (0, 0)),
        pl.BlockSpec((tq, HEAD), lambda b, h, i: (b * nq + i, h)),
        pl.BlockSpec((tq, HEAD), lambda b, h, i: (b * nq + i, 3 * nh + h)),
        pl.BlockSpec((lc, HEAD), lambda b, h, i: (b, h)),
        pl.BlockSpec((lc, HEAD), lambda b, h, i: (b, nh + h)),
        pl.BlockSpec((ls, HEAD), lambda b, h, i: (b, nh + h)),
        pl.BlockSpec((ls, HEAD), lambda b, h, i: (b, 2 * nh + h)),
    ]
    return pl.pallas_call(
        functools.partial(_diff_attn_kernel, lam_init=lam_init),
        grid=(batch, nh, nq),
        in_specs=in_specs,
        out_specs=pl.BlockSpec((tq, HEAD), lambda b, h, i: (b * nq + i, h)),
        out_shape=jax.ShapeDtypeStruct((m, DIFF_W), BF16),
        compiler_params=_params(("arbitrary", "arbitrary", "arbitrary")),
        name="diff_attention",
    )(*lams, sub_norm, p_l, p_l, p_c, p_c, p_l, p_l)


def _pool_kernel(u_ref, up_ref, un_ref, z_ref, wg_ref, sc_ref, y_ref, ext_ref, *, seq):
    tm = u_ref.shape[0]
    tps = seq // tm
    pos = pl.program_id(0) % tps
    has_prev = (pos != 0).astype(F32)
    has_next = (pos != tps - 1).astype(F32)
    ext_ref[0:POOL_HALO, :] = up_ref[...].astype(F32) * has_prev
    ext_ref[POOL_HALO:POOL_HALO + tm, :] = u_ref[...].astype(F32)
    ext_ref[POOL_HALO + tm:, :] = un_ref[...].astype(F32) * has_next
    t = pos * tm + lax.broadcasted_iota(jnp.int32, (tm, 1), 0)
    for gi, w in enumerate(POOL_WINDOWS):
        cols = slice(gi * POOL_GW, (gi + 1) * POOL_GW)
        tot = ext_ref[POOL_HALO - w // 2:POOL_HALO - w // 2 + tm, cols]
        for k in range(1, w):
            tot = tot + ext_ref[POOL_HALO - w // 2 + k:POOL_HALO - w // 2 + k + tm, cols]
        cnt = jnp.minimum(t - w // 2 + w, seq) - jnp.maximum(t - w // 2, 0)
        dgrp = tot / cnt.astype(F32) - ext_ref[POOL_HALO:POOL_HALO + tm, cols]
        r = _dot(dgrp.astype(BF16), wg_ref[gi]) * sc_ref[:, cols]
        y_ref[:, cols] = (r * _silu(z_ref[:, cols].astype(F32))).astype(BF16)


def _pool_mix(p, w_grp, ch_scale, seq):
    m = p.shape[0]
    w_u = D_MODEL
    tm = _tile(seq, 256)
    hb = tm // POOL_HALO
    last = m // POOL_HALO - 1
    return pl.pallas_call(
        functools.partial(_pool_kernel, seq=seq),
        grid=(m // tm,),
        in_specs=[
            pl.BlockSpec((tm, w_u), lambda i: (i, 0)),
            pl.BlockSpec((POOL_HALO, w_u), lambda i: (jnp.maximum(i * hb - 1, 0), 0)),
            pl.BlockSpec((POOL_HALO, w_u), lambda i: (jnp.minimum((i + 1) * hb, last), 0)),
            pl.BlockSpec((tm, w_u), lambda i: (i, 1)),
            pl.BlockSpec(w_grp.shape, lambda i: (0, 0, 0)),
            pl.BlockSpec((1, w_u), lambda i: (0, 0)),
        ],
        out_specs=pl.BlockSpec((tm, w_u), lambda i: (i, 0)),
        out_shape=jax.ShapeDtypeStruct((m, w_u), BF16),
        scratch_shapes=[pltpu.VMEM((tm + 2 * POOL_HALO, w_u), F32)],
        compiler_params=_params(("arbitrary",)),
        name="pool_mix",
    )(p, p, p, p, w_grp, ch_scale)


def _split3(x):
    hi = x.astype(BF16)
    r1 = x - hi.astype(F32)
    mid = r1.astype(BF16)
    lo = (r1 - mid.astype(F32)).astype(BF16)
    return hi, mid, lo


def _gdn_gates_kernel(ab_ref, alog_ref, dtb_ref, gb_ref, gt_ref):
    tm = ab_ref.shape[0]
    ab = ab_ref[...]
    lane = lax.broadcasted_iota(jnp.int32, (1, LANES), 1)
    is_g = (lane % (2 * GDN_V_HEADS)) < GDN_V_HEADS
    gval = -jnp.exp(alog_ref[...]) * _softplus(ab + dtb_ref[...])
    g = jnp.where(is_g, gval, 0.0)
    r = lax.broadcasted_iota(jnp.int32, (tm, tm), 0)
    c = lax.broadcasted_iota(jnp.int32, (tm, tm), 1)
    same = (r // GDN_CHUNK) == (c // GDN_CHUNK)
    tri_f = jnp.where(jnp.logical_and(same, r >= c), 1.0, 0.0).astype(BF16)
    tri_b = jnp.where(jnp.logical_and(same, r <= c), 1.0, 0.0).astype(BF16)
    pieces = _split3(g)
    gc_f = functools.reduce(jnp.add, [_dot(tri_f, x) for x in pieces])
    gc_b = functools.reduce(jnp.add, [_dot(tri_b, x) for x in pieces])
    gc = jnp.where(lane < 2 * GDN_V_HEADS, gc_f, gc_b)
    gb_ref[...] = jnp.where(is_g, gc, jax.nn.sigmoid(ab))
    for ci in range(tm // GDN_CHUNK):
        tile = gc[ci * GDN_CHUNK:(ci + 1) * GDN_CHUNK, :]
        tt = jnp.concatenate([tile, tile], axis=0).T
        gt_ref[ci, 0:GDN_V_HEADS, :] = tt[0:GDN_V_HEADS, :]
        gt_ref[ci, GDN_V_HEADS:, :] = tt[2 * GDN_V_HEADS:3 * GDN_V_HEADS, :]


def _gdn_gates(ab, alog_row, dtb_row, seq):
    m = ab.shape[0]
    tm = _tile(seq, 512)
    return pl.pallas_call(
        _gdn_gates_kernel,
        grid=(m // tm,),
        in_specs=[
            pl.BlockSpec((tm, LANES), lambda i: (i, 0)),
            pl.BlockSpec((1, LANES), lambda i: (0, 0)),
            pl.BlockSpec((1, LANES), lambda i: (0, 0)),
        ],
        out_specs=[
            pl.BlockSpec((tm, LANES), lambda i: (i, 0)),
            pl.BlockSpec((tm // GDN_CHUNK, 2 * GDN_V_HEADS, LANES), lambda i: (i, 0, 0)),
        ],
        out_shape=[
            jax.ShapeDtypeStruct((m, LANES), F32),
            jax.ShapeDtypeStruct((m // GDN_CHUNK, 2 * GDN_V_HEADS, LANES), F32),
        ],
        compiler_params=_params(("arbitrary",)),
        name="gdn_gates",
    )(ab, alog_row, dtb_row)


def _gdn_core_kernel(qc_ref, kc_ref, vc_ref, gbc_ref, gtc_ref, zc_ref, ql_ref, kl_ref, vl_ref, gbl_ref, gtl_ref,
                     zl_ref, n_ref, yc_ref, yl_ref, oc_ref, ol_ref, s_ref, wq_ref, u_ref, qkm_ref, kt_ref,
                     ksc_ref, gtot_ref):
    C = GDN_CHUNK
    jh = pl.program_id(1)
    s_ref[...] = jnp.zeros_like(s_ref)
    oc_ref[...] = jnp.zeros_like(oc_ref)
    ol_ref[...] = jnp.zeros_like(ol_ref)

    lane_w = lax.broadcasted_iota(jnp.int32, (C, 2 * LANES), 1)
    row_w = lax.broadcasted_iota(jnp.int32, (C, 2 * LANES), 0)
    col_w = lane_w % C
    fwd_w = lane_w < LANES
    ahead_w = jnp.where(fwd_w, row_w - col_w, col_w - row_w)
    incl_w = ahead_w >= 0
    strict_w = ahead_w > 0
    prob_w = lane_w // C
    prob_masks = [jnp.where(prob_w == p, 1.0, 0.0).astype(BF16) for p in range(4)]
    lane_p = lax.broadcasted_iota(jnp.int32, (C, LANES), 1)
    row_p = lax.broadcasted_iota(jnp.int32, (C, LANES), 0)
    first_p = lane_p < C
    lane_s = lax.broadcasted_iota(jnp.int32, (1, 2 * LANES), 1)
    lo_s = lane_s < LANES

    def block_diag(a):
        return jnp.concatenate([a * mk for mk in prob_masks], axis=0)

    def chunk_rows(ci):
        return ci * C if isinstance(ci, int) else pl.multiple_of(ci * C, C)

    def pair_blocks(a):
        zero = jnp.zeros((a.shape[0], LANES), a.dtype)
        return jnp.concatenate([jnp.concatenate([a[:, :LANES], zero], axis=1),
                                jnp.concatenate([zero, a[:, LANES:]], axis=1)], axis=0)

    def split2(a):
        hi = a.astype(BF16)
        return hi, (a - hi.astype(F32)).astype(BF16)

    def precompute(src, steps):
        q_ref, k_ref, v_ref, gb_ref, gt_ref = src
        nchunks = q_ref.shape[0] // C
        todo = range(len(steps))
        st = []
        for n, _ in steps:
            per_dir = []
            for d in range(2):
                ci = n if d == 0 else nchunks - 1 - n
                r0 = chunk_rows(ci)
                q = q_ref[pl.ds(r0, C), :]
                k = k_ref[pl.ds(r0, C), :]
                v = v_ref[pl.ds(r0, C), :].astype(F32)
                gb = gb_ref[pl.ds(r0, C), :]
                gt = gt_ref[ci]
                g_lane = d * 2 * GDN_V_HEADS + 2 * jh
                cols = []
                for lane0 in (g_lane, g_lane + GDN_V_HEADS):
                    e0 = jnp.sum(jnp.where(lane_p == lane0, gb, 0.0), axis=1, keepdims=True)
                    e1 = jnp.sum(jnp.where(lane_p == lane0 + 1, gb, 0.0), axis=1, keepdims=True)
                    cols.append((e0, e1))
                (gc0, gc1), (be0, be1) = cols
                g_row = d * GDN_V_HEADS + 2 * jh
                pick = row_p == jnp.where(first_p, g_row, g_row + 1)
                gc_row = jnp.sum(jnp.where(pick, gt, 0.0), axis=0, keepdims=True)
                kq = _dot_nt(jnp.concatenate([k, q], axis=0), jnp.concatenate([k, k], axis=0))
                last = C - 1 if d == 0 else 0
                per_dir.append(dict(q=q.astype(F32), k=k.astype(F32), v=v, gc=(gc0, gc1), be=(be0, be1),
                                    gc_col=jnp.where(first_p, gc0, gc1), be_col=jnp.where(first_p, be0, be1),
                                    gc_row=gc_row, kk=kq[0:C], qk=kq[C:2 * C],
                                    tot=(gc0[last:last + 1], gc1[last:last + 1])))
            st.append(per_dir)
        yield

        ms, qkms = [], []
        for per_dir in st:
            wide = lambda key: jnp.concatenate([per_dir[0][key], per_dir[1][key]], axis=1)
            dm = wide("gc_col") - wide("gc_row")
            decay = jnp.where(incl_w, jnp.exp(jnp.where(incl_w, dm, 0.0)), 0.0)
            ms.append(-jnp.where(strict_w, wide("be_col") * wide("kk") * decay, 0.0))
            qkms.append(wide("qk") * decay)

        xs = list(ms)
        for lvl in range(6):
            lhs = [ms[i] if lvl == 0 else xs[i] if lvl == 5 else jnp.concatenate([ms[i], xs[i]], axis=0)
                   for i in todo]
            if lvl < GDN_INV_EXACT_LEVELS:
                lhs = [split2(a) for a in lhs]
                rhs = [tuple(block_diag(piece) for piece in split2(ms[i])) for i in todo]
                prod = [_dot(jnp.concatenate([lhs[i][0], lhs[i][1], lhs[i][0]], axis=1),
                             jnp.concatenate([rhs[i][0], rhs[i][0], rhs[i][1]], axis=0)) for i in todo]
            else:
                prod = [_dot(lhs[i].astype(BF16), block_diag(ms[i].astype(BF16))) for i in todo]
            yield
            for i in todo:
                if lvl == 0:
                    ms[i] = prod[i]
                elif lvl < 5:
                    xs[i] = xs[i] + ms[i] + prod[i][C:2 * C]
                    ms[i] = prod[i][0:C]
                else:
                    xs[i] = xs[i] + ms[i] + prod[i]

        rhs_all = []
        for per_dir in st:
            rhs = []
            for d in range(2):
                pd = per_dir[d]
                for e in range(2):
                    be, egc = pd["be"][e], jnp.exp(pd["gc"][e])
                    rhs.append(jnp.concatenate([be * pd["v"][:, e * HEAD:(e + 1) * HEAD], (be * egc) * pd["k"]],
                                               axis=1))
            rhs_all.append(jnp.concatenate(rhs, axis=0))
        xr = [_dot(block_diag(xs[i].astype(BF16)), rhs_all[i].astype(BF16)) for i in todo]
        yield

        for i in todo:
            uw = rhs_all[i] + xr[i]
            slot = steps[i][1]
            for d in range(2):
                pd = st[i][d]
                u = jnp.concatenate([uw[(2 * d + e) * C:(2 * d + e + 1) * C, 0:HEAD] for e in range(2)], axis=1)
                w = jnp.concatenate([uw[(2 * d + e) * C:(2 * d + e + 1) * C, HEAD:] for e in range(2)], axis=1)
                qdec = jnp.concatenate([pd["q"] * jnp.exp(pd["gc"][e]) for e in range(2)], axis=1)
                tot0, tot1 = pd["tot"]
                kscale = jnp.concatenate([jnp.broadcast_to(jnp.exp(tot0 - pd["gc"][0]), (C, HEAD)),
                                          jnp.broadcast_to(jnp.exp(tot1 - pd["gc"][1]), (C, HEAD))], axis=1)
                wq_ref[slot, d] = jnp.concatenate([w, qdec], axis=0).astype(BF16)
                u_ref[slot, d] = u
                qkm_ref[slot, d] = qkms[i][:, d * LANES:(d + 1) * LANES].astype(BF16)
                kt_ref[slot, d] = pd["k"].T.astype(BF16)
                ksc_ref[slot, d] = kscale
                gtot_ref[slot, d] = jnp.where(lo_s, jnp.exp(tot0), jnp.exp(tot1))
        yield

    def recur(o_ref, steps):
        nchunks = o_ref.shape[0] // C
        for n, slot in steps:
            state = [s_ref[d] for d in range(2)]
            ws_qs = [_dot(wq_ref[slot, d], pair_blocks(state[d].astype(BF16))) for d in range(2)]
            yield
            v_new = [u_ref[slot, d] - ws_qs[d][0:C] for d in range(2)]
            for d in range(2):
                ci = n if d == 0 else nchunks - 1 - n
                o = ws_qs[d][C:2 * C] + _dot(qkm_ref[slot, d], pair_blocks(v_new[d].astype(BF16)))
                o_ref[pl.ds(chunk_rows(ci), C), :] += o
                ds = _dot(kt_ref[slot, d], (v_new[d] * ksc_ref[slot, d]).astype(BF16))
                s_ref[d] = state[d] * gtot_ref[slot, d] + ds
            yield

    def interleave(*gens):
        gens = list(gens)
        while gens:
            for g in list(gens):
                if next(g, gens) is gens:
                    gens.remove(g)

    def phase(q_ref, k_ref, v_ref, gb_ref, gt_ref, o_ref):
        nchunks = q_ref.shape[0] // C
        unroll = math.gcd(nchunks, GDN_UNROLL)
        groups = nchunks // unroll
        src = (q_ref, k_ref, v_ref, gb_ref, gt_ref)
        interleave(precompute(src, [(t, t) for t in range(unroll)]))

        def body(g, carry):
            base = (g % 2) * unroll
            nxt = unroll - base
            ahead = [(jnp.minimum((g + 1) * unroll + t, nchunks - 1), nxt + t) for t in range(unroll)]
            interleave(recur(o_ref, [(g * unroll + t, base + t) for t in range(unroll)]), precompute(src, ahead))
            return carry

        if groups > 1:
            lax.fori_loop(0, groups, body, 0)
        else:
            interleave(recur(o_ref, [(t, t) for t in range(unroll)]))

    phase(qc_ref, kc_ref, vc_ref, gbc_ref, gtc_ref, oc_ref)
    phase(ql_ref, kl_ref, vl_ref, gbl_ref, gtl_ref, ol_ref)

    def finish(o_ref, z_ref, y_ref):
        rows = o_ref.shape[0]
        blk = math.gcd(rows, 512)
        gain = n_ref[...]

        def body(i, carry):
            r0 = pl.multiple_of(i * blk, blk)
            for e in range(2):
                sl = slice(e * HEAD, (e + 1) * HEAD)
                o = o_ref[pl.ds(r0, blk), sl]
                y = o * lax.rsqrt(jnp.mean(o * o, axis=-1, keepdims=True) + RMS_EPS) * gain
                y_ref[pl.ds(r0, blk), sl] = (y * _silu(z_ref[pl.ds(r0, blk), sl].astype(F32))).astype(BF16)
            return carry

        lax.fori_loop(0, rows // blk, body, 0)

    finish(oc_ref, zc_ref, yc_ref)
    finish(ol_ref, zl_ref, yl_ref)


def _gdn_core(qkv_c, gb_c, gt_c, qkv_l, gb_l, gt_l, norm_row, batch):
    lc = qkv_c.shape[0] // batch
    ls = qkv_l.shape[0] // batch
    nqk = GDN_QK_HEADS
    pair = 2 * HEAD
    slots = 2 * GDN_UNROLL

    def specs(rows):
        return [
            pl.BlockSpec((rows, HEAD), lambda b, j: (b, j)),
            pl.BlockSpec((rows, HEAD), lambda b, j: (b, nqk + j)),
            pl.BlockSpec((rows, pair), lambda b, j: (b, nqk + j)),
            pl.BlockSpec((rows, LANES), lambda b, j: (b, 0)),
            pl.BlockSpec((rows // GDN_CHUNK, 2 * GDN_V_HEADS, LANES), lambda b, j: (b, 0, 0)),
            pl.BlockSpec((rows, pair), lambda b, j: (b, GDN_CONV_CH // pair + j)),
        ]

    return pl.pallas_call(
        _gdn_core_kernel,
        grid=(batch, nqk),
        in_specs=specs(lc) + specs(ls) + [pl.BlockSpec((1, LANES), lambda b, j: (0, 0))],
        out_specs=[pl.BlockSpec((lc, pair), lambda b, j: (b, j)),
                   pl.BlockSpec((ls, pair), lambda b, j: (b, j))],
        out_shape=[jax.ShapeDtypeStruct((batch * lc, GDN_V_W), BF16),
                   jax.ShapeDtypeStruct((batch * ls, GDN_V_W), BF16)],
        scratch_shapes=[
            pltpu.VMEM((lc, pair), F32),
            pltpu.VMEM((ls, pair), F32),
            pltpu.VMEM((2, HEAD, pair), F32),
            pltpu.VMEM((slots, 2, 2 * GDN_CHUNK, pair), BF16),
            pltpu.VMEM((slots, 2, GDN_CHUNK, pair), F32),
            pltpu.VMEM((slots, 2, GDN_CHUNK, HEAD), BF16),
            pltpu.VMEM((slots, 2, HEAD, GDN_CHUNK), BF16),
            pltpu.VMEM((slots, 2, GDN_CHUNK, pair), F32),
            pltpu.VMEM((slots, 2, 1, pair), F32),
        ],
        compiler_params=_params(("arbitrary", "arbitrary")),
        name="gdn_core",
    )(qkv_c, qkv_c, qkv_c, gb_c, gt_c, qkv_c, qkv_l, qkv_l, qkv_l, gb_l, gt_l, qkv_l, norm_row)


def _rope_tables(n_tokens, head_dim):
    rows = n_tokens // GRID_W
    r = jnp.repeat(jnp.arange(rows, dtype=F32), GRID_W)
    col = jnp.tile(jnp.arange(GRID_W, dtype=F32), rows)
    d_axis = head_dim // 2
    inv = ROPE_THETA ** (-jnp.arange(0, d_axis, 2, dtype=F32) / d_axis)
    ang = jnp.concatenate([r[:, None] * inv, col[:, None] * inv], axis=-1)
    cos = jnp.repeat(jnp.cos(ang), 2, axis=-1)
    sin = jnp.repeat(jnp.sin(ang), 2, axis=-1) * jnp.tile(jnp.array([-1.0, 1.0], F32), head_dim // 2)
    reps = LANES // head_dim
    return jnp.tile(cos, (1, reps)), jnp.tile(sin, (1, reps))


def _lane_row(v, reps=1):
    return jnp.tile(v.astype(F32), reps).reshape(1, -1)


def kernel(x, c, ctx, c_ctx, norm_g, mod_w, mod_b, gdn_w_in, gdn_conv_w, gdn_a_log, gdn_dt_bias, gdn_out_norm, gdn_w_out, gqa_w_in, gqa_q_norm, gqa_k_norm, gqa_w_out, pool_w_in, pool_w_grp, pool_scale, pool_w_out, diff_w_in, diff_q_norm, diff_k_norm, diff_lambda_q1, diff_lambda_k1, diff_lambda_q2, diff_lambda_k2, diff_sub_norm, diff_w_out):
    batch, seq, d = x.shape
    lctx = ctx.shape[1]
    depth = norm_g.shape[0]
    lat = x.reshape(batch * seq, d)
    cx = ctx.reshape(batch * lctx, d)

    mod_rows = -(-(batch + 1) // 8) * 8
    c_all = jnp.concatenate([c, c_ctx[None, :], jnp.zeros((mod_rows - batch - 1, d), F32)], axis=0)
    mod3 = _mod_call(c_all, mod_w, mod_b).reshape(depth * mod_rows * 3, 1, d)

    rope_gqa = _rope_tables(seq, HEAD)
    rope_diff = _rope_tables(seq, DIFF_DH)

    for i in range(depth):
        m, jl = i % N_MIXERS, i // N_MIXERS
        need_ctx = i < depth - 1
        base_l, base_c = i * mod_rows, i * mod_rows + batch
        g_row = norm_g[i].reshape(1, d)
        lat_args = (lat, mod3, base_l, seq, g_row)
        ctx_args = (cx, mod3, base_c, None, g_row)
        if m == 0:
            w_in = gdn_w_in[jl]
            w_main = w_in[:, :GDN_CONV_CH + GDN_V_W].astype(BF16)
            w_ab = w_in[:, GDN_CONV_CH + GDN_V_W:].astype(BF16)
            conv_w = gdn_conv_w[jl]
            zeros = jnp.zeros((GDN_V_HEADS,), F32)
            alog_row = jnp.concatenate([gdn_a_log[jl, 0], zeros, gdn_a_log[jl, 1], zeros]).reshape(1, LANES)
            dtb_row = jnp.concatenate([gdn_dt_bias[jl, 0], zeros, gdn_dt_bias[jl, 1], zeros]).reshape(1, LANES)
            p_l, ab_l = _proj_gdn(*lat_args, w_main, w_ab, conv_w, seq)
            p_c, ab_c = _proj_gdn(*ctx_args, w_main, w_ab, conv_w, lctx, tn=1024)
            gb_l, gt_l = _gdn_gates(ab_l, alog_row, dtb_row, seq)
            gb_c, gt_c = _gdn_gates(ab_c, alog_row, dtb_row, lctx)
            y_c, y_l = _gdn_core(p_c, gb_c, gt_c, p_l, gb_l, gt_l, _lane_row(gdn_out_norm[jl]), batch)
            w_out = gdn_w_out[jl].astype(BF16)
            lat = _out_proj(y_l, w_out, lat, mod3, base_l, seq)
            if need_ctx:
                cx = _out_proj(y_c, w_out, cx, mod3, base_c, None)
        elif m == 1:
            w_in = gqa_w_in[jl].astype(BF16)
            heads = (_lane_row(gqa_q_norm[jl]), _lane_row(gqa_k_norm[jl]), GQA_QW, GQA_KVW, HEAD,
                     HEAD ** -0.5 * LOG2E)
            p_l = _proj(*lat_args, w_in, kind="heads", heads=heads, rope_tabs=rope_gqa)
            p_c = _proj(*ctx_args, w_in, kind="heads", heads=heads)
            w_out = gqa_w_out[jl].astype(BF16)
            y_l = _gqa_attention(p_l, p_c, p_l, batch)
            lat = _out_proj(y_l, w_out, lat, mod3, base_l, seq)
            if need_ctx:
                y_c = _gqa_attention(p_c, p_c, None, batch)
                cx = _out_proj(y_c, w_out, cx, mod3, base_c, None)
        elif m == 2:
            w_in = pool_w_in[jl].astype(BF16)
            w_grp = pool_w_grp[jl].astype(BF16)
            sc_row = pool_scale[jl].reshape(1, d)
            w_out = pool_w_out[jl].astype(BF16)
            y_l = _pool_mix(_proj(*lat_args, w_in), w_grp, sc_row, seq)
            lat = _out_proj(y_l, w_out, lat, mod3, base_l, seq)
            if need_ctx:
                y_c = _pool_mix(_proj(*ctx_args, w_in), w_grp, sc_row, lctx)
                cx = _out_proj(y_c, w_out, cx, mod3, base_c, None)
        else:
            lam_init = 0.8 - 0.6 * math.exp(-0.3 * i)
            w_in = diff_w_in[jl].astype(BF16)
            heads = (_lane_row(diff_q_norm[jl], 2), _lane_row(diff_k_norm[jl], 2), DIFF_W, DIFF_W, DIFF_DH,
                     DIFF_DH ** -0.5 * LOG2E)
            p_l = _proj(*lat_args, w_in, kind="heads", heads=heads, rope_tabs=rope_diff)
            p_c = _proj(*ctx_args, w_in, kind="heads", heads=heads, j0=DIFF_W // 512, nj=2 * DIFF_W // 512)
            lams = [v[jl].reshape(1, DIFF_DH) for v in (diff_lambda_q1, diff_lambda_k1, diff_lambda_q2, diff_lambda_k2)]
            y_l = _diff_attention(p_l, p_c, lams, _lane_row(diff_sub_norm[jl]), batch, lam_init)
            lat = _out_proj(y_l, diff_w_out[jl].astype(BF16), lat, mod3, base_l, seq)
    return lat.reshape(batch, seq, d)
```

```python
import functools
import math

import jax
import jax.numpy as jnp
from jax import lax
from jax.experimental import pallas as pl
from jax.experimental.pallas import tpu as pltpu

F32 = jnp.float32
BF16 = jnp.bfloat16

D_MODEL = 2048
N_MIXERS = 4
RMS_EPS = 1e-6
L2_EPS = 1e-6
ROPE_THETA = 10000.0
GRID_W = 64

HEAD = 128
KV_TILE = 2048
EPILOGUE_ROWS = 256
LOG2E = math.log2(math.e)
GDN_QK_HEADS = D_MODEL // HEAD
GDN_V_HEADS = 2 * GDN_QK_HEADS
GDN_CONV = 5
GDN_CHUNK = 64
GDN_UNROLL = 4
GDN_INV_EXACT_LEVELS = 6
GDN_QK_W = GDN_QK_HEADS * HEAD
GDN_V_W = GDN_V_HEADS * HEAD
GDN_CONV_CH = 2 * GDN_QK_W + GDN_V_W

GQA_HEADS = D_MODEL // HEAD
GQA_KV_HEADS = GQA_HEADS // 4
GQA_QW = GQA_HEADS * HEAD
GQA_KVW = GQA_KV_HEADS * HEAD

POOL_WINDOWS = (2, 4, 8, 16)
POOL_GW = D_MODEL // len(POOL_WINDOWS)
POOL_HALO = 16

DIFF_DH = 64
DIFF_HEADS = D_MODEL // (2 * DIFF_DH)
DIFF_W = DIFF_HEADS * 2 * DIFF_DH

VMEM_LIMIT_BYTES = 56 * 1024 * 1024
LANES = 128
BF16_SUBLANES = 16
CONV_HALO = 16


def _params(sem):
    return pltpu.CompilerParams(dimension_semantics=sem, vmem_limit_bytes=VMEM_LIMIT_BYTES)


def _tile(n, pref):
    t = min(n, pref)
    while n % t or t % BF16_SUBLANES:
        t -= 1
    return t


def _silu(x):
    return x * jax.nn.sigmoid(x)


def _softplus(x):
    return jnp.maximum(x, 0.0) + jnp.log(1.0 + jnp.exp(-jnp.abs(x)))


def _dot(a, b):
    return jnp.dot(a, b, preferred_element_type=F32)


def _dot_nt(a, b):
    return lax.dot_general(a, b, (((1,), (1,)), ((), ())), preferred_element_type=F32)


def _swap_pairs(x):
    lane = lax.broadcasted_iota(jnp.int32, x.shape, 1)
    return jnp.where(lane % 2 == 0, pltpu.roll(x, LANES - 1, 1), pltpu.roll(x, 1, 1))


def _group_mean_sq(x, group):
    x2 = x * x
    if group == LANES:
        return jnp.mean(x2, axis=-1, keepdims=True)
    lo = lax.broadcasted_iota(jnp.int32, x.shape, 1) < group
    s_lo = jnp.sum(jnp.where(lo, x2, 0.0), axis=-1, keepdims=True)
    s_hi = jnp.sum(jnp.where(lo, 0.0, x2), axis=-1, keepdims=True)
    return jnp.where(lo, s_lo, s_hi) * (1.0 / group)


def _mod_kernel(c_ref, w_ref, b_ref, o_ref):
    a = _silu(c_ref[...]).astype(BF16)
    o_ref[0] = _dot(a, w_ref[0].astype(BF16)) + b_ref[0]


def _mod_call(c_all, mod_w, mod_b):
    depth, d, n = mod_w.shape
    rows = c_all.shape[0]
    tn = _tile(n, 768)
    return pl.pallas_call(
        _mod_kernel,
        grid=(depth, n // tn),
        in_specs=[
            pl.BlockSpec((rows, d), lambda l, j: (0, 0)),
            pl.BlockSpec((1, d, tn), lambda l, j: (l, 0, j)),
            pl.BlockSpec((1, 1, tn), lambda l, j: (l, 0, j)),
        ],
        out_specs=pl.BlockSpec((1, rows, tn), lambda l, j: (l, 0, j)),
        out_shape=jax.ShapeDtypeStruct((depth, rows, n), F32),
        compiler_params=_params(("arbitrary", "arbitrary")),
        name="mod_vectors",
    )(c_all, mod_w, mod_b.reshape(depth, 1, n))


def _head_epilogue(acc, o_ref, gain, cs, scale, group):
    for c in range(acc.shape[1] // LANES):
        xc = acc[:, c * LANES:(c + 1) * LANES]
        y = xc * lax.rsqrt(_group_mean_sq(xc, group) + RMS_EPS) * gain
        if cs is not None:
            y = y * cs[0] + _swap_pairs(y) * cs[1]
        if scale != 1.0:
            y = y * scale
        o_ref[:, c * LANES:(c + 1) * LANES] = y.astype(o_ref.dtype)


def _proj_kernel(*refs, kind, rope, j0, q_tiles, k_tiles, group, q_scale):
    x_ref, g_ref, sh_ref, sc_ref, w_ref = refs[:5]
    rest = list(refs[5:])
    h_ref = rest.pop()
    qn_ref = kn_ref = c_ref = s_ref = None
    if kind == "heads":
        qn_ref, kn_ref = rest.pop(0), rest.pop(0)
        if rope:
            c_ref, s_ref = rest.pop(0), rest.pop(0)
    (o_ref,) = rest
    j = pl.program_id(1)
    jg = j + j0
    tm = h_ref.shape[0]
    rs = math.gcd(tm, EPILOGUE_ROWS)

    def norm_rows(r):
        x = x_ref[r:r + rs, :]
        ms = jnp.mean(x * x, axis=-1, keepdims=True)
        y = x * lax.rsqrt(ms + RMS_EPS) * g_ref[...]
        h_ref[r:r + rs, :] = (y * (1.0 + sc_ref[0]) + sh_ref[0]).astype(BF16)

    def tile(mode, first):
        if mode == "plain" and not first:
            o_ref[...] = _dot(h_ref[...], w_ref[...]).astype(o_ref.dtype)
            return
        gain_ref, scale = (qn_ref, q_scale) if mode == "q" else (kn_ref, 1.0)
        pending = None
        for r in range(0, tm + rs, rs):
            acc = None
            if r < tm:
                if first:
                    norm_rows(r)
                acc = _dot(h_ref[r:r + rs, :], w_ref[...])
            if pending is not None:
                p0, p_acc = pending
                if mode == "plain":
                    o_ref[p0:p0 + rs, :] = p_acc.astype(o_ref.dtype)
                else:
                    cs = (c_ref[p0:p0 + rs, :], s_ref[p0:p0 + rs, :]) if rope else None
                    _head_epilogue(p_acc, o_ref.at[p0:p0 + rs, :], gain_ref[...], cs, scale, group)
            pending = (r, acc)

    def mode_of(col_tile):
        return "q" if col_tile < q_tiles else "k" if col_tile < q_tiles + k_tiles else "plain"

    @pl.when(j == 0)
    def _():
        tile(mode_of(j0), True)

    later = j > 0
    if q_tiles > j0 + 1:
        @pl.when(jnp.logical_and(later, jg < q_tiles))
        def _():
            tile("q", False)

    if k_tiles:
        @pl.when(jnp.logical_and(later, jnp.logical_and(jg >= q_tiles, jg < q_tiles + k_tiles)))
        def _():
            tile("k", False)

    @pl.when(jnp.logical_and(later, jg >= q_tiles + k_tiles))
    def _():
        tile("plain", False)


def _proj(xs, mod3, mod_base, batch_rows, g_row, w, *, kind="plain", heads=None, rope_tabs=None,
          j0=0, nj=None, tn=512, tm_pref=1024):
    m, d = xs.shape
    seq = batch_rows if batch_rows is not None else m
    tm = _tile(seq if rope_tabs is not None or batch_rows is not None else m, tm_pref)
    n_total = w.shape[1] // tn
    nj = n_total - j0 if nj is None else nj
    tiles_per_batch = (batch_rows // tm) if batch_rows is not None else None

    def mod_idx(which):
        if tiles_per_batch is None:
            return lambda i, j: (mod_base * 3 + which, 0, 0)
        return lambda i, j: ((mod_base + i // tiles_per_batch) * 3 + which, 0, 0)

    in_specs = [
        pl.BlockSpec((tm, d), lambda i, j: (i, 0)),
        pl.BlockSpec((1, d), lambda i, j: (0, 0)),
        pl.BlockSpec((1, 1, d), mod_idx(0)),
        pl.BlockSpec((1, 1, d), mod_idx(1)),
        pl.BlockSpec((d, tn), lambda i, j: (0, j + j0)),
    ]
    args = [xs, g_row, mod3, mod3, w]
    kw = dict(kind=kind, rope=rope_tabs is not None, j0=j0, q_tiles=0, k_tiles=0, group=LANES, q_scale=1.0)
    if kind == "heads":
        qn, kn, q_cols, k_cols, group, q_scale = heads
        kw.update(q_tiles=q_cols // tn, k_tiles=k_cols // tn, group=group, q_scale=q_scale)
        in_specs += [pl.BlockSpec((1, LANES), lambda i, j: (0, 0))] * 2
        args += [qn, kn]
        if rope_tabs is not None:
            tps = seq // tm
            in_specs += [pl.BlockSpec((tm, LANES), lambda i, j: (i % tps, 0))] * 2
            args += list(rope_tabs)
    return pl.pallas_call(
        functools.partial(_proj_kernel, **kw),
        grid=(m // tm, nj),
        in_specs=in_specs,
        out_specs=pl.BlockSpec((tm, tn), lambda i, j: (i, j)),
        out_shape=jax.ShapeDtypeStruct((m, nj * tn), BF16),
        scratch_shapes=[pltpu.VMEM((tm, d), BF16)],
        compiler_params=_params(("arbitrary", "arbitrary")),
        name="proj_" + kind,
    )(*args)


def _proj_gdn_kernel(x_ref, xp_ref, xn_ref, g_ref, sh_ref, sc_ref, w_ref, wab_ref, cw_ref, o_ref, ab_ref,
                     h_ref, acc_ref, *, seq):
    tm, tn = o_ref.shape
    j = pl.program_id(1)
    tps = seq // tm
    pos = pl.program_id(0) % tps
    conv_tiles = GDN_CONV_CH // tn
    qk_tiles = GDN_QK_W // tn

    def norm_mod(x):
        ms = jnp.mean(x * x, axis=-1, keepdims=True)
        y = x * lax.rsqrt(ms + RMS_EPS) * g_ref[...]
        return (y * (1.0 + sc_ref[0]) + sh_ref[0]).astype(BF16)

    @pl.when(j >= conv_tiles)
    def _():
        o_ref[...] = _dot(h_ref[CONV_HALO:CONV_HALO + tm, :], w_ref[...]).astype(BF16)

    def conv_tile(first, l2norm):
        has_prev = (pos != 0).astype(F32)
        has_next = (pos != tps - 1).astype(F32)
        rs = math.gcd(tm, EPILOGUE_ROWS)
        nb = tm // rs
        rows = tm + 2 * CONV_HALO
        base = CONV_HALO - GDN_CONV // 2
        out_scale = jnp.where(j < qk_tiles, HEAD ** -0.5, 1.0)

        def matmul_rows(b):
            r0, r1 = b * rs, ((b + 1) * rs if b < nb - 1 else rows)
            if first:
                if b == 0:
                    h_ref[0:CONV_HALO, :] = norm_mod(xp_ref[...])
                x0, x1 = max(r0 - CONV_HALO, 0), min(r1 - CONV_HALO, tm)
                h_ref[x0 + CONV_HALO:x1 + CONV_HALO, :] = norm_mod(x_ref[x0:x1, :])
                if b == nb - 1:
                    h_ref[CONV_HALO + tm:, :] = norm_mod(xn_ref[...])
            acc_ref[r0:r1, :] = _dot(h_ref[r0:r1, :], w_ref[...])
            if b == 0:
                acc_ref[0:CONV_HALO, :] = acc_ref[0:CONV_HALO, :] * has_prev
            if b == nb - 1:
                acc_ref[CONV_HALO + tm:, :] = acc_ref[CONV_HALO + tm:, :] * has_next

        def conv_rows(b):
            r0 = base + b * rs
            y = cw_ref[0:1, :] * acc_ref[r0:r0 + rs, :]
            for k in range(1, GDN_CONV):
                y = y + cw_ref[k:k + 1, :] * acc_ref[r0 + k:r0 + k + rs, :]
            y = _silu(y)
            if not l2norm:
                o_ref[b * rs:(b + 1) * rs, :] = y.astype(BF16)
                return
            for c in range(tn // HEAD):
                sl = slice(c * HEAD, (c + 1) * HEAD)
                yc = y[:, sl]
                inv = lax.rsqrt(jnp.sum(yc * yc, axis=-1, keepdims=True) + L2_EPS) * out_scale
                o_ref[b * rs:(b + 1) * rs, sl] = (yc * inv).astype(BF16)

        matmul_rows(0)
        for b in range(1, nb):
            matmul_rows(b)
            conv_rows(b - 1)
        conv_rows(nb - 1)
        if first:
            ab_ref[...] = _dot(h_ref[CONV_HALO:CONV_HALO + tm, :], wab_ref[...])

    @pl.when(j == 0)
    def _():
        conv_tile(True, True)

    @pl.when(jnp.logical_and(j > 0, j < 2 * qk_tiles))
    def _():
        conv_tile(False, True)

    @pl.when(jnp.logical_and(j >= 2 * qk_tiles, j < conv_tiles))
    def _():
        conv_tile(False, False)


def _proj_gdn(xs, mod3, mod_base, batch_rows, g_row, w, wab, conv_w, seq, *, tn=512, tm_pref=1024):
    m, d = xs.shape
    tm = _tile(seq, tm_pref)
    hb = tm // CONV_HALO
    last = m // CONV_HALO - 1
    tiles_per_batch = (batch_rows // tm) if batch_rows is not None else None
    conv_last = GDN_CONV_CH // tn - 1

    def mod_idx(which):
        if tiles_per_batch is None:
            return lambda i, j: (mod_base * 3 + which, 0, 0)
        return lambda i, j: ((mod_base + i // tiles_per_batch) * 3 + which, 0, 0)

    return pl.pallas_call(
        functools.partial(_proj_gdn_kernel, seq=seq),
        grid=(m // tm, w.shape[1] // tn),
        in_specs=[
            pl.BlockSpec((tm, d), lambda i, j: (i, 0)),
            pl.BlockSpec((CONV_HALO, d), lambda i, j: (jnp.maximum(i * hb - 1, 0), 0)),
            pl.BlockSpec((CONV_HALO, d), lambda i, j: (jnp.minimum((i + 1) * hb, last), 0)),
            pl.BlockSpec((1, d), lambda i, j: (0, 0)),
            pl.BlockSpec((1, 1, d), mod_idx(0)),
            pl.BlockSpec((1, 1, d), mod_idx(1)),
            pl.BlockSpec((d, tn), lambda i, j: (0, j)),
            pl.BlockSpec((d, LANES), lambda i, j: (0, 0)),
            pl.BlockSpec((GDN_CONV, tn), lambda i, j: (0, jnp.minimum(j, conv_last))),
        ],
        out_specs=[pl.BlockSpec((tm, tn), lambda i, j: (i, j)),
                   pl.BlockSpec((tm, LANES), lambda i, j: (i, 0))],
        out_shape=[jax.ShapeDtypeStruct((m, w.shape[1]), BF16), jax.ShapeDtypeStruct((m, LANES), F32)],
        scratch_shapes=[pltpu.VMEM((tm + 2 * CONV_HALO, d), BF16), pltpu.VMEM((tm + 2 * CONV_HALO, tn), F32)],
        compiler_params=_params(("arbitrary", "arbitrary")),
        name="proj_gdn_conv",
    )(xs, xs, xs, g_row, mod3, mod3, w, wab, conv_w)


def _out_kernel(y_ref, w_ref, x_ref, gate_ref, out_ref):
    step = math.gcd(out_ref.shape[1], 512)
    for c in range(0, out_ref.shape[1], step):
        sl = slice(c, c + step)
        out_ref[:, sl] = x_ref[:, sl] + gate_ref[0][:, sl] * _dot(y_ref[...], w_ref[:, sl])


def _out_proj(y, w, xs, mod3, mod_base, batch_rows):
    m, d = xs.shape
    kdim = w.shape[0]
    resident = 2 * w.size * w.dtype.itemsize <= VMEM_LIMIT_BYTES // 3
    tn = d if resident else 512
    tm = _tile(batch_rows if batch_rows is not None else m, 512 if resident else 1024)
    tiles_per_batch = (batch_rows // tm) if batch_rows is not None else None
    if tiles_per_batch is None:
        gate_idx = lambda i, j: (mod_base * 3 + 2, 0, j)
    else:
        gate_idx = lambda i, j: ((mod_base + i // tiles_per_batch) * 3 + 2, 0, j)
    return pl.pallas_call(
        _out_kernel,
        grid=(m // tm, d // tn),
        in_specs=[
            pl.BlockSpec((tm, kdim), lambda i, j: (i, 0)),
            pl.BlockSpec((kdim, tn), lambda i, j: (0, j)),
            pl.BlockSpec((tm, tn), lambda i, j: (i, j)),
            pl.BlockSpec((1, 1, tn), gate_idx),
        ],
        out_specs=pl.BlockSpec((tm, tn), lambda i, j: (i, j)),
        out_shape=jax.ShapeDtypeStruct((m, d), F32),
        compiler_params=_params(("arbitrary", "arbitrary")),
        name="out_proj",
    )(y, w, xs, mod3)


def _flash(queries, kv_refs):
    tq = queries[0].shape[0]
    m = [jnp.full((tq, 1), -1e30, F32) for _ in queries]
    l = [jnp.zeros((tq, 1), F32) for _ in queries]
    acc = [jnp.zeros((tq, HEAD), F32) for _ in queries]
    for k_ref, v_ref in kv_refs:
        rows = k_ref.shape[0]
        tk = math.gcd(rows, KV_TILE)
        for t in range(rows // tk):
            k = k_ref[t * tk:(t + 1) * tk, :]
            v = v_ref[t * tk:(t + 1) * tk, :]
            for i, q in enumerate(queries):
                s = _dot_nt(q, k)
                m_new = jnp.maximum(m[i], jnp.max(s, axis=-1, keepdims=True))
                alpha = jnp.exp2(m[i] - m_new)
                p = jnp.exp2(s - m_new)
                l[i] = alpha * l[i] + jnp.sum(p, axis=-1, keepdims=True)
                acc[i] = alpha * acc[i] + _dot(p.astype(BF16), v)
                m[i] = m_new
    return [a / li for a, li in zip(acc, l)]


def _gqa_attn_kernel(*refs, has_lat):
    if has_lat:
        q_ref, z_ref, kc_ref, vc_ref, kl_ref, vl_ref, y_ref = refs
        kv = [(kc_ref, vc_ref), (kl_ref, vl_ref)]
    else:
        q_ref, z_ref, kc_ref, vc_ref, y_ref = refs
        kv = [(kc_ref, vc_ref)]
    tq = q_ref.shape[0]
    heads = [slice(g * HEAD, (g + 1) * HEAD) for g in range(q_ref.shape[1] // HEAD)]
    (o,) = _flash([jnp.concatenate([q_ref[:, sl] for sl in heads], axis=0)], kv)
    for g, sl in enumerate(heads):
        y_ref[:, sl] = (o[g * tq:(g + 1) * tq] * _silu(z_ref[:, sl].astype(F32))).astype(BF16)


def _gqa_attention(p_q, p_c, p_l, batch, *, tq_pref=256):
    has_lat = p_l is not None
    m = p_q.shape[0]
    rows_q = m // batch
    tq = _tile(rows_q, tq_pref)
    nq = rows_q // tq
    lc = p_c.shape[0] // batch
    grp = GQA_QW // GQA_KV_HEADS
    k_blk, v_blk, z_blk = GQA_QW // HEAD, (GQA_QW + GQA_KVW) // HEAD, (GQA_QW + 2 * GQA_KVW) // grp
    in_specs = [
        pl.BlockSpec((tq, grp), lambda b, h, i: (b * nq + i, h)),
        pl.BlockSpec((tq, grp), lambda b, h, i: (b * nq + i, z_blk + h)),
        pl.BlockSpec((lc, HEAD), lambda b, h, i: (b, k_blk + h)),
        pl.BlockSpec((lc, HEAD), lambda b, h, i: (b, v_blk + h)),
    ]
    args = [p_q, p_q, p_c, p_c]
    if has_lat:
        ls = p_l.shape[0] // batch
        in_specs += [pl.BlockSpec((ls, HEAD), lambda b, h, i: (b, k_blk + h)),
                     pl.BlockSpec((ls, HEAD), lambda b, h, i: (b, v_blk + h))]
        args += [p_l, p_l]
    return pl.pallas_call(
        functools.partial(_gqa_attn_kernel, has_lat=has_lat),
        grid=(batch, GQA_KV_HEADS, nq),
        in_specs=in_specs,
        out_specs=pl.BlockSpec((tq, grp), lambda b, h, i: (b * nq + i, h)),
        out_shape=jax.ShapeDtypeStruct((m, GQA_QW), BF16),
        compiler_params=_params(("arbitrary", "arbitrary", "arbitrary")),
        name="gqa_attention",
    )(*args)


def _diff_attn_kernel(lq1_ref, lk1_ref, lq2_ref, lk2_ref, n_ref, q_ref, z_ref, kc_ref, vc_ref, kl_ref, vl_ref,
                      y_ref, *, lam_init):
    lam = (jnp.exp(jnp.sum(lq1_ref[...] * lk1_ref[...], axis=-1, keepdims=True))
           - jnp.exp(jnp.sum(lq2_ref[...] * lk2_ref[...], axis=-1, keepdims=True)) + lam_init)
    q = q_ref[...]
    lo = lax.broadcasted_iota(jnp.int32, q.shape, 1) < DIFF_DH
    zero = jnp.zeros_like(q)
    tq = q.shape[0]
    (o01,) = _flash([jnp.concatenate([jnp.where(lo, q, zero), jnp.where(lo, zero, q)], axis=0)],
                    [(kc_ref, vc_ref), (kl_ref, vl_ref)])
    o = o01[0:tq] - lam * o01[tq:2 * tq]
    y = o * lax.rsqrt(jnp.mean(o * o, axis=-1, keepdims=True) + RMS_EPS) * n_ref[...] * (1.0 - lam_init)
    y_ref[...] = (y * _silu(z_ref[...].astype(F32))).astype(BF16)


def _diff_attention(p_l, p_c, lams, sub_norm, batch, lam_init, *, tq_pref=512):
    m = p_l.shape[0]
    ls = m // batch
    lc = p_c.shape[0] // batch
    tq = _tile(ls, tq_pref)
    nq = ls // tq
    nh = DIFF_HEADS
    small = pl.BlockSpec((1, DIFF_DH), lambda b, h, i: (0, 0))
    in_specs = [small] * 4 + [
        pl.BlockSpec((1, HEAD), lambda b, h, i: (0, 0)),
        pl.BlockSpec((tq, HEAD), lambda b, h, i: (b * nq + i, h)),
        pl.BlockSpec((tq, HEAD), lambda b, h, i: (b * nq + i, 3 * nh + h)),
        pl.BlockSpec((lc, HEAD), lambda b, h, i: (b, h)),
        pl.BlockSpec((lc, HEAD), lambda b, h, i: (b, nh + h)),
        pl.BlockSpec((ls, HEAD), lambda b, h, i: (b, nh + h)),
        pl.BlockSpec((ls, HEAD), lambda b, h, i: (b, 2 * nh + h)),
    ]
    return pl.pallas_call(
        functools.partial(_diff_attn_kernel, lam_init=lam_init),
        grid=(batch, nh, nq),
        in_specs=in_specs,
        out_specs=pl.BlockSpec((tq, HEAD), lambda b, h, i: (b * nq + i, h)),
        out_shape=jax.ShapeDtypeStruct((m, DIFF_W), BF16),
        compiler_params=_params(("arbitrary", "arbitrary", "arbitrary")),
        name="diff_attention",
    )(*lams, sub_norm, p_l, p_l, p_c, p_c, p_l, p_l)


def _pool_kernel(u_ref, up_ref, un_ref, z_ref, wg_ref, sc_ref, y_ref, ext_ref, *, seq):
    tm = u_ref.shape[0]
    tps = seq // tm
    pos = pl.program_id(0) % tps
    has_prev = (pos != 0).astype(F32)
    has_next = (pos != tps - 1).astype(F32)
    ext_ref[0:POOL_HALO, :] = up_ref[...].astype(F32) * has_prev
    ext_ref[POOL_HALO:POOL_HALO + tm, :] = u_ref[...].astype(F32)
    ext_ref[POOL_HALO + tm:, :] = un_ref[...].astype(F32) * has_next
    t = pos * tm + lax.broadcasted_iota(jnp.int32, (tm, 1), 0)
    for gi, w in enumerate(POOL_WINDOWS):
        cols = slice(gi * POOL_GW, (gi + 1) * POOL_GW)
        tot = ext_ref[POOL_HALO - w // 2:POOL_HALO - w // 2 + tm, cols]
        for k in range(1, w):
            tot = tot + ext_ref[POOL_HALO - w // 2 + k:POOL_HALO - w // 2 + k + tm, cols]
        cnt = jnp.minimum(t - w // 2 + w, seq) - jnp.maximum(t - w // 2, 0)
        dgrp = tot / cnt.astype(F32) - ext_ref[POOL_HALO:POOL_HALO + tm, cols]
        r = _dot(dgrp.astype(BF16), wg_ref[gi]) * sc_ref[:, cols]
        y_ref[:, cols] = (r * _silu(z_ref[:, cols].astype(F32))).astype(BF16)


def _pool_mix(p, w_grp, ch_scale, seq):
    m = p.shape[0]
    w_u = D_MODEL
    tm = _tile(seq, 256)
    hb = tm // POOL_HALO
    last = m // POOL_HALO - 1
    return pl.pallas_call(
        functools.partial(_pool_kernel, seq=seq),
        grid=(m // tm,),
        in_specs=[
            pl.BlockSpec((tm, w_u), lambda i: (i, 0)),
            pl.BlockSpec((POOL_HALO, w_u), lambda i: (jnp.maximum(i * hb - 1, 0), 0)),
            pl.BlockSpec((POOL_HALO, w_u), lambda i: (jnp.minimum((i + 1) * hb, last), 0)),
            pl.BlockSpec((tm, w_u), lambda i: (i, 1)),
            pl.BlockSpec(w_grp.shape, lambda i: (0, 0, 0)),
            pl.BlockSpec((1, w_u), lambda i: (0, 0)),
        ],
        out_specs=pl.BlockSpec((tm, w_u), lambda i: (i, 0)),
        out_shape=jax.ShapeDtypeStruct((m, w_u), BF16),
        scratch_shapes=[pltpu.VMEM((tm + 2 * POOL_HALO, w_u), F32)],
        compiler_params=_params(("arbitrary",)),
        name="pool_mix",
    )(p, p, p, p, w_grp, ch_scale)


def _split3(x):
    hi = x.astype(BF16)
    r1 = x - hi.astype(F32)
    mid = r1.astype(BF16)
    lo = (r1 - mid.astype(F32)).astype(BF16)
    return hi, mid, lo


def _gdn_gates_kernel(ab_ref, alog_ref, dtb_ref, gb_ref, gt_ref):
    tm = ab_ref.shape[0]
    ab = ab_ref[...]
    lane = lax.broadcasted_iota(jnp.int32, (1, LANES), 1)
    is_g = (lane % (2 * GDN_V_HEADS)) < GDN_V_HEADS
    gval = -jnp.exp(alog_ref[...]) * _softplus(ab + dtb_ref[...])
    g = jnp.where(is_g, gval, 0.0)
    r = lax.broadcasted_iota(jnp.int32, (tm, tm), 0)
    c = lax.broadcasted_iota(jnp.int32, (tm, tm), 1)
    same = (r // GDN_CHUNK) == (c // GDN_CHUNK)
    tri_f = jnp.where(jnp.logical_and(same, r >= c), 1.0, 0.0).astype(BF16)
    tri_b = jnp.where(jnp.logical_and(same, r <= c), 1.0, 0.0).astype(BF16)
    pieces = _split3(g)
    gc_f = functools.reduce(jnp.add, [_dot(tri_f, x) for x in pieces])
    gc_b = functools.reduce(jnp.add, [_dot(tri_b, x) for x in pieces])
    gc = jnp.where(lane < 2 * GDN_V_HEADS, gc_f, gc_b)
    gb_ref[...] = jnp.where(is_g, gc, jax.nn.sigmoid(ab))
    for ci in range(tm // GDN_CHUNK):
        tile = gc[ci * GDN_CHUNK:(ci + 1) * GDN_CHUNK, :]
        tt = jnp.concatenate([tile, tile], axis=0).T
        gt_ref[ci, 0:GDN_V_HEADS, :] = tt[0:GDN_V_HEADS, :]
        gt_ref[ci, GDN_V_HEADS:, :] = tt[2 * GDN_V_HEADS:3 * GDN_V_HEADS, :]


def _gdn_gates(ab, alog_row, dtb_row, seq):
    m = ab.shape[0]
    tm = _tile(seq, 512)
    return pl.pallas_call(
        _gdn_gates_kernel,
        grid=(m // tm,),
        in_specs=[
            pl.BlockSpec((tm, LANES), lambda i: (i, 0)),
            pl.BlockSpec((1, LANES), lambda i: (0, 0)),
            pl.BlockSpec((1, LANES), lambda i: (0, 0)),
        ],
        out_specs=[
            pl.BlockSpec((tm, LANES), lambda i: (i, 0)),
            pl.BlockSpec((tm // GDN_CHUNK, 2 * GDN_V_HEADS, LANES), lambda i: (i, 0, 0)),
        ],
        out_shape=[
            jax.ShapeDtypeStruct((m, LANES), F32),
            jax.ShapeDtypeStruct((m // GDN_CHUNK, 2 * GDN_V_HEADS, LANES), F32),
        ],
        compiler_params=_params(("arbitrary",)),
        name="gdn_gates",
    )(ab, alog_row, dtb_row)


def _gdn_core_kernel(qc_ref, kc_ref, vc_ref, gbc_ref, gtc_ref, zc_ref, ql_ref, kl_ref, vl_ref, gbl_ref, gtl_ref,
                     zl_ref, n_ref, yc_ref, yl_ref, oc_ref, ol_ref, s_ref, wq_ref, u_ref, qkm_ref, kt_ref,
                     ksc_ref, gtot_ref):
    C = GDN_CHUNK
    jh = pl.program_id(1)
    s_ref[...] = jnp.zeros_like(s_ref)
    oc_ref[...] = jnp.zeros_like(oc_ref)
    ol_ref[...] = jnp.zeros_like(ol_ref)

    lane_w = lax.broadcasted_iota(jnp.int32, (C, 2 * LANES), 1)
    row_w = lax.broadcasted_iota(jnp.int32, (C, 2 * LANES), 0)
    col_w = lane_w % C
    fwd_w = lane_w < LANES
    ahead_w = jnp.where(fwd_w, row_w - col_w, col_w - row_w)
    incl_w = ahead_w >= 0
    strict_w = ahead_w > 0
    prob_w = lane_w // C
    prob_masks = [jnp.where(prob_w == p, 1.0, 0.0).astype(BF16) for p in range(4)]
    lane_p = lax.broadcasted_iota(jnp.int32, (C, LANES), 1)
    row_p = lax.broadcasted_iota(jnp.int32, (C, LANES), 0)
    first_p = lane_p < C
    lane_s = lax.broadcasted_iota(jnp.int32, (1, 2 * LANES), 1)
    lo_s = lane_s < LANES

    def block_diag(a):
        return jnp.concatenate([a * mk for mk in prob_masks], axis=0)

    def chunk_rows(ci):
        return ci * C if isinstance(ci, int) else pl.multiple_of(ci * C, C)

    def pair_blocks(a):
        zero = jnp.zeros((a.shape[0], LANES), a.dtype)
        return jnp.concatenate([jnp.concatenate([a[:, :LANES], zero], axis=1),
                                jnp.concatenate([zero, a[:, LANES:]], axis=1)], axis=0)

    def split2(a):
        hi = a.astype(BF16)
        return hi, (a - hi.astype(F32)).astype(BF16)

    def precompute(src, steps):
        q_ref, k_ref, v_ref, gb_ref, gt_ref = src
        nchunks = q_ref.shape[0] // C
        todo = range(len(steps))
        st = []
        for n, _ in steps:
            per_dir = []
            for d in range(2):
                ci = n if d == 0 else nchunks - 1 - n
                r0 = chunk_rows(ci)
                q = q_ref[pl.ds(r0, C), :]
                k = k_ref[pl.ds(r0, C), :]
                v = v_ref[pl.ds(r0, C), :].astype(F32)
                gb = gb_ref[pl.ds(r0, C), :]
                gt = gt_ref[ci]
                g_lane = d * 2 * GDN_V_HEADS + 2 * jh
                cols = []
                for lane0 in (g_lane, g_lane + GDN_V_HEADS):
                    e0 = jnp.sum(jnp.where(lane_p == lane0, gb, 0.0), axis=1, keepdims=True)
                    e1 = jnp.sum(jnp.where(lane_p == lane0 + 1, gb, 0.0), axis=1, keepdims=True)
                    cols.append((e0, e1))
                (gc0, gc1), (be0, be1) = cols
                g_row = d * GDN_V_HEADS + 2 * jh
                pick = row_p == jnp.where(first_p, g_row, g_row + 1)
                gc_row = jnp.sum(jnp.where(pick, gt, 0.0), axis=0, keepdims=True)
                kq = _dot_nt(jnp.concatenate([k, q], axis=0), jnp.concatenate([k, k], axis=0))
                last = C - 1 if d == 0 else 0
                per_dir.append(dict(q=q.astype(F32), k=k.astype(F32), v=v, gc=(gc0, gc1), be=(be0, be1),
                                    gc_col=jnp.where(first_p, gc0, gc1), be_col=jnp.where(first_p, be0, be1),
                                    gc_row=gc_row, kk=kq[0:C], qk=kq[C:2 * C],
                                    tot=(gc0[last:last + 1], gc1[last:last + 1])))
            st.append(per_dir)
        yield

        ms, qkms = [], []
        for per_dir in st:
            wide = lambda key: jnp.concatenate([per_dir[0][key], per_dir[1][key]], axis=1)
            dm = wide("gc_col") - wide("gc_row")
            decay = jnp.where(incl_w, jnp.exp(jnp.where(incl_w, dm, 0.0)), 0.0)
            ms.append(-jnp.where(strict_w, wide("be_col") * wide("kk") * decay, 0.0))
            qkms.append(wide("qk") * decay)

        xs = list(ms)
        for lvl in range(6):
            lhs = [ms[i] if lvl == 0 else xs[i] if lvl == 5 else jnp.concatenate([ms[i], xs[i]], axis=0)
                   for i in todo]
            if lvl < GDN_INV_EXACT_LEVELS:
                lhs = [split2(a) for a in lhs]
                rhs = [tuple(block_diag(piece) for piece in split2(ms[i])) for i in todo]
                prod = [_dot(jnp.concatenate([lhs[i][0], lhs[i][1], lhs[i][0]], axis=1),
                             jnp.concatenate([rhs[i][0], rhs[i][0], rhs[i][1]], axis=0)) for i in todo]
            else:
                prod = [_dot(lhs[i].astype(BF16), block_diag(ms[i].astype(BF16))) for i in todo]
            yield
            for i in todo:
                if lvl == 0:
                    ms[i] = prod[i]
                elif lvl < 5:
                    xs[i] = xs[i] + ms[i] + prod[i][C:2 * C]
                    ms[i] = prod[i][0:C]
                else:
                    xs[i] = xs[i] + ms[i] + prod[i]

        rhs_all = []
        for per_dir in st:
            rhs = []
            for d in range(2):
                pd = per_dir[d]
                for e in range(2):
                    be, egc = pd["be"][e], jnp.exp(pd["gc"][e])
                    rhs.append(jnp.concatenate([be * pd["v"][:, e * HEAD:(e + 1) * HEAD], (be * egc) * pd["k"]],
                                               axis=1))
            rhs_all.append(jnp.concatenate(rhs, axis=0))
        xr = [_dot(block_diag(xs[i].astype(BF16)), rhs_all[i].astype(BF16)) for i in todo]
        yield

        for i in todo:
            uw = rhs_all[i] + xr[i]
            slot = steps[i][1]
            for d in range(2):
                pd = st[i][d]
                u = jnp.concatenate([uw[(2 * d + e) * C:(2 * d + e + 1) * C, 0:HEAD] for e in range(2)], axis=1)
                w = jnp.concatenate([uw[(2 * d + e) * C:(2 * d + e + 1) * C, HEAD:] for e in range(2)], axis=1)
                qdec = jnp.concatenate([pd["q"] * jnp.exp(pd["gc"][e]) for e in range(2)], axis=1)
                tot0, tot1 = pd["tot"]
                kscale = jnp.concatenate([jnp.broadcast_to(jnp.exp(tot0 - pd["gc"][0]), (C, HEAD)),
                                          jnp.broadcast_to(jnp.exp(tot1 - pd["gc"][1]), (C, HEAD))], axis=1)
                wq_ref[slot, d] = jnp.concatenate([w, qdec], axis=0).astype(BF16)
                u_ref[slot, d] = u
                qkm_ref[slot, d] = qkms[i][:, d * LANES:(d + 1) * LANES].astype(BF16)
                kt_ref[slot, d] = pd["k"].T.astype(BF16)
                ksc_ref[slot, d] = kscale
                gtot_ref[slot, d] = jnp.where(lo_s, jnp.exp(tot0), jnp.exp(tot1))
        yield

    def recur(o_ref, steps):
        nchunks = o_ref.shape[0] // C
        for n, slot in steps:
            state = [s_ref[d] for d in range(2)]
            ws_qs = [_dot(wq_ref[slot, d], pair_blocks(state[d].astype(BF16))) for d in range(2)]
            yield
            v_new = [u_ref[slot, d] - ws_qs[d][0:C] for d in range(2)]
            for d in range(2):
                ci = n if d == 0 else nchunks - 1 - n
                o = ws_qs[d][C:2 * C] + _dot(qkm_ref[slot, d], pair_blocks(v_new[d].astype(BF16)))
                o_ref[pl.ds(chunk_rows(ci), C), :] += o
                ds = _dot(kt_ref[slot, d], (v_new[d] * ksc_ref[slot, d]).astype(BF16))
                s_ref[d] = state[d] * gtot_ref[slot, d] + ds
            yield

    def interleave(*gens):
        gens = list(gens)
        while gens:
            for g in list(gens):
                if next(g, gens) is gens:
                    gens.remove(g)

    def phase(q_ref, k_ref, v_ref, gb_ref, gt_ref, o_ref):
        nchunks = q_ref.shape[0] // C
        unroll = math.gcd(nchunks, GDN_UNROLL)
        groups = nchunks // unroll
        src = (q_ref, k_ref, v_ref, gb_ref, gt_ref)
        interleave(precompute(src, [(t, t) for t in range(unroll)]))

        def body(g, carry):
            base = (g % 2) * unroll
            nxt = unroll - base
            ahead = [(jnp.minimum((g + 1) * unroll + t, nchunks - 1), nxt + t) for t in range(unroll)]
            interleave(recur(o_ref, [(g * unroll + t, base + t) for t in range(unroll)]), precompute(src, ahead))
            return carry

        if groups > 1:
            lax.fori_loop(0, groups, body, 0)
        else:
            interleave(recur(o_ref, [(t, t) for t in range(unroll)]))

    phase(qc_ref, kc_ref, vc_ref, gbc_ref, gtc_ref, oc_ref)
    phase(ql_ref, kl_ref, vl_ref, gbl_ref, gtl_ref, ol_ref)

    def finish(o_ref, z_ref, y_ref):
        rows = o_ref.shape[0]
        blk = math.gcd(rows, 512)
        gain = n_ref[...]

        def body(i, carry):
            r0 = pl.multiple_of(i * blk, blk)
            for e in range(2):
                sl = slice(e * HEAD, (e + 1) * HEAD)
                o = o_ref[pl.ds(r0, blk), sl]
                y = o * lax.rsqrt(jnp.mean(o * o, axis=-1, keepdims=True) + RMS_EPS) * gain
                y_ref[pl.ds(r0, blk), sl] = (y * _silu(z_ref[pl.ds(r0, blk), sl].astype(F32))).astype(BF16)
            return carry

        lax.fori_loop(0, rows // blk, body, 0)

    finish(oc_ref, zc_ref, yc_ref)
    finish(ol_ref, zl_ref, yl_ref)


def _gdn_core(qkv_c, gb_c, gt_c, qkv_l, gb_l, gt_l, norm_row, batch):
    lc = qkv_c.shape[0] // batch
    ls = qkv_l.shape[0] // batch
    nqk = GDN_QK_HEADS
    pair = 2 * HEAD
    slots = 2 * GDN_UNROLL

    def specs(rows):
        return [
            pl.BlockSpec((rows, HEAD), lambda b, j: (b, j)),
            pl.BlockSpec((rows, HEAD), lambda b, j: (b, nqk + j)),
            pl.BlockSpec((rows, pair), lambda b, j: (b, nqk + j)),
            pl.BlockSpec((rows, LANES), lambda b, j: (b, 0)),
            pl.BlockSpec((rows // GDN_CHUNK, 2 * GDN_V_HEADS, LANES), lambda b, j: (b, 0, 0)),
            pl.BlockSpec((rows, pair), lambda b, j: (b, GDN_CONV_CH // pair + j)),
        ]

    return pl.pallas_call(
        _gdn_core_kernel,
        grid=(batch, nqk),
        in_specs=specs(lc) + specs(ls) + [pl.BlockSpec((1, LANES), lambda b, j: (0, 0))],
        out_specs=[pl.BlockSpec((lc, pair), lambda b, j: (b, j)),
                   pl.BlockSpec((ls, pair), lambda b, j: (b, j))],
        out_shape=[jax.ShapeDtypeStruct((batch * lc, GDN_V_W), BF16),
                   jax.ShapeDtypeStruct((batch * ls, GDN_V_W), BF16)],
        scratch_shapes=[
            pltpu.VMEM((lc, pair), F32),
            pltpu.VMEM((ls, pair), F32),
            pltpu.VMEM((2, HEAD, pair), F32),
            pltpu.VMEM((slots, 2, 2 * GDN_CHUNK, pair), BF16),
            pltpu.VMEM((slots, 2, GDN_CHUNK, pair), F32),
            pltpu.VMEM((slots, 2, GDN_CHUNK, HEAD), BF16),
            pltpu.VMEM((slots, 2, HEAD, GDN_CHUNK), BF16),
            pltpu.VMEM((slots, 2, GDN_CHUNK, pair), F32),
            pltpu.VMEM((slots, 2, 1, pair), F32),
        ],
        compiler_params=_params(("arbitrary", "arbitrary")),
        name="gdn_core",
    )(qkv_c, qkv_c, qkv_c, gb_c, gt_c, qkv_c, qkv_l, qkv_l, qkv_l, gb_l, gt_l, qkv_l, norm_row)


def _rope_tables(n_tokens, head_dim):
    rows = n_tokens // GRID_W
    r = jnp.repeat(jnp.arange(rows, dtype=F32), GRID_W)
    col = jnp.tile(jnp.arange(GRID_W, dtype=F32), rows)
    d_axis = head_dim // 2
    inv = ROPE_THETA ** (-jnp.arange(0, d_axis, 2, dtype=F32) / d_axis)
    ang = jnp.concatenate([r[:, None] * inv, col[:, None] * inv], axis=-1)
    cos = jnp.repeat(jnp.cos(ang), 2, axis=-1)
    sin = jnp.repeat(jnp.sin(ang), 2, axis=-1) * jnp.tile(jnp.array([-1.0, 1.0], F32), head_dim // 2)
    reps = LANES // head_dim
    return jnp.tile(cos, (1, reps)), jnp.tile(sin, (1, reps))


def _lane_row(v, reps=1):
    return jnp.tile(v.astype(F32), reps).reshape(1, -1)


def kernel(x, c, ctx, c_ctx, norm_g, mod_w, mod_b, gdn_w_in, gdn_conv_w, gdn_a_log, gdn_dt_bias, gdn_out_norm, gdn_w_out, gqa_w_in, gqa_q_norm, gqa_k_norm, gqa_w_out, pool_w_in, pool_w_grp, pool_scale, pool_w_out, diff_w_in, diff_q_norm, diff_k_norm, diff_lambda_q1, diff_lambda_k1, diff_lambda_q2, diff_lambda_k2, diff_sub_norm, diff_w_out):
    batch, seq, d = x.shape
    lctx = ctx.shape[1]
    depth = norm_g.shape[0]
    lat = x.reshape(batch * seq, d)
    cx = ctx.reshape(batch * lctx, d)

    mod_rows = -(-(batch + 1) // 8) * 8
    c_all = jnp.concatenate([c, c_ctx[None, :], jnp.zeros((mod_rows - batch - 1, d), F32)], axis=0)
    mod3 = _mod_call(c_all, mod_w, mod_b).reshape(depth * mod_rows * 3, 1, d)

    rope_gqa = _rope_tables(seq, HEAD)
    rope_diff = _rope_tables(seq, DIFF_DH)

    for i in range(depth):
        m, jl = i % N_MIXERS, i // N_MIXERS
        need_ctx = i < depth - 1
        base_l, base_c = i * mod_rows, i * mod_rows + batch
        g_row = norm_g[i].reshape(1, d)
        lat_args = (lat, mod3, base_l, seq, g_row)
        ctx_args = (cx, mod3, base_c, None, g_row)
        if m == 0:
            w_in = gdn_w_in[jl]
            w_main = w_in[:, :GDN_CONV_CH + GDN_V_W].astype(BF16)
            w_ab = w_in[:, GDN_CONV_CH + GDN_V_W:].astype(BF16)
            conv_w = gdn_conv_w[jl]
            zeros = jnp.zeros((GDN_V_HEADS,), F32)
            alog_row = jnp.concatenate([gdn_a_log[jl, 0], zeros, gdn_a_log[jl, 1], zeros]).reshape(1, LANES)
            dtb_row = jnp.concatenate([gdn_dt_bias[jl, 0], zeros, gdn_dt_bias[jl, 1], zeros]).reshape(1, LANES)
            p_l, ab_l = _proj_gdn(*lat_args, w_main, w_ab, conv_w, seq)
            p_c, ab_c = _proj_gdn(*ctx_args, w_main, w_ab, conv_w, lctx, tn=1024)
            gb_l, gt_l = _gdn_gates(ab_l, alog_row, dtb_row, seq)
            gb_c, gt_c = _gdn_gates(ab_c, alog_row, dtb_row, lctx)
            y_c, y_l = _gdn_core(p_c, gb_c, gt_c, p_l, gb_l, gt_l, _lane_row(gdn_out_norm[jl]), batch)
            w_out = gdn_w_out[jl].astype(BF16)
            lat = _out_proj(y_l, w_out, lat, mod3, base_l, seq)
            if need_ctx:
                cx = _out_proj(y_c, w_out, cx, mod3, base_c, None)
        elif m == 1:
            w_in = gqa_w_in[jl].astype(BF16)
            heads = (_lane_row(gqa_q_norm[jl]), _lane_row(gqa_k_norm[jl]), GQA_QW, GQA_KVW, HEAD,
                     HEAD ** -0.5 * LOG2E)
            p_l = _proj(*lat_args, w_in, kind="heads", heads=heads, rope_tabs=rope_gqa)
            p_c = _proj(*ctx_args, w_in, kind="heads", heads=heads)
            w_out = gqa_w_out[jl].astype(BF16)
            y_l = _gqa_attention(p_l, p_c, p_l, batch)
            lat = _out_proj(y_l, w_out, lat, mod3, base_l, seq)
            if need_ctx:
                y_c = _gqa_attention(p_c, p_c, None, batch)
                cx = _out_proj(y_c, w_out, cx, mod3, base_c, None)
        elif m == 2:
            w_in = pool_w_in[jl].astype(BF16)
            w_grp = pool_w_grp[jl].astype(BF16)
            sc_row = pool_scale[jl].reshape(1, d)
            w_out = pool_w_out[jl].astype(BF16)
            y_l = _pool_mix(_proj(*lat_args, w_in), w_grp, sc_row, seq)
            lat = _out_proj(y_l, w_out, lat, mod3, base_l, seq)
            if need_ctx:
                y_c = _pool_mix(_proj(*ctx_args, w_in), w_grp, sc_row, lctx)
                cx = _out_proj(y_c, w_out, cx, mod3, base_c, None)
        else:
            lam_init = 0.8 - 0.6 * math.exp(-0.3 * i)
            w_in = diff_w_in[jl].astype(BF16)
            heads = (_lane_row(diff_q_norm[jl], 2), _lane_row(diff_k_norm[jl], 2), DIFF_W, DIFF_W, DIFF_DH,
                     DIFF_DH ** -0.5 * LOG2E)
            p_l = _proj(*lat_args, w_in, kind="heads", heads=heads, rope_tabs=rope_diff)
            p_c = _proj(*ctx_args, w_in, kind="heads", heads=heads, j0=DIFF_W // 512, nj=2 * DIFF_W // 512)
            lams = [v[jl].reshape(1, DIFF_DH) for v in (diff_lambda_q1, diff_lambda_k1, diff_lambda_q2, diff_lambda_k2)]
            y_l = _diff_attention(p_l, p_c, lams, _lane_row(diff_sub_norm[jl]), batch, lam_init)
            lat = _out_proj(y_l, diff_w_out[jl].astype(BF16), lat, mod3, base_l, seq)
            assert not need_ctx, "a differential layer that still feeds a context stream is not supported"
    return lat.reshape(batch, seq, d)
```

```python
import functools
import math

import jax
import jax.numpy as jnp
from jax import lax
from jax.experimental import pallas as pl
from jax.experimental.pallas import tpu as pltpu

F32 = jnp.float32
BF16 = jnp.bfloat16

D_MODEL = 2048
N_MIXERS = 4
RMS_EPS = 1e-6
L2_EPS = 1e-6
ROPE_THETA = 10000.0
GRID_W = 64

HEAD = 128
KV_TILE = 2048
WIDE_TN = 1024
EPILOGUE_ROWS = 256
LOG2E = math.log2(math.e)
GDN_QK_HEADS = D_MODEL // HEAD
GDN_V_HEADS = 2 * GDN_QK_HEADS
GDN_CONV = 5
GDN_CHUNK = 64
GDN_UNROLL = 4
GDN_INV_EXACT_LEVELS = 6
GDN_QK_W = GDN_QK_HEADS * HEAD
GDN_V_W = GDN_V_HEADS * HEAD
GDN_CONV_CH = 2 * GDN_QK_W + GDN_V_W

GQA_HEADS = D_MODEL // HEAD
GQA_KV_HEADS = GQA_HEADS // 4
GQA_QW = GQA_HEADS * HEAD
GQA_KVW = GQA_KV_HEADS * HEAD

POOL_WINDOWS = (2, 4, 8, 16)
POOL_GW = D_MODEL // len(POOL_WINDOWS)
POOL_HALO = 16

DIFF_DH = 64
DIFF_HEADS = D_MODEL // (2 * DIFF_DH)
DIFF_W = DIFF_HEADS * 2 * DIFF_DH

VMEM_LIMIT_BYTES = 56 * 1024 * 1024
LANES = 128
BF16_SUBLANES = 16
CONV_HALO = 16


def _params(sem):
    return pltpu.CompilerParams(dimension_semantics=sem, vmem_limit_bytes=VMEM_LIMIT_BYTES)


def _tile(n, pref):
    t = min(n, pref)
    while n % t or t % BF16_SUBLANES:
        t -= 1
    return t


def _silu(x):
    return x * jax.nn.sigmoid(x)


def _softplus(x):
    return jnp.maximum(x, 0.0) + jnp.log(1.0 + jnp.exp(-jnp.abs(x)))


def _dot(a, b):
    return jnp.dot(a, b, preferred_element_type=F32)


def _dot_nt(a, b):
    return lax.dot_general(a, b, (((1,), (1,)), ((), ())), preferred_element_type=F32)


def _swap_pairs(x):
    lane = lax.broadcasted_iota(jnp.int32, x.shape, 1)
    return jnp.where(lane % 2 == 0, pltpu.roll(x, LANES - 1, 1), pltpu.roll(x, 1, 1))


def _group_mean_sq(x, group):
    x2 = x * x
    if group == LANES:
        return jnp.mean(x2, axis=-1, keepdims=True)
    lo = lax.broadcasted_iota(jnp.int32, x.shape, 1) < group
    s_lo = jnp.sum(jnp.where(lo, x2, 0.0), axis=-1, keepdims=True)
    s_hi = jnp.sum(jnp.where(lo, 0.0, x2), axis=-1, keepdims=True)
    return jnp.where(lo, s_lo, s_hi) * (1.0 / group)


def _mod_kernel(c_ref, w_ref, b_ref, o_ref):
    a = _silu(c_ref[...]).astype(BF16)
    o_ref[0] = _dot(a, w_ref[0].astype(BF16)) + b_ref[0]


def _mod_call(c_all, mod_w, mod_b):
    depth, d, n = mod_w.shape
    rows = c_all.shape[0]
    tn = _tile(n, 768)
    return pl.pallas_call(
        _mod_kernel,
        grid=(depth, n // tn),
        in_specs=[
            pl.BlockSpec((rows, d), lambda l, j: (0, 0)),
            pl.BlockSpec((1, d, tn), lambda l, j: (l, 0, j)),
            pl.BlockSpec((1, 1, tn), lambda l, j: (l, 0, j)),
        ],
        out_specs=pl.BlockSpec((1, rows, tn), lambda l, j: (l, 0, j)),
        out_shape=jax.ShapeDtypeStruct((depth, rows, n), F32),
        compiler_params=_params(("arbitrary", "arbitrary")),
        name="mod_vectors",
    )(c_all, mod_w, mod_b.reshape(depth, 1, n))


def _head_epilogue(acc, o_ref, gain, cs, scale, group):
    for c in range(acc.shape[1] // LANES):
        xc = acc[:, c * LANES:(c + 1) * LANES]
        y = xc * lax.rsqrt(_group_mean_sq(xc, group) + RMS_EPS) * gain
        if cs is not None:
            y = y * cs[0] + _swap_pairs(y) * cs[1]
        if scale != 1.0:
            y = y * scale
        o_ref[:, c * LANES:(c + 1) * LANES] = y.astype(o_ref.dtype)


def _proj_kernel(*refs, kind, rope, j0, q_tiles, k_tiles, group, q_scale):
    x_ref, g_ref, sh_ref, sc_ref, w_ref = refs[:5]
    rest = list(refs[5:])
    h_ref = rest.pop()
    qn_ref = kn_ref = c_ref = s_ref = None
    if kind == "heads":
        qn_ref, kn_ref = rest.pop(0), rest.pop(0)
        if rope:
            c_ref, s_ref = rest.pop(0), rest.pop(0)
    (o_ref,) = rest
    j = pl.program_id(1)
    jg = j + j0
    tm = h_ref.shape[0]
    rs = math.gcd(tm, EPILOGUE_ROWS)

    def norm_rows(r):
        x = x_ref[r:r + rs, :]
        ms = jnp.mean(x * x, axis=-1, keepdims=True)
        y = x * lax.rsqrt(ms + RMS_EPS) * g_ref[...]
        h_ref[r:r + rs, :] = (y * (1.0 + sc_ref[0]) + sh_ref[0]).astype(BF16)

    def tile(mode, first):
        if mode == "plain" and not first:
            o_ref[...] = _dot(h_ref[...], w_ref[...]).astype(o_ref.dtype)
            return
        gain_ref, scale = (qn_ref, q_scale) if mode == "q" else (kn_ref, 1.0)
        pending = None
        for r in range(0, tm + rs, rs):
            acc = None
            if r < tm:
                if first:
                    norm_rows(r)
                acc = _dot(h_ref[r:r + rs, :], w_ref[...])
            if pending is not None:
                p0, p_acc = pending
                if mode == "plain":
                    o_ref[p0:p0 + rs, :] = p_acc.astype(o_ref.dtype)
                else:
                    cs = (c_ref[p0:p0 + rs, :], s_ref[p0:p0 + rs, :]) if rope else None
                    _head_epilogue(p_acc, o_ref.at[p0:p0 + rs, :], gain_ref[...], cs, scale, group)
            pending = (r, acc)

    def mode_of(col_tile):
        return "q" if col_tile < q_tiles else "k" if col_tile < q_tiles + k_tiles else "plain"

    @pl.when(j == 0)
    def _():
        tile(mode_of(j0), True)

    later = j > 0
    if q_tiles > j0 + 1:
        @pl.when(jnp.logical_and(later, jg < q_tiles))
        def _():
            tile("q", False)

    if k_tiles:
        @pl.when(jnp.logical_and(later, jnp.logical_and(jg >= q_tiles, jg < q_tiles + k_tiles)))
        def _():
            tile("k", False)

    @pl.when(jnp.logical_and(later, jg >= q_tiles + k_tiles))
    def _():
        tile("plain", False)


def _proj(xs, mod3, mod_base, batch_rows, g_row, w, *, kind="plain", heads=None, rope_tabs=None,
          j0=0, nj=None, tn=512, tm_pref=1024):
    m, d = xs.shape
    seq = batch_rows if batch_rows is not None else m
    tm = _tile(seq if rope_tabs is not None or batch_rows is not None else m, tm_pref)
    n_total = w.shape[1] // tn
    nj = n_total - j0 if nj is None else nj
    tiles_per_batch = (batch_rows // tm) if batch_rows is not None else None

    def mod_idx(which):
        if tiles_per_batch is None:
            return lambda i, j: (mod_base * 3 + which, 0, 0)
        return lambda i, j: ((mod_base + i // tiles_per_batch) * 3 + which, 0, 0)

    in_specs = [
        pl.BlockSpec((tm, d), lambda i, j: (i, 0)),
        pl.BlockSpec((1, d), lambda i, j: (0, 0)),
        pl.BlockSpec((1, 1, d), mod_idx(0)),
        pl.BlockSpec((1, 1, d), mod_idx(1)),
        pl.BlockSpec((d, tn), lambda i, j: (0, j + j0)),
    ]
    args = [xs, g_row, mod3, mod3, w]
    kw = dict(kind=kind, rope=rope_tabs is not None, j0=j0, q_tiles=0, k_tiles=0, group=LANES, q_scale=1.0)
    if kind == "heads":
        qn, kn, q_cols, k_cols, group, q_scale = heads
        kw.update(q_tiles=q_cols // tn, k_tiles=k_cols // tn, group=group, q_scale=q_scale)
        in_specs += [pl.BlockSpec((1, LANES), lambda i, j: (0, 0))] * 2
        args += [qn, kn]
        if rope_tabs is not None:
            tps = seq // tm
            in_specs += [pl.BlockSpec((tm, LANES), lambda i, j: (i % tps, 0))] * 2
            args += list(rope_tabs)
    return pl.pallas_call(
        functools.partial(_proj_kernel, **kw),
        grid=(m // tm, nj),
        in_specs=in_specs,
        out_specs=pl.BlockSpec((tm, tn), lambda i, j: (i, j)),
        out_shape=jax.ShapeDtypeStruct((m, nj * tn), BF16),
        scratch_shapes=[pltpu.VMEM((tm, d), BF16)],
        compiler_params=_params(("arbitrary", "arbitrary")),
        name="proj_" + kind,
    )(*args)


def _proj_gdn_kernel(x_ref, xp_ref, xn_ref, g_ref, sh_ref, sc_ref, w_ref, wab_ref, cw_ref, o_ref, ab_ref,
                     h_ref, acc_ref, *, seq):
    tm, tn = o_ref.shape
    j = pl.program_id(1)
    tps = seq // tm
    pos = pl.program_id(0) % tps
    conv_tiles = GDN_CONV_CH // tn
    qk_tiles = GDN_QK_W // tn

    def norm_mod(x):
        ms = jnp.mean(x * x, axis=-1, keepdims=True)
        y = x * lax.rsqrt(ms + RMS_EPS) * g_ref[...]
        return (y * (1.0 + sc_ref[0]) + sh_ref[0]).astype(BF16)

    @pl.when(j >= conv_tiles)
    def _():
        o_ref[...] = _dot(h_ref[CONV_HALO:CONV_HALO + tm, :], w_ref[...]).astype(BF16)

    def conv_tile(first, l2norm):
        has_prev = (pos != 0).astype(F32)
        has_next = (pos != tps - 1).astype(F32)
        rs = math.gcd(tm, EPILOGUE_ROWS)
        nb = tm // rs
        rows = tm + 2 * CONV_HALO
        base = CONV_HALO - GDN_CONV // 2
        out_scale = jnp.where(j < qk_tiles, HEAD ** -0.5, 1.0)

        def matmul_rows(b):
            r0, r1 = b * rs, ((b + 1) * rs if b < nb - 1 else rows)
            if first:
                if b == 0:
                    h_ref[0:CONV_HALO, :] = norm_mod(xp_ref[...])
                x0, x1 = max(r0 - CONV_HALO, 0), min(r1 - CONV_HALO, tm)
                h_ref[x0 + CONV_HALO:x1 + CONV_HALO, :] = norm_mod(x_ref[x0:x1, :])
                if b == nb - 1:
                    h_ref[CONV_HALO + tm:, :] = norm_mod(xn_ref[...])
            acc_ref[r0:r1, :] = _dot(h_ref[r0:r1, :], w_ref[...])
            if b == 0:
                acc_ref[0:CONV_HALO, :] = acc_ref[0:CONV_HALO, :] * has_prev
            if b == nb - 1:
                acc_ref[CONV_HALO + tm:, :] = acc_ref[CONV_HALO + tm:, :] * has_next

        def conv_rows(b):
            r0 = base + b * rs
            y = cw_ref[0:1, :] * acc_ref[r0:r0 + rs, :]
            for k in range(1, GDN_CONV):
                y = y + cw_ref[k:k + 1, :] * acc_ref[r0 + k:r0 + k + rs, :]
            y = _silu(y)
            if not l2norm:
                o_ref[b * rs:(b + 1) * rs, :] = y.astype(BF16)
                return
            for c in range(tn // HEAD):
                sl = slice(c * HEAD, (c + 1) * HEAD)
                yc = y[:, sl]
                inv = lax.rsqrt(jnp.sum(yc * yc, axis=-1, keepdims=True) + L2_EPS) * out_scale
                o_ref[b * rs:(b + 1) * rs, sl] = (yc * inv).astype(BF16)

        matmul_rows(0)
        for b in range(1, nb):
            matmul_rows(b)
            conv_rows(b - 1)
        conv_rows(nb - 1)
        if first:
            ab_ref[...] = _dot(h_ref[CONV_HALO:CONV_HALO + tm, :], wab_ref[...])

    @pl.when(j == 0)
    def _():
        conv_tile(True, True)

    @pl.when(jnp.logical_and(j > 0, j < 2 * qk_tiles))
    def _():
        conv_tile(False, True)

    @pl.when(jnp.logical_and(j >= 2 * qk_tiles, j < conv_tiles))
    def _():
        conv_tile(False, False)


def _proj_gdn(xs, mod3, mod_base, batch_rows, g_row, w, wab, conv_w, seq, *, tn=512, tm_pref=1024):
    m, d = xs.shape
    tm = _tile(seq, tm_pref)
    hb = tm // CONV_HALO
    last = m // CONV_HALO - 1
    tiles_per_batch = (batch_rows // tm) if batch_rows is not None else None
    conv_last = GDN_CONV_CH // tn - 1

    def mod_idx(which):
        if tiles_per_batch is None:
            return lambda i, j: (mod_base * 3 + which, 0, 0)
        return lambda i, j: ((mod_base + i // tiles_per_batch) * 3 + which, 0, 0)

    return pl.pallas_call(
        functools.partial(_proj_gdn_kernel, seq=seq),
        grid=(m // tm, w.shape[1] // tn),
        in_specs=[
            pl.BlockSpec((tm, d), lambda i, j: (i, 0)),
            pl.BlockSpec((CONV_HALO, d), lambda i, j: (jnp.maximum(i * hb - 1, 0), 0)),
            pl.BlockSpec((CONV_HALO, d), lambda i, j: (jnp.minimum((i + 1) * hb, last), 0)),
            pl.BlockSpec((1, d), lambda i, j: (0, 0)),
            pl.BlockSpec((1, 1, d), mod_idx(0)),
            pl.BlockSpec((1, 1, d), mod_idx(1)),
            pl.BlockSpec((d, tn), lambda i, j: (0, j)),
            pl.BlockSpec((d, LANES), lambda i, j: (0, 0)),
            pl.BlockSpec((GDN_CONV, tn), lambda i, j: (0, jnp.minimum(j, conv_last))),
        ],
        out_specs=[pl.BlockSpec((tm, tn), lambda i, j: (i, j)),
                   pl.BlockSpec((tm, LANES), lambda i, j: (i, 0))],
        out_shape=[jax.ShapeDtypeStruct((m, w.shape[1]), BF16), jax.ShapeDtypeStruct((m, LANES), F32)],
        scratch_shapes=[pltpu.VMEM((tm + 2 * CONV_HALO, d), BF16), pltpu.VMEM((tm + 2 * CONV_HALO, tn), F32)],
        compiler_params=_params(("arbitrary", "arbitrary")),
        name="proj_gdn_conv",
    )(xs, xs, xs, g_row, mod3, mod3, w, wab, conv_w)


def _out_kernel(y_ref, w_ref, x_ref, gate_ref, out_ref):
    step = math.gcd(out_ref.shape[1], 512)
    for c in range(0, out_ref.shape[1], step):
        sl = slice(c, c + step)
        out_ref[:, sl] = x_ref[:, sl] + gate_ref[0][:, sl] * _dot(y_ref[...], w_ref[:, sl])


def _out_proj(y, w, xs, mod3, mod_base, batch_rows):
    m, d = xs.shape
    kdim = w.shape[0]
    resident = 2 * w.size * w.dtype.itemsize <= VMEM_LIMIT_BYTES // 3
    tn = d if resident else 512
    tm = _tile(batch_rows if batch_rows is not None else m, 512 if resident else 1024)
    tiles_per_batch = (batch_rows // tm) if batch_rows is not None else None
    if tiles_per_batch is None:
        gate_idx = lambda i, j: (mod_base * 3 + 2, 0, j)
    else:
        gate_idx = lambda i, j: ((mod_base + i // tiles_per_batch) * 3 + 2, 0, j)
    return pl.pallas_call(
        _out_kernel,
        grid=(m // tm, d // tn),
        in_specs=[
            pl.BlockSpec((tm, kdim), lambda i, j: (i, 0)),
            pl.BlockSpec((kdim, tn), lambda i, j: (0, j)),
            pl.BlockSpec((tm, tn), lambda i, j: (i, j)),
            pl.BlockSpec((1, 1, tn), gate_idx),
        ],
        out_specs=pl.BlockSpec((tm, tn), lambda i, j: (i, j)),
        out_shape=jax.ShapeDtypeStruct((m, d), F32),
        compiler_params=_params(("arbitrary", "arbitrary")),
        name="out_proj",
    )(y, w, xs, mod3)


def _flash(queries, kv_refs):
    tq = queries[0].shape[0]
    m = [jnp.full((tq, 1), -1e30, F32) for _ in queries]
    l = [jnp.zeros((tq, 1), F32) for _ in queries]
    acc = [jnp.zeros((tq, HEAD), F32) for _ in queries]
    for k_ref, v_ref in kv_refs:
        rows = k_ref.shape[0]
        tk = math.gcd(rows, KV_TILE)
        for t in range(rows // tk):
            k = k_ref[t * tk:(t + 1) * tk, :]
            v = v_ref[t * tk:(t + 1) * tk, :]
            for i, q in enumerate(queries):
                s = _dot_nt(q, k)
                m_new = jnp.maximum(m[i], jnp.max(s, axis=-1, keepdims=True))
                alpha = jnp.exp2(m[i] - m_new)
                p = jnp.exp2(s - m_new)
                l[i] = alpha * l[i] + jnp.sum(p, axis=-1, keepdims=True)
                acc[i] = alpha * acc[i] + _dot(p.astype(BF16), v)
                m[i] = m_new
    return [a / li for a, li in zip(acc, l)]


def _gqa_attn_kernel(*refs, has_lat):
    if has_lat:
        q_ref, z_ref, kc_ref, vc_ref, kl_ref, vl_ref, y_ref = refs
        kv = [(kc_ref, vc_ref), (kl_ref, vl_ref)]
    else:
        q_ref, z_ref, kc_ref, vc_ref, y_ref = refs
        kv = [(kc_ref, vc_ref)]
    tq = q_ref.shape[0]
    heads = [slice(g * HEAD, (g + 1) * HEAD) for g in range(q_ref.shape[1] // HEAD)]
    (o,) = _flash([jnp.concatenate([q_ref[:, sl] for sl in heads], axis=0)], kv)
    for g, sl in enumerate(heads):
        y_ref[:, sl] = (o[g * tq:(g + 1) * tq] * _silu(z_ref[:, sl].astype(F32))).astype(BF16)


def _gqa_attention(p_q, p_c, p_l, batch, *, tq_pref=256):
    has_lat = p_l is not None
    m = p_q.shape[0]
    rows_q = m // batch
    tq = _tile(rows_q, tq_pref)
    nq = rows_q // tq
    lc = p_c.shape[0] // batch
    grp = GQA_QW // GQA_KV_HEADS
    k_blk, v_blk, z_blk = GQA_QW // HEAD, (GQA_QW + GQA_KVW) // HEAD, (GQA_QW + 2 * GQA_KVW) // grp
    in_specs = [
        pl.BlockSpec((tq, grp), lambda b, h, i: (b * nq + i, h)),
        pl.BlockSpec((tq, grp), lambda b, h, i: (b * nq + i, z_blk + h)),
        pl.BlockSpec((lc, HEAD), lambda b, h, i: (b, k_blk + h)),
        pl.BlockSpec((lc, HEAD), lambda b, h, i: (b, v_blk + h)),
    ]
    args = [p_q, p_q, p_c, p_c]
    if has_lat:
        ls = p_l.shape[0] // batch
        in_specs += [pl.BlockSpec((ls, HEAD), lambda b, h, i: (b, k_blk + h)),
                     pl.BlockSpec((ls, HEAD), lambda b, h, i: (b, v_blk + h))]
        args += [p_l, p_l]
    return pl.pallas_call(
        functools.partial(_gqa_attn_kernel, has_lat=has_lat),
        grid=(batch, GQA_KV_HEADS, nq),
        in_specs=in_specs,
        out_specs=pl.BlockSpec((tq, grp), lambda b, h, i: (b * nq + i, h)),
        out_shape=jax.ShapeDtypeStruct((m, GQA_QW), BF16),
        compiler_params=_params(("arbitrary", "arbitrary", "arbitrary")),
        name="gqa_attention",
    )(*args)


def _diff_attn_kernel(lq1_ref, lk1_ref, lq2_ref, lk2_ref, n_ref, q_ref, z_ref, kc_ref, vc_ref, kl_ref, vl_ref,
                      y_ref, *, lam_init):
    lam = (jnp.exp(jnp.sum(lq1_ref[...] * lk1_ref[...], axis=-1, keepdims=True))
           - jnp.exp(jnp.sum(lq2_ref[...] * lk2_ref[...], axis=-1, keepdims=True)) + lam_init)
    q = q_ref[...]
    lo = lax.broadcasted_iota(jnp.int32, q.shape, 1) < DIFF_DH
    zero = jnp.zeros_like(q)
    tq = q.shape[0]
    (o01,) = _flash([jnp.concatenate([jnp.where(lo, q, zero), jnp.where(lo, zero, q)], axis=0)],
                    [(kc_ref, vc_ref), (kl_ref, vl_ref)])
    o = o01[0:tq] - lam * o01[tq:2 * tq]
    y = o * lax.rsqrt(jnp.mean(o * o, axis=-1, keepdims=True) + RMS_EPS) * n_ref[...] * (1.0 - lam_init)
    y_ref[...] = (y * _silu(z_ref[...].astype(F32))).astype(BF16)


def _diff_attention(p_l, p_c, lams, sub_norm, batch, lam_init, *, tq_pref=512):
    m = p_l.shape[0]
    ls = m // batch
    lc = p_c.shape[0] // batch
    tq = _tile(ls, tq_pref)
    nq = ls // tq
    nh = DIFF_HEADS
    small = pl.BlockSpec((1, DIFF_DH), lambda b, h, i: (0, 0))
    in_specs = [small] * 4 + [
        pl.BlockSpec((1, HEAD), lambda b, h, i: (0, 0)),
        pl.BlockSpec((tq, HEAD), lambda b, h, i: (b * nq + i, h)),
        pl.BlockSpec((tq, HEAD), lambda b, h, i: (b * nq + i, 3 * nh + h)),
        pl.BlockSpec((lc, HEAD), lambda b, h, i: (b, h)),
        pl.BlockSpec((lc, HEAD), lambda b, h, i: (b, nh + h)),
        pl.BlockSpec((ls, HEAD), lambda b, h, i: (b, nh + h)),
        pl.BlockSpec((ls, HEAD), lambda b, h, i: (b, 2 * nh + h)),
    ]
    return pl.pallas_call(
        functools.partial(_diff_attn_kernel, lam_init=lam_init),
        grid=(batch, nh, nq),
        in_specs=in_specs,
        out_specs=pl.BlockSpec((tq, HEAD), lambda b, h, i: (b * nq + i, h)),
        out_shape=jax.ShapeDtypeStruct((m, DIFF_W), BF16),
        compiler_params=_params(("arbitrary", "arbitrary", "arbitrary")),
        name="diff_attention",
    )(*lams, sub_norm, p_l, p_l, p_c, p_c, p_l, p_l)


def _pool_kernel(u_ref, up_ref, un_ref, z_ref, wg_ref, sc_ref, y_ref, ext_ref, *, seq):
    tm = u_ref.shape[0]
    tps = seq // tm
    pos = pl.program_id(0) % tps
    has_prev = (pos != 0).astype(F32)
    has_next = (pos != tps - 1).astype(F32)
    ext_ref[0:POOL_HALO, :] = up_ref[...].astype(F32) * has_prev
    ext_ref[POOL_HALO:POOL_HALO + tm, :] = u_ref[...].astype(F32)
    ext_ref[POOL_HALO + tm:, :] = un_ref[...].astype(F32) * has_next
    t = pos * tm + lax.broadcasted_iota(jnp.int32, (tm, 1), 0)
    for gi, w in enumerate(POOL_WINDOWS):
        cols = slice(gi * POOL_GW, (gi + 1) * POOL_GW)
        tot = ext_ref[POOL_HALO - w // 2:POOL_HALO - w // 2 + tm, cols]
        for k in range(1, w):
            tot = tot + ext_ref[POOL_HALO - w // 2 + k:POOL_HALO - w // 2 + k + tm, cols]
        cnt = jnp.minimum(t - w // 2 + w, seq) - jnp.maximum(t - w // 2, 0)
        dgrp = tot / cnt.astype(F32) - ext_ref[POOL_HALO:POOL_HALO + tm, cols]
        r = _dot(dgrp.astype(BF16), wg_ref[gi]) * sc_ref[:, cols]
        y_ref[:, cols] = (r * _silu(z_ref[:, cols].astype(F32))).astype(BF16)


def _pool_mix(p, w_grp, ch_scale, seq):
    m = p.shape[0]
    w_u = D_MODEL
    tm = _tile(seq, 256)
    hb = tm // POOL_HALO
    last = m // POOL_HALO - 1
    return pl.pallas_call(
        functools.partial(_pool_kernel, seq=seq),
        grid=(m // tm,),
        in_specs=[
            pl.BlockSpec((tm, w_u), lambda i: (i, 0)),
            pl.BlockSpec((POOL_HALO, w_u), lambda i: (jnp.maximum(i * hb - 1, 0), 0)),
            pl.BlockSpec((POOL_HALO, w_u), lambda i: (jnp.minimum((i + 1) * hb, last), 0)),
            pl.BlockSpec((tm, w_u), lambda i: (i, 1)),
            pl.BlockSpec(w_grp.shape, lambda i: (0, 0, 0)),
            pl.BlockSpec((1, w_u), lambda i: (0, 0)),
        ],
        out_specs=pl.BlockSpec((tm, w_u), lambda i: (i, 0)),
        out_shape=jax.ShapeDtypeStruct((m, w_u), BF16),
        scratch_shapes=[pltpu.VMEM((tm + 2 * POOL_HALO, w_u), F32)],
        compiler_params=_params(("arbitrary",)),
        name="pool_mix",
    )(p, p, p, p, w_grp, ch_scale)


def _split3(x):
    hi = x.astype(BF16)
    r1 = x - hi.astype(F32)
    mid = r1.astype(BF16)
    lo = (r1 - mid.astype(F32)).astype(BF16)
    return hi, mid, lo


def _gdn_gates_kernel(ab_ref, alog_ref, dtb_ref, gb_ref, gt_ref):
    tm = ab_ref.shape[0]
    ab = ab_ref[...]
    lane = lax.broadcasted_iota(jnp.int32, (1, LANES), 1)
    is_g = (lane % (2 * GDN_V_HEADS)) < GDN_V_HEADS
    gval = -jnp.exp(alog_ref[...]) * _softplus(ab + dtb_ref[...])
    g = jnp.where(is_g, gval, 0.0)
    r = lax.broadcasted_iota(jnp.int32, (tm, tm), 0)
    c = lax.broadcasted_iota(jnp.int32, (tm, tm), 1)
    same = (r // GDN_CHUNK) == (c // GDN_CHUNK)
    tri_f = jnp.where(jnp.logical_and(same, r >= c), 1.0, 0.0).astype(BF16)
    tri_b = jnp.where(jnp.logical_and(same, r <= c), 1.0, 0.0).astype(BF16)
    pieces = _split3(g)
    gc_f = functools.reduce(jnp.add, [_dot(tri_f, x) for x in pieces])
    gc_b = functools.reduce(jnp.add, [_dot(tri_b, x) for x in pieces])
    gc = jnp.where(lane < 2 * GDN_V_HEADS, gc_f, gc_b)
    gb_ref[...] = jnp.where(is_g, gc, jax.nn.sigmoid(ab))
    for ci in range(tm // GDN_CHUNK):
        tile = gc[ci * GDN_CHUNK:(ci + 1) * GDN_CHUNK, :]
        tt = jnp.concatenate([tile, tile], axis=0).T
        gt_ref[ci, 0:GDN_V_HEADS, :] = tt[0:GDN_V_HEADS, :]
        gt_ref[ci, GDN_V_HEADS:, :] = tt[2 * GDN_V_HEADS:3 * GDN_V_HEADS, :]


def _gdn_gates(ab, alog_row, dtb_row, seq):
    m = ab.shape[0]
    tm = _tile(seq, 512)
    return pl.pallas_call(
        _gdn_gates_kernel,
        grid=(m // tm,),
        in_specs=[
            pl.BlockSpec((tm, LANES), lambda i: (i, 0)),
            pl.BlockSpec((1, LANES), lambda i: (0, 0)),
            pl.BlockSpec((1, LANES), lambda i: (0, 0)),
        ],
        out_specs=[
            pl.BlockSpec((tm, LANES), lambda i: (i, 0)),
            pl.BlockSpec((tm // GDN_CHUNK, 2 * GDN_V_HEADS, LANES), lambda i: (i, 0, 0)),
        ],
        out_shape=[
            jax.ShapeDtypeStruct((m, LANES), F32),
            jax.ShapeDtypeStruct((m // GDN_CHUNK, 2 * GDN_V_HEADS, LANES), F32),
        ],
        compiler_params=_params(("arbitrary",)),
        name="gdn_gates",
    )(ab, alog_row, dtb_row)


def _gdn_core_kernel(qc_ref, kc_ref, vc_ref, gbc_ref, gtc_ref, zc_ref, ql_ref, kl_ref, vl_ref, gbl_ref, gtl_ref,
                     zl_ref, n_ref, yc_ref, yl_ref, oc_ref, ol_ref, s_ref, wq_ref, u_ref, qkm_ref, kt_ref,
                     ksc_ref, gtot_ref):
    C = GDN_CHUNK
    jh = pl.program_id(1)
    s_ref[...] = jnp.zeros_like(s_ref)
    oc_ref[...] = jnp.zeros_like(oc_ref)
    ol_ref[...] = jnp.zeros_like(ol_ref)

    lane_w = lax.broadcasted_iota(jnp.int32, (C, 2 * LANES), 1)
    row_w = lax.broadcasted_iota(jnp.int32, (C, 2 * LANES), 0)
    col_w = lane_w % C
    fwd_w = lane_w < LANES
    ahead_w = jnp.where(fwd_w, row_w - col_w, col_w - row_w)
    incl_w = ahead_w >= 0
    strict_w = ahead_w > 0
    prob_w = lane_w // C
    prob_masks = [jnp.where(prob_w == p, 1.0, 0.0).astype(BF16) for p in range(4)]
    lane_p = lax.broadcasted_iota(jnp.int32, (C, LANES), 1)
    row_p = lax.broadcasted_iota(jnp.int32, (C, LANES), 0)
    first_p = lane_p < C
    lane_s = lax.broadcasted_iota(jnp.int32, (1, 2 * LANES), 1)
    lo_s = lane_s < LANES

    def block_diag(a):
        return jnp.concatenate([a * mk for mk in prob_masks], axis=0)

    def chunk_rows(ci):
        return ci * C if isinstance(ci, int) else pl.multiple_of(ci * C, C)

    def pair_blocks(a):
        zero = jnp.zeros((a.shape[0], LANES), a.dtype)
        return jnp.concatenate([jnp.concatenate([a[:, :LANES], zero], axis=1),
                                jnp.concatenate([zero, a[:, LANES:]], axis=1)], axis=0)

    def split2(a):
        hi = a.astype(BF16)
        return hi, (a - hi.astype(F32)).astype(BF16)

    def precompute(src, steps):
        q_ref, k_ref, v_ref, gb_ref, gt_ref = src
        nchunks = q_ref.shape[0] // C
        todo = range(len(steps))
        st = []
        for n, _ in steps:
            per_dir = []
            for d in range(2):
                ci = n if d == 0 else nchunks - 1 - n
                r0 = chunk_rows(ci)
                q = q_ref[pl.ds(r0, C), :]
                k = k_ref[pl.ds(r0, C), :]
                v = v_ref[pl.ds(r0, C), :].astype(F32)
                gb = gb_ref[pl.ds(r0, C), :]
                gt = gt_ref[ci]
                g_lane = d * 2 * GDN_V_HEADS + 2 * jh
                cols = []
                for lane0 in (g_lane, g_lane + GDN_V_HEADS):
                    e0 = jnp.sum(jnp.where(lane_p == lane0, gb, 0.0), axis=1, keepdims=True)
                    e1 = jnp.sum(jnp.where(lane_p == lane0 + 1, gb, 0.0), axis=1, keepdims=True)
                    cols.append((e0, e1))
                (gc0, gc1), (be0, be1) = cols
                g_row = d * GDN_V_HEADS + 2 * jh
                pick = row_p == jnp.where(first_p, g_row, g_row + 1)
                gc_row = jnp.sum(jnp.where(pick, gt, 0.0), axis=0, keepdims=True)
                kq = _dot_nt(jnp.concatenate([k, q], axis=0), jnp.concatenate([k, k], axis=0))
                last = C - 1 if d == 0 else 0
                per_dir.append(dict(q=q.astype(F32), k=k.astype(F32), v=v, gc=(gc0, gc1), be=(be0, be1),
                                    gc_col=jnp.where(first_p, gc0, gc1), be_col=jnp.where(first_p, be0, be1),
                                    gc_row=gc_row, kk=kq[0:C], qk=kq[C:2 * C],
                                    tot=(gc0[last:last + 1], gc1[last:last + 1])))
            st.append(per_dir)
        yield

        ms, qkms = [], []
        for per_dir in st:
            wide = lambda key: jnp.concatenate([per_dir[0][key], per_dir[1][key]], axis=1)
            dm = wide("gc_col") - wide("gc_row")
            decay = jnp.where(incl_w, jnp.exp(jnp.where(incl_w, dm, 0.0)), 0.0)
            ms.append(-jnp.where(strict_w, wide("be_col") * wide("kk") * decay, 0.0))
            qkms.append(wide("qk") * decay)

        xs = list(ms)
        for lvl in range(6):
            lhs = [ms[i] if lvl == 0 else xs[i] if lvl == 5 else jnp.concatenate([ms[i], xs[i]], axis=0)
                   for i in todo]
            if lvl < GDN_INV_EXACT_LEVELS:
                lhs = [split2(a) for a in lhs]
                rhs = [tuple(block_diag(piece) for piece in split2(ms[i])) for i in todo]
                prod = [_dot(jnp.concatenate([lhs[i][0], lhs[i][1], lhs[i][0]], axis=1),
                             jnp.concatenate([rhs[i][0], rhs[i][0], rhs[i][1]], axis=0)) for i in todo]
            else:
                prod = [_dot(lhs[i].astype(BF16), block_diag(ms[i].astype(BF16))) for i in todo]
            yield
            for i in todo:
                if lvl == 0:
                    ms[i] = prod[i]
                elif lvl < 5:
                    xs[i] = xs[i] + ms[i] + prod[i][C:2 * C]
                    ms[i] = prod[i][0:C]
                else:
                    xs[i] = xs[i] + ms[i] + prod[i]

        rhs_all = []
        for per_dir in st:
            rhs = []
            for d in range(2):
                pd = per_dir[d]
                for e in range(2):
                    be, egc = pd["be"][e], jnp.exp(pd["gc"][e])
                    rhs.append(jnp.concatenate([be * pd["v"][:, e * HEAD:(e + 1) * HEAD], (be * egc) * pd["k"]],
                                               axis=1))
            rhs_all.append(jnp.concatenate(rhs, axis=0))
        xr = [_dot(block_diag(xs[i].astype(BF16)), rhs_all[i].astype(BF16)) for i in todo]
        yield

        for i in todo:
            uw = rhs_all[i] + xr[i]
            slot = steps[i][1]
            for d in range(2):
                pd = st[i][d]
                u = jnp.concatenate([uw[(2 * d + e) * C:(2 * d + e + 1) * C, 0:HEAD] for e in range(2)], axis=1)
                w = jnp.concatenate([uw[(2 * d + e) * C:(2 * d + e + 1) * C, HEAD:] for e in range(2)], axis=1)
                qdec = jnp.concatenate([pd["q"] * jnp.exp(pd["gc"][e]) for e in range(2)], axis=1)
                tot0, tot1 = pd["tot"]
                kscale = jnp.concatenate([jnp.broadcast_to(jnp.exp(tot0 - pd["gc"][0]), (C, HEAD)),
                                          jnp.broadcast_to(jnp.exp(tot1 - pd["gc"][1]), (C, HEAD))], axis=1)
                wq_ref[slot, d] = jnp.concatenate([w, qdec], axis=0).astype(BF16)
                u_ref[slot, d] = u
                qkm_ref[slot, d] = qkms[i][:, d * LANES:(d + 1) * LANES].astype(BF16)
                kt_ref[slot, d] = pd["k"].T.astype(BF16)
                ksc_ref[slot, d] = kscale
                gtot_ref[slot, d] = jnp.where(lo_s, jnp.exp(tot0), jnp.exp(tot1))
        yield

    def recur(o_ref, steps):
        nchunks = o_ref.shape[0] // C
        for n, slot in steps:
            state = [s_ref[d] for d in range(2)]
            ws_qs = [_dot(wq_ref[slot, d], pair_blocks(state[d].astype(BF16))) for d in range(2)]
            yield
            v_new = [u_ref[slot, d] - ws_qs[d][0:C] for d in range(2)]
            for d in range(2):
                ci = n if d == 0 else nchunks - 1 - n
                o = ws_qs[d][C:2 * C] + _dot(qkm_ref[slot, d], pair_blocks(v_new[d].astype(BF16)))
                o_ref[pl.ds(chunk_rows(ci), C), :] += o
                ds = _dot(kt_ref[slot, d], (v_new[d] * ksc_ref[slot, d]).astype(BF16))
                s_ref[d] = state[d] * gtot_ref[slot, d] + ds
            yield

    def interleave(*gens):
        gens = list(gens)
        while gens:
            for g in list(gens):
                if next(g, gens) is gens:
                    gens.remove(g)

    def phase(q_ref, k_ref, v_ref, gb_ref, gt_ref, o_ref):
        nchunks = q_ref.shape[0] // C
        unroll = math.gcd(nchunks, GDN_UNROLL)
        groups = nchunks // unroll
        src = (q_ref, k_ref, v_ref, gb_ref, gt_ref)
        interleave(precompute(src, [(t, t) for t in range(unroll)]))

        def body(g, carry):
            base = (g % 2) * unroll
            nxt = unroll - base
            ahead = [(jnp.minimum((g + 1) * unroll + t, nchunks - 1), nxt + t) for t in range(unroll)]
            interleave(recur(o_ref, [(g * unroll + t, base + t) for t in range(unroll)]), precompute(src, ahead))
            return carry

        if groups > 1:
            lax.fori_loop(0, groups, body, 0)
        else:
            interleave(recur(o_ref, [(t, t) for t in range(unroll)]))

    phase(qc_ref, kc_ref, vc_ref, gbc_ref, gtc_ref, oc_ref)
    phase(ql_ref, kl_ref, vl_ref, gbl_ref, gtl_ref, ol_ref)

    def finish(o_ref, z_ref, y_ref):
        rows = o_ref.shape[0]
        blk = math.gcd(rows, 512)
        gain = n_ref[...]

        def body(i, carry):
            r0 = pl.multiple_of(i * blk, blk)
            for e in range(2):
                sl = slice(e * HEAD, (e + 1) * HEAD)
                o = o_ref[pl.ds(r0, blk), sl]
                y = o * lax.rsqrt(jnp.mean(o * o, axis=-1, keepdims=True) + RMS_EPS) * gain
                y_ref[pl.ds(r0, blk), sl] = (y * _silu(z_ref[pl.ds(r0, blk), sl].astype(F32))).astype(BF16)
            return carry

        lax.fori_loop(0, rows // blk, body, 0)

    finish(oc_ref, zc_ref, yc_ref)
    finish(ol_ref, zl_ref, yl_ref)


def _gdn_core(qkv_c, gb_c, gt_c, qkv_l, gb_l, gt_l, norm_row, batch):
    lc = qkv_c.shape[0] // batch
    ls = qkv_l.shape[0] // batch
    nqk = GDN_QK_HEADS
    pair = 2 * HEAD
    slots = 2 * GDN_UNROLL

    def specs(rows):
        return [
            pl.BlockSpec((rows, HEAD), lambda b, j: (b, j)),
            pl.BlockSpec((rows, HEAD), lambda b, j: (b, nqk + j)),
            pl.BlockSpec((rows, pair), lambda b, j: (b, nqk + j)),
            pl.BlockSpec((rows, LANES), lambda b, j: (b, 0)),
            pl.BlockSpec((rows // GDN_CHUNK, 2 * GDN_V_HEADS, LANES), lambda b, j: (b, 0, 0)),
            pl.BlockSpec((rows, pair), lambda b, j: (b, GDN_CONV_CH // pair + j)),
        ]

    return pl.pallas_call(
        _gdn_core_kernel,
        grid=(batch, nqk),
        in_specs=specs(lc) + specs(ls) + [pl.BlockSpec((1, LANES), lambda b, j: (0, 0))],
        out_specs=[pl.BlockSpec((lc, pair), lambda b, j: (b, j)),
                   pl.BlockSpec((ls, pair), lambda b, j: (b, j))],
        out_shape=[jax.ShapeDtypeStruct((batch * lc, GDN_V_W), BF16),
                   jax.ShapeDtypeStruct((batch * ls, GDN_V_W), BF16)],
        scratch_shapes=[
            pltpu.VMEM((lc, pair), F32),
            pltpu.VMEM((ls, pair), F32),
            pltpu.VMEM((2, HEAD, pair), F32),
            pltpu.VMEM((slots, 2, 2 * GDN_CHUNK, pair), BF16),
            pltpu.VMEM((slots, 2, GDN_CHUNK, pair), F32),
            pltpu.VMEM((slots, 2, GDN_CHUNK, HEAD), BF16),
            pltpu.VMEM((slots, 2, HEAD, GDN_CHUNK), BF16),
            pltpu.VMEM((slots, 2, GDN_CHUNK, pair), F32),
            pltpu.VMEM((slots, 2, 1, pair), F32),
        ],
        compiler_params=_params(("arbitrary", "arbitrary")),
        name="gdn_core",
    )(qkv_c, qkv_c, qkv_c, gb_c, gt_c, qkv_c, qkv_l, qkv_l, qkv_l, gb_l, gt_l, qkv_l, norm_row)


def _rope_tables(n_tokens, head_dim):
    rows = n_tokens // GRID_W
    r = jnp.repeat(jnp.arange(rows, dtype=F32), GRID_W)
    col = jnp.tile(jnp.arange(GRID_W, dtype=F32), rows)
    d_axis = head_dim // 2
    inv = ROPE_THETA ** (-jnp.arange(0, d_axis, 2, dtype=F32) / d_axis)
    ang = jnp.concatenate([r[:, None] * inv, col[:, None] * inv], axis=-1)
    cos = jnp.repeat(jnp.cos(ang), 2, axis=-1)
    sin = jnp.repeat(jnp.sin(ang), 2, axis=-1) * jnp.tile(jnp.array([-1.0, 1.0], F32), head_dim // 2)
    reps = LANES // head_dim
    return jnp.tile(cos, (1, reps)), jnp.tile(sin, (1, reps))


def _lane_row(v, reps=1):
    return jnp.tile(v.astype(F32), reps).reshape(1, -1)


def kernel(x, c, ctx, c_ctx, norm_g, mod_w, mod_b, gdn_w_in, gdn_conv_w, gdn_a_log, gdn_dt_bias, gdn_out_norm, gdn_w_out, gqa_w_in, gqa_q_norm, gqa_k_norm, gqa_w_out, pool_w_in, pool_w_grp, pool_scale, pool_w_out, diff_w_in, diff_q_norm, diff_k_norm, diff_lambda_q1, diff_lambda_k1, diff_lambda_q2, diff_lambda_k2, diff_sub_norm, diff_w_out):
    batch, seq, d = x.shape
    lctx = ctx.shape[1]
    depth = norm_g.shape[0]
    lat = x.reshape(batch * seq, d)
    cx = ctx.reshape(batch * lctx, d)

    mod_rows = -(-(batch + 1) // 8) * 8
    c_all = jnp.concatenate([c, c_ctx[None, :], jnp.zeros((mod_rows - batch - 1, d), F32)], axis=0)
    mod3 = _mod_call(c_all, mod_w, mod_b).reshape(depth * mod_rows * 3, 1, d)

    rope_gqa = _rope_tables(seq, HEAD)
    rope_diff = _rope_tables(seq, DIFF_DH)

    for i in range(depth):
        m, jl = i % N_MIXERS, i // N_MIXERS
        need_ctx = i < depth - 1
        base_l, base_c = i * mod_rows, i * mod_rows + batch
        g_row = norm_g[i].reshape(1, d)
        lat_args = (lat, mod3, base_l, seq, g_row)
        ctx_args = (cx, mod3, base_c, None, g_row)
        if m == 0:
            w_in = gdn_w_in[jl]
            w_main = w_in[:, :GDN_CONV_CH + GDN_V_W].astype(BF16)
            w_ab = w_in[:, GDN_CONV_CH + GDN_V_W:].astype(BF16)
            conv_w = gdn_conv_w[jl]
            zeros = jnp.zeros((GDN_V_HEADS,), F32)
            alog_row = jnp.concatenate([gdn_a_log[jl, 0], zeros, gdn_a_log[jl, 1], zeros]).reshape(1, LANES)
            dtb_row = jnp.concatenate([gdn_dt_bias[jl, 0], zeros, gdn_dt_bias[jl, 1], zeros]).reshape(1, LANES)
            p_l, ab_l = _proj_gdn(*lat_args, w_main, w_ab, conv_w, seq, tn=WIDE_TN)
            p_c, ab_c = _proj_gdn(*ctx_args, w_main, w_ab, conv_w, lctx, tn=WIDE_TN)
            gb_l, gt_l = _gdn_gates(ab_l, alog_row, dtb_row, seq)
            gb_c, gt_c = _gdn_gates(ab_c, alog_row, dtb_row, lctx)
            y_c, y_l = _gdn_core(p_c, gb_c, gt_c, p_l, gb_l, gt_l, _lane_row(gdn_out_norm[jl]), batch)
            w_out = gdn_w_out[jl].astype(BF16)
            lat = _out_proj(y_l, w_out, lat, mod3, base_l, seq)
            if need_ctx:
                cx = _out_proj(y_c, w_out, cx, mod3, base_c, None)
        elif m == 1:
            w_in = gqa_w_in[jl].astype(BF16)
            heads = (_lane_row(gqa_q_norm[jl]), _lane_row(gqa_k_norm[jl]), GQA_QW, GQA_KVW, HEAD,
                     HEAD ** -0.5 * LOG2E)
            p_l = _proj(*lat_args, w_in, kind="heads", heads=heads, rope_tabs=rope_gqa)
            p_c = _proj(*ctx_args, w_in, kind="heads", heads=heads)
            w_out = gqa_w_out[jl].astype(BF16)
            y_l = _gqa_attention(p_l, p_c, p_l, batch)
            lat = _out_proj(y_l, w_out, lat, mod3, base_l, seq)
            if need_ctx:
                y_c = _gqa_attention(p_c, p_c, None, batch)
                cx = _out_proj(y_c, w_out, cx, mod3, base_c, None)
        elif m == 2:
            w_in = pool_w_in[jl].astype(BF16)
            w_grp = pool_w_grp[jl].astype(BF16)
            sc_row = pool_scale[jl].reshape(1, d)
            w_out = pool_w_out[jl].astype(BF16)
            y_l = _pool_mix(_proj(*lat_args, w_in, tn=WIDE_TN), w_grp, sc_row, seq)
            lat = _out_proj(y_l, w_out, lat, mod3, base_l, seq)
            if need_ctx:
                y_c = _pool_mix(_proj(*ctx_args, w_in), w_grp, sc_row, lctx)
                cx = _out_proj(y_c, w_out, cx, mod3, base_c, None)
        else:
            lam_init = 0.8 - 0.6 * math.exp(-0.3 * i)
            w_in = diff_w_in[jl].astype(BF16)
            heads = (_lane_row(diff_q_norm[jl], 2), _lane_row(diff_k_norm[jl], 2), DIFF_W, DIFF_W, DIFF_DH,
                     DIFF_DH ** -0.5 * LOG2E)
            p_l = _proj(*lat_args, w_in, kind="heads", heads=heads, rope_tabs=rope_diff, tn=WIDE_TN)
            p_c = _proj(*ctx_args, w_in, kind="heads", heads=heads, j0=DIFF_W // WIDE_TN, nj=2 * DIFF_W // WIDE_TN,
                        tn=WIDE_TN)
            lams = [v[jl].reshape(1, DIFF_DH) for v in (diff_lambda_q1, diff_lambda_k1, diff_lambda_q2, diff_lambda_k2)]
            y_l = _diff_attention(p_l, p_c, lams, _lane_row(diff_sub_norm[jl]), batch, lam_init)
            lat = _out_proj(y_l, diff_w_out[jl].astype(BF16), lat, mod3, base_l, seq)
            assert not need_ctx, "a differential layer that still feeds a context stream is not supported"
    return lat.reshape(batch, seq, d)
```

```python
import functools
import math

import jax
import jax.numpy as jnp
from jax import lax
from jax.experimental import pallas as pl
from jax.experimental.pallas import tpu as pltpu

F32 = jnp.float32
BF16 = jnp.bfloat16

D_MODEL = 2048
N_MIXERS = 4
RMS_EPS = 1e-6
L2_EPS = 1e-6
ROPE_THETA = 10000.0
GRID_W = 64

HEAD = 128
KV_TILE = 2048
WIDE_TN = 1024
EPILOGUE_ROWS = 256
LOG2E = math.log2(math.e)
GDN_QK_HEADS = D_MODEL // HEAD
GDN_V_HEADS = 2 * GDN_QK_HEADS
GDN_CONV = 5
GDN_CHUNK = 64
GDN_UNROLL = 4
GDN_INV_EXACT_LEVELS = 6
GDN_QK_W = GDN_QK_HEADS * HEAD
GDN_V_W = GDN_V_HEADS * HEAD
GDN_CONV_CH = 2 * GDN_QK_W + GDN_V_W

GQA_HEADS = D_MODEL // HEAD
GQA_KV_HEADS = GQA_HEADS // 4
GQA_QW = GQA_HEADS * HEAD
GQA_KVW = GQA_KV_HEADS * HEAD

POOL_WINDOWS = (2, 4, 8, 16)
POOL_GW = D_MODEL // len(POOL_WINDOWS)
POOL_HALO = 16

DIFF_DH = 64
DIFF_HEADS = D_MODEL // (2 * DIFF_DH)
DIFF_W = DIFF_HEADS * 2 * DIFF_DH

VMEM_LIMIT_BYTES = 56 * 1024 * 1024
LANES = 128
BF16_SUBLANES = 16
CONV_HALO = 16


def _params(sem):
    return pltpu.CompilerParams(dimension_semantics=sem, vmem_limit_bytes=VMEM_LIMIT_BYTES)


def _tile(n, pref):
    t = min(n, pref)
    while n % t or t % BF16_SUBLANES:
        t -= 1
    return t


def _silu(x):
    return x * jax.nn.sigmoid(x)


def _softplus(x):
    return jnp.maximum(x, 0.0) + jnp.log(1.0 + jnp.exp(-jnp.abs(x)))


def _dot(a, b):
    return jnp.dot(a, b, preferred_element_type=F32)


def _dot_nt(a, b):
    return lax.dot_general(a, b, (((1,), (1,)), ((), ())), preferred_element_type=F32)


def _swap_pairs(x):
    lane = lax.broadcasted_iota(jnp.int32, x.shape, 1)
    return jnp.where(lane % 2 == 0, pltpu.roll(x, LANES - 1, 1), pltpu.roll(x, 1, 1))


def _group_mean_sq(x, group):
    x2 = x * x
    if group == LANES:
        return jnp.mean(x2, axis=-1, keepdims=True)
    lo = lax.broadcasted_iota(jnp.int32, x.shape, 1) < group
    s_lo = jnp.sum(jnp.where(lo, x2, 0.0), axis=-1, keepdims=True)
    s_hi = jnp.sum(jnp.where(lo, 0.0, x2), axis=-1, keepdims=True)
    return jnp.where(lo, s_lo, s_hi) * (1.0 / group)


def _mod_kernel(c_ref, w_ref, b_ref, o_ref):
    a = _silu(c_ref[...]).astype(BF16)
    o_ref[0] = _dot(a, w_ref[0].astype(BF16)) + b_ref[0]


def _mod_call(c_all, mod_w, mod_b):
    depth, d, n = mod_w.shape
    rows = c_all.shape[0]
    tn = _tile(n, 768)
    return pl.pallas_call(
        _mod_kernel,
        grid=(depth, n // tn),
        in_specs=[
            pl.BlockSpec((rows, d), lambda l, j: (0, 0)),
            pl.BlockSpec((1, d, tn), lambda l, j: (l, 0, j)),
            pl.BlockSpec((1, 1, tn), lambda l, j: (l, 0, j)),
        ],
        out_specs=pl.BlockSpec((1, rows, tn), lambda l, j: (l, 0, j)),
        out_shape=jax.ShapeDtypeStruct((depth, rows, n), F32),
        compiler_params=_params(("arbitrary", "arbitrary")),
        name="mod_vectors",
    )(c_all, mod_w, mod_b.reshape(depth, 1, n))


def _head_epilogue(acc, o_ref, gain, cs, scale, group):
    for c in range(acc.shape[1] // LANES):
        xc = acc[:, c * LANES:(c + 1) * LANES]
        y = xc * lax.rsqrt(_group_mean_sq(xc, group) + RMS_EPS) * gain
        if cs is not None:
            y = y * cs[0] + _swap_pairs(y) * cs[1]
        if scale != 1.0:
            y = y * scale
        o_ref[:, c * LANES:(c + 1) * LANES] = y.astype(o_ref.dtype)


def _proj_kernel(*refs, kind, rope, j0, q_tiles, k_tiles, group, q_scale):
    x_ref, g_ref, sh_ref, sc_ref, w_ref = refs[:5]
    rest = list(refs[5:])
    h_ref = rest.pop()
    qn_ref = kn_ref = c_ref = s_ref = None
    if kind == "heads":
        qn_ref, kn_ref = rest.pop(0), rest.pop(0)
        if rope:
            c_ref, s_ref = rest.pop(0), rest.pop(0)
    (o_ref,) = rest
    j = pl.program_id(1)
    jg = j + j0
    tm = h_ref.shape[0]
    rs = math.gcd(tm, EPILOGUE_ROWS)

    def norm_rows(r):
        x = x_ref[r:r + rs, :]
        ms = jnp.mean(x * x, axis=-1, keepdims=True)
        y = x * lax.rsqrt(ms + RMS_EPS) * g_ref[...]
        h_ref[r:r + rs, :] = (y * (1.0 + sc_ref[0]) + sh_ref[0]).astype(BF16)

    def tile(mode, first):
        if mode == "plain" and not first:
            o_ref[...] = _dot(h_ref[...], w_ref[...]).astype(o_ref.dtype)
            return
        gain_ref, scale = (qn_ref, q_scale) if mode == "q" else (kn_ref, 1.0)
        pending = None
        for r in range(0, tm + rs, rs):
            acc = None
            if r < tm:
                if first:
                    norm_rows(r)
                acc = _dot(h_ref[r:r + rs, :], w_ref[...])
            if pending is not None:
                p0, p_acc = pending
                if mode == "plain":
                    o_ref[p0:p0 + rs, :] = p_acc.astype(o_ref.dtype)
                else:
                    cs = (c_ref[p0:p0 + rs, :], s_ref[p0:p0 + rs, :]) if rope else None
                    _head_epilogue(p_acc, o_ref.at[p0:p0 + rs, :], gain_ref[...], cs, scale, group)
            pending = (r, acc)

    def mode_of(col_tile):
        return "q" if col_tile < q_tiles else "k" if col_tile < q_tiles + k_tiles else "plain"

    @pl.when(j == 0)
    def _():
        tile(mode_of(j0), True)

    later = j > 0
    if q_tiles > j0 + 1:
        @pl.when(jnp.logical_and(later, jg < q_tiles))
        def _():
            tile("q", False)

    if k_tiles:
        @pl.when(jnp.logical_and(later, jnp.logical_and(jg >= q_tiles, jg < q_tiles + k_tiles)))
        def _():
            tile("k", False)

    @pl.when(jnp.logical_and(later, jg >= q_tiles + k_tiles))
    def _():
        tile("plain", False)


def _proj(xs, mod3, mod_base, batch_rows, g_row, w, *, kind="plain", heads=None, rope_tabs=None,
          j0=0, nj=None, tn=512, tm_pref=1024):
    m, d = xs.shape
    seq = batch_rows if batch_rows is not None else m
    tm = _tile(seq if rope_tabs is not None or batch_rows is not None else m, tm_pref)
    n_total = w.shape[1] // tn
    nj = n_total - j0 if nj is None else nj
    tiles_per_batch = (batch_rows // tm) if batch_rows is not None else None

    def mod_idx(which):
        if tiles_per_batch is None:
            return lambda i, j: (mod_base * 3 + which, 0, 0)
        return lambda i, j: ((mod_base + i // tiles_per_batch) * 3 + which, 0, 0)

    in_specs = [
        pl.BlockSpec((tm, d), lambda i, j: (i, 0)),
        pl.BlockSpec((1, d), lambda i, j: (0, 0)),
        pl.BlockSpec((1, 1, d), mod_idx(0)),
        pl.BlockSpec((1, 1, d), mod_idx(1)),
        pl.BlockSpec((d, tn), lambda i, j: (0, j + j0)),
    ]
    args = [xs, g_row, mod3, mod3, w]
    kw = dict(kind=kind, rope=rope_tabs is not None, j0=j0, q_tiles=0, k_tiles=0, group=LANES, q_scale=1.0)
    if kind == "heads":
        qn, kn, q_cols, k_cols, group, q_scale = heads
        kw.update(q_tiles=q_cols // tn, k_tiles=k_cols // tn, group=group, q_scale=q_scale)
        in_specs += [pl.BlockSpec((1, LANES), lambda i, j: (0, 0))] * 2
        args += [qn, kn]
        if rope_tabs is not None:
            tps = seq // tm
            in_specs += [pl.BlockSpec((tm, LANES), lambda i, j: (i % tps, 0))] * 2
            args += list(rope_tabs)
    return pl.pallas_call(
        functools.partial(_proj_kernel, **kw),
        grid=(m // tm, nj),
        in_specs=in_specs,
        out_specs=pl.BlockSpec((tm, tn), lambda i, j: (i, j)),
        out_shape=jax.ShapeDtypeStruct((m, nj * tn), BF16),
        scratch_shapes=[pltpu.VMEM((tm, d), BF16)],
        compiler_params=_params(("arbitrary", "arbitrary")),
        name="proj_" + kind,
    )(*args)


def _proj_gdn_kernel(x_ref, xp_ref, xn_ref, g_ref, sh_ref, sc_ref, w_ref, wab_ref, cw_ref, o_ref, ab_ref,
                     h_ref, acc_ref, *, seq):
    tm, tn = o_ref.shape
    j = pl.program_id(1)
    tps = seq // tm
    pos = pl.program_id(0) % tps
    conv_tiles = GDN_CONV_CH // tn
    qk_tiles = GDN_QK_W // tn

    def norm_mod(x):
        ms = jnp.mean(x * x, axis=-1, keepdims=True)
        y = x * lax.rsqrt(ms + RMS_EPS) * g_ref[...]
        return (y * (1.0 + sc_ref[0]) + sh_ref[0]).astype(BF16)

    @pl.when(j >= conv_tiles)
    def _():
        o_ref[...] = _dot(h_ref[CONV_HALO:CONV_HALO + tm, :], w_ref[...]).astype(BF16)

    def conv_tile(first, l2norm):
        has_prev = (pos != 0).astype(F32)
        has_next = (pos != tps - 1).astype(F32)
        rs = math.gcd(tm, EPILOGUE_ROWS)
        nb = tm // rs
        rows = tm + 2 * CONV_HALO
        base = CONV_HALO - GDN_CONV // 2
        out_scale = jnp.where(j < qk_tiles, HEAD ** -0.5, 1.0)

        def matmul_rows(b):
            r0, r1 = b * rs, ((b + 1) * rs if b < nb - 1 else rows)
            if first:
                if b == 0:
                    h_ref[0:CONV_HALO, :] = norm_mod(xp_ref[...])
                x0, x1 = max(r0 - CONV_HALO, 0), min(r1 - CONV_HALO, tm)
                h_ref[x0 + CONV_HALO:x1 + CONV_HALO, :] = norm_mod(x_ref[x0:x1, :])
                if b == nb - 1:
                    h_ref[CONV_HALO + tm:, :] = norm_mod(xn_ref[...])
            acc_ref[r0:r1, :] = _dot(h_ref[r0:r1, :], w_ref[...])
            if b == 0:
                acc_ref[0:CONV_HALO, :] = acc_ref[0:CONV_HALO, :] * has_prev
            if b == nb - 1:
                acc_ref[CONV_HALO + tm:, :] = acc_ref[CONV_HALO + tm:, :] * has_next

        def conv_rows(b):
            win = acc_ref[b * rs + 8:b * rs + rs + 24, :]
            span = rs + 16
            y = None
            for k in range(GDN_CONV):
                off = base - 8 + k
                tap = pltpu.roll(win, span - off, 0)[0:rs, :]
                y = cw_ref[k:k + 1, :] * tap if y is None else y + cw_ref[k:k + 1, :] * tap
            y = _silu(y)
            if not l2norm:
                o_ref[b * rs:(b + 1) * rs, :] = y.astype(BF16)
                return
            for c in range(tn // HEAD):
                sl = slice(c * HEAD, (c + 1) * HEAD)
                yc = y[:, sl]
                inv = lax.rsqrt(jnp.sum(yc * yc, axis=-1, keepdims=True) + L2_EPS) * out_scale
                o_ref[b * rs:(b + 1) * rs, sl] = (yc * inv).astype(BF16)

        matmul_rows(0)
        for b in range(1, nb):
            matmul_rows(b)
            conv_rows(b - 1)
        conv_rows(nb - 1)
        if first:
            ab_ref[...] = _dot(h_ref[CONV_HALO:CONV_HALO + tm, :], wab_ref[...])

    @pl.when(j == 0)
    def _():
        conv_tile(True, True)

    @pl.when(jnp.logical_and(j > 0, j < 2 * qk_tiles))
    def _():
        conv_tile(False, True)

    @pl.when(jnp.logical_and(j >= 2 * qk_tiles, j < conv_tiles))
    def _():
        conv_tile(False, False)


def _proj_gdn(xs, mod3, mod_base, batch_rows, g_row, w, wab, conv_w, seq, *, tn=512, tm_pref=1024):
    m, d = xs.shape
    tm = _tile(seq, tm_pref)
    hb = tm // CONV_HALO
    last = m // CONV_HALO - 1
    tiles_per_batch = (batch_rows // tm) if batch_rows is not None else None
    conv_last = GDN_CONV_CH // tn - 1

    def mod_idx(which):
        if tiles_per_batch is None:
            return lambda i, j: (mod_base * 3 + which, 0, 0)
        return lambda i, j: ((mod_base + i // tiles_per_batch) * 3 + which, 0, 0)

    return pl.pallas_call(
        functools.partial(_proj_gdn_kernel, seq=seq),
        grid=(m // tm, w.shape[1] // tn),
        in_specs=[
            pl.BlockSpec((tm, d), lambda i, j: (i, 0)),
            pl.BlockSpec((CONV_HALO, d), lambda i, j: (jnp.maximum(i * hb - 1, 0), 0)),
            pl.BlockSpec((CONV_HALO, d), lambda i, j: (jnp.minimum((i + 1) * hb, last), 0)),
            pl.BlockSpec((1, d), lambda i, j: (0, 0)),
            pl.BlockSpec((1, 1, d), mod_idx(0)),
            pl.BlockSpec((1, 1, d), mod_idx(1)),
            pl.BlockSpec((d, tn), lambda i, j: (0, j)),
            pl.BlockSpec((d, LANES), lambda i, j: (0, 0)),
            pl.BlockSpec((GDN_CONV, tn), lambda i, j: (0, jnp.minimum(j, conv_last))),
        ],
        out_specs=[pl.BlockSpec((tm, tn), lambda i, j: (i, j)),
                   pl.BlockSpec((tm, LANES), lambda i, j: (i, 0))],
        out_shape=[jax.ShapeDtypeStruct((m, w.shape[1]), BF16), jax.ShapeDtypeStruct((m, LANES), F32)],
        scratch_shapes=[pltpu.VMEM((tm + 2 * CONV_HALO, d), BF16), pltpu.VMEM((tm + 2 * CONV_HALO, tn), F32)],
        compiler_params=_params(("arbitrary", "arbitrary")),
        name="proj_gdn_conv",
    )(xs, xs, xs, g_row, mod3, mod3, w, wab, conv_w)


def _out_kernel(y_ref, w_ref, x_ref, gate_ref, out_ref):
    step = math.gcd(out_ref.shape[1], 512)
    for c in range(0, out_ref.shape[1], step):
        sl = slice(c, c + step)
        out_ref[:, sl] = x_ref[:, sl] + gate_ref[0][:, sl] * _dot(y_ref[...], w_ref[:, sl])


def _out_proj(y, w, xs, mod3, mod_base, batch_rows):
    m, d = xs.shape
    kdim = w.shape[0]
    resident = 2 * w.size * w.dtype.itemsize <= VMEM_LIMIT_BYTES // 3
    tn = d if resident else 512
    tm = _tile(batch_rows if batch_rows is not None else m, 512 if resident else 1024)
    tiles_per_batch = (batch_rows // tm) if batch_rows is not None else None
    if tiles_per_batch is None:
        gate_idx = lambda i, j: (mod_base * 3 + 2, 0, j)
    else:
        gate_idx = lambda i, j: ((mod_base + i // tiles_per_batch) * 3 + 2, 0, j)
    return pl.pallas_call(
        _out_kernel,
        grid=(m // tm, d // tn),
        in_specs=[
            pl.BlockSpec((tm, kdim), lambda i, j: (i, 0)),
            pl.BlockSpec((kdim, tn), lambda i, j: (0, j)),
            pl.BlockSpec((tm, tn), lambda i, j: (i, j)),
            pl.BlockSpec((1, 1, tn), gate_idx),
        ],
        out_specs=pl.BlockSpec((tm, tn), lambda i, j: (i, j)),
        out_shape=jax.ShapeDtypeStruct((m, d), F32),
        compiler_params=_params(("arbitrary", "arbitrary")),
        name="out_proj",
    )(y, w, xs, mod3)


def _flash(queries, kv_refs):
    tq = queries[0].shape[0]
    m = [jnp.full((tq, 1), -1e30, F32) for _ in queries]
    l = [jnp.zeros((tq, 1), F32) for _ in queries]
    acc = [jnp.zeros((tq, HEAD), F32) for _ in queries]
    for k_ref, v_ref in kv_refs:
        rows = k_ref.shape[0]
        tk = math.gcd(rows, KV_TILE)
        for t in range(rows // tk):
            k = k_ref[t * tk:(t + 1) * tk, :]
            v = v_ref[t * tk:(t + 1) * tk, :]
            for i, q in enumerate(queries):
                s = _dot_nt(q, k)
                m_new = jnp.maximum(m[i], jnp.max(s, axis=-1, keepdims=True))
                alpha = jnp.exp2(m[i] - m_new)
                p = jnp.exp2(s - m_new)
                l[i] = alpha * l[i] + jnp.sum(p, axis=-1, keepdims=True)
                acc[i] = alpha * acc[i] + _dot(p.astype(BF16), v)
                m[i] = m_new
    return [a / li for a, li in zip(acc, l)]


def _gqa_attn_kernel(*refs, has_lat):
    if has_lat:
        q_ref, z_ref, kc_ref, vc_ref, kl_ref, vl_ref, y_ref = refs
        kv = [(kc_ref, vc_ref), (kl_ref, vl_ref)]
    else:
        q_ref, z_ref, kc_ref, vc_ref, y_ref = refs
        kv = [(kc_ref, vc_ref)]
    tq = q_ref.shape[0]
    heads = [slice(g * HEAD, (g + 1) * HEAD) for g in range(q_ref.shape[1] // HEAD)]
    (o,) = _flash([jnp.concatenate([q_ref[:, sl] for sl in heads], axis=0)], kv)
    for g, sl in enumerate(heads):
        y_ref[:, sl] = (o[g * tq:(g + 1) * tq] * _silu(z_ref[:, sl].astype(F32))).astype(BF16)


def _gqa_attention(p_q, p_c, p_l, batch, *, tq_pref=256):
    has_lat = p_l is not None
    m = p_q.shape[0]
    rows_q = m // batch
    tq = _tile(rows_q, tq_pref)
    nq = rows_q // tq
    lc = p_c.shape[0] // batch
    grp = GQA_QW // GQA_KV_HEADS
    k_blk, v_blk, z_blk = GQA_QW // HEAD, (GQA_QW + GQA_KVW) // HEAD, (GQA_QW + 2 * GQA_KVW) // grp
    in_specs = [
        pl.BlockSpec((tq, grp), lambda b, h, i: (b * nq + i, h)),
        pl.BlockSpec((tq, grp), lambda b, h, i: (b * nq + i, z_blk + h)),
        pl.BlockSpec((lc, HEAD), lambda b, h, i: (b, k_blk + h)),
        pl.BlockSpec((lc, HEAD), lambda b, h, i: (b, v_blk + h)),
    ]
    args = [p_q, p_q, p_c, p_c]
    if has_lat:
        ls = p_l.shape[0] // batch
        in_specs += [pl.BlockSpec((ls, HEAD), lambda b, h, i: (b, k_blk + h)),
                     pl.BlockSpec((ls, HEAD), lambda b, h, i: (b, v_blk + h))]
        args += [p_l, p_l]
    return pl.pallas_call(
        functools.partial(_gqa_attn_kernel, has_lat=has_lat),
        grid=(batch, GQA_KV_HEADS, nq),
        in_specs=in_specs,
        out_specs=pl.BlockSpec((tq, grp), lambda b, h, i: (b * nq + i, h)),
        out_shape=jax.ShapeDtypeStruct((m, GQA_QW), BF16),
        compiler_params=_params(("arbitrary", "arbitrary", "arbitrary")),
        name="gqa_attention",
    )(*args)


def _diff_attn_kernel(lq1_ref, lk1_ref, lq2_ref, lk2_ref, n_ref, q_ref, z_ref, kc_ref, vc_ref, kl_ref, vl_ref,
                      y_ref, *, lam_init):
    lam = (jnp.exp(jnp.sum(lq1_ref[...] * lk1_ref[...], axis=-1, keepdims=True))
           - jnp.exp(jnp.sum(lq2_ref[...] * lk2_ref[...], axis=-1, keepdims=True)) + lam_init)
    q = q_ref[...]
    lo = lax.broadcasted_iota(jnp.int32, q.shape, 1) < DIFF_DH
    zero = jnp.zeros_like(q)
    tq = q.shape[0]
    (o01,) = _flash([jnp.concatenate([jnp.where(lo, q, zero), jnp.where(lo, zero, q)], axis=0)],
                    [(kc_ref, vc_ref), (kl_ref, vl_ref)])
    o = o01[0:tq] - lam * o01[tq:2 * tq]
    y = o * lax.rsqrt(jnp.mean(o * o, axis=-1, keepdims=True) + RMS_EPS) * n_ref[...] * (1.0 - lam_init)
    y_ref[...] = (y * _silu(z_ref[...].astype(F32))).astype(BF16)


def _diff_attention(p_l, p_c, lams, sub_norm, batch, lam_init, *, tq_pref=512):
    m = p_l.shape[0]
    ls = m // batch
    lc = p_c.shape[0] // batch
    tq = _tile(ls, tq_pref)
    nq = ls // tq
    nh = DIFF_HEADS
    small = pl.BlockSpec((1, DIFF_DH), lambda b, h, i: (0, 0))
    in_specs = [small] * 4 + [
        pl.BlockSpec((1, HEAD), lambda b, h, i: (0, 0)),
        pl.BlockSpec((tq, HEAD), lambda b, h, i: (b * nq + i, h)),
        pl.BlockSpec((tq, HEAD), lambda b, h, i: (b * nq + i, 3 * nh + h)),
        pl.BlockSpec((lc, HEAD), lambda b, h, i: (b, h)),
        pl.BlockSpec((lc, HEAD), lambda b, h, i: (b, nh + h)),
        pl.BlockSpec((ls, HEAD), lambda b, h, i: (b, nh + h)),
        pl.BlockSpec((ls, HEAD), lambda b, h, i: (b, 2 * nh + h)),
    ]
    return pl.pallas_call(
        functools.partial(_diff_attn_kernel, lam_init=lam_init),
        grid=(batch, nh, nq),
        in_specs=in_specs,
        out_specs=pl.BlockSpec((tq, HEAD), lambda b, h, i: (b * nq + i, h)),
        out_shape=jax.ShapeDtypeStruct((m, DIFF_W), BF16),
        compiler_params=_params(("arbitrary", "arbitrary", "arbitrary")),
        name="diff_attention",
    )(*lams, sub_norm, p_l, p_l, p_c, p_c, p_l, p_l)


def _pool_kernel(u_ref, up_ref, un_ref, z_ref, wg_ref, sc_ref, y_ref, ext_ref, *, seq):
    tm = u_ref.shape[0]
    tps = seq // tm
    pos = pl.program_id(0) % tps
    has_prev = (pos != 0).astype(F32)
    has_next = (pos != tps - 1).astype(F32)
    ext_ref[0:POOL_HALO, :] = up_ref[...].astype(F32) * has_prev
    ext_ref[POOL_HALO:POOL_HALO + tm, :] = u_ref[...].astype(F32)
    ext_ref[POOL_HALO + tm:, :] = un_ref[...].astype(F32) * has_next
    t = pos * tm + lax.broadcasted_iota(jnp.int32, (tm, 1), 0)
    for gi, w in enumerate(POOL_WINDOWS):
        cols = slice(gi * POOL_GW, (gi + 1) * POOL_GW)
        tot = ext_ref[POOL_HALO - w // 2:POOL_HALO - w // 2 + tm, cols]
        for k in range(1, w):
            tot = tot + ext_ref[POOL_HALO - w // 2 + k:POOL_HALO - w // 2 + k + tm, cols]
        cnt = jnp.minimum(t - w // 2 + w, seq) - jnp.maximum(t - w // 2, 0)
        dgrp = tot / cnt.astype(F32) - ext_ref[POOL_HALO:POOL_HALO + tm, cols]
        r = _dot(dgrp.astype(BF16), wg_ref[gi]) * sc_ref[:, cols]
        y_ref[:, cols] = (r * _silu(z_ref[:, cols].astype(F32))).astype(BF16)


def _pool_mix(p, w_grp, ch_scale, seq):
    m = p.shape[0]
    w_u = D_MODEL
    tm = _tile(seq, 256)
    hb = tm // POOL_HALO
    last = m // POOL_HALO - 1
    return pl.pallas_call(
        functools.partial(_pool_kernel, seq=seq),
        grid=(m // tm,),
        in_specs=[
            pl.BlockSpec((tm, w_u), lambda i: (i, 0)),
            pl.BlockSpec((POOL_HALO, w_u), lambda i: (jnp.maximum(i * hb - 1, 0), 0)),
            pl.BlockSpec((POOL_HALO, w_u), lambda i: (jnp.minimum((i + 1) * hb, last), 0)),
            pl.BlockSpec((tm, w_u), lambda i: (i, 1)),
            pl.BlockSpec(w_grp.shape, lambda i: (0, 0, 0)),
            pl.BlockSpec((1, w_u), lambda i: (0, 0)),
        ],
        out_specs=pl.BlockSpec((tm, w_u), lambda i: (i, 0)),
        out_shape=jax.ShapeDtypeStruct((m, w_u), BF16),
        scratch_shapes=[pltpu.VMEM((tm + 2 * POOL_HALO, w_u), F32)],
        compiler_params=_params(("arbitrary",)),
        name="pool_mix",
    )(p, p, p, p, w_grp, ch_scale)


def _split3(x):
    hi = x.astype(BF16)
    r1 = x - hi.astype(F32)
    mid = r1.astype(BF16)
    lo = (r1 - mid.astype(F32)).astype(BF16)
    return hi, mid, lo


def _gdn_gates_kernel(ab_ref, alog_ref, dtb_ref, gb_ref, gt_ref):
    tm = ab_ref.shape[0]
    ab = ab_ref[...]
    lane = lax.broadcasted_iota(jnp.int32, (1, LANES), 1)
    is_g = (lane % (2 * GDN_V_HEADS)) < GDN_V_HEADS
    gval = -jnp.exp(alog_ref[...]) * _softplus(ab + dtb_ref[...])
    g = jnp.where(is_g, gval, 0.0)
    r = lax.broadcasted_iota(jnp.int32, (tm, tm), 0)
    c = lax.broadcasted_iota(jnp.int32, (tm, tm), 1)
    same = (r // GDN_CHUNK) == (c // GDN_CHUNK)
    tri_f = jnp.where(jnp.logical_and(same, r >= c), 1.0, 0.0).astype(BF16)
    tri_b = jnp.where(jnp.logical_and(same, r <= c), 1.0, 0.0).astype(BF16)
    pieces = _split3(g)
    gc_f = functools.reduce(jnp.add, [_dot(tri_f, x) for x in pieces])
    gc_b = functools.reduce(jnp.add, [_dot(tri_b, x) for x in pieces])
    gc = jnp.where(lane < 2 * GDN_V_HEADS, gc_f, gc_b)
    gb_ref[...] = jnp.where(is_g, gc, jax.nn.sigmoid(ab))
    for ci in range(tm // GDN_CHUNK):
        tile = gc[ci * GDN_CHUNK:(ci + 1) * GDN_CHUNK, :]
        tt = jnp.concatenate([tile, tile], axis=0).T
        gt_ref[ci, 0:GDN_V_HEADS, :] = tt[0:GDN_V_HEADS, :]
        gt_ref[ci, GDN_V_HEADS:, :] = tt[2 * GDN_V_HEADS:3 * GDN_V_HEADS, :]


def _gdn_gates(ab, alog_row, dtb_row, seq):
    m = ab.shape[0]
    tm = _tile(seq, 512)
    return pl.pallas_call(
        _gdn_gates_kernel,
        grid=(m // tm,),
        in_specs=[
            pl.BlockSpec((tm, LANES), lambda i: (i, 0)),
            pl.BlockSpec((1, LANES), lambda i: (0, 0)),
            pl.BlockSpec((1, LANES), lambda i: (0, 0)),
        ],
        out_specs=[
            pl.BlockSpec((tm, LANES), lambda i: (i, 0)),
            pl.BlockSpec((tm // GDN_CHUNK, 2 * GDN_V_HEADS, LANES), lambda i: (i, 0, 0)),
        ],
        out_shape=[
            jax.ShapeDtypeStruct((m, LANES), F32),
            jax.ShapeDtypeStruct((m // GDN_CHUNK, 2 * GDN_V_HEADS, LANES), F32),
        ],
        compiler_params=_params(("arbitrary",)),
        name="gdn_gates",
    )(ab, alog_row, dtb_row)


def _gdn_core_kernel(qc_ref, kc_ref, vc_ref, gbc_ref, gtc_ref, zc_ref, ql_ref, kl_ref, vl_ref, gbl_ref, gtl_ref,
                     zl_ref, n_ref, yc_ref, yl_ref, oc_ref, ol_ref, s_ref, wq_ref, u_ref, qkm_ref, kt_ref,
                     ksc_ref, gtot_ref):
    C = GDN_CHUNK
    jh = pl.program_id(1)
    s_ref[...] = jnp.zeros_like(s_ref)
    oc_ref[...] = jnp.zeros_like(oc_ref)
    ol_ref[...] = jnp.zeros_like(ol_ref)

    lane_w = lax.broadcasted_iota(jnp.int32, (C, 2 * LANES), 1)
    row_w = lax.broadcasted_iota(jnp.int32, (C, 2 * LANES), 0)
    col_w = lane_w % C
    fwd_w = lane_w < LANES
    ahead_w = jnp.where(fwd_w, row_w - col_w, col_w - row_w)
    incl_w = ahead_w >= 0
    strict_w = ahead_w > 0
    prob_w = lane_w // C
    prob_masks = [jnp.where(prob_w == p, 1.0, 0.0).astype(BF16) for p in range(4)]
    lane_p = lax.broadcasted_iota(jnp.int32, (C, LANES), 1)
    row_p = lax.broadcasted_iota(jnp.int32, (C, LANES), 0)
    first_p = lane_p < C
    lane_s = lax.broadcasted_iota(jnp.int32, (1, 2 * LANES), 1)
    lo_s = lane_s < LANES

    def block_diag(a):
        return jnp.concatenate([a * mk for mk in prob_masks], axis=0)

    def chunk_rows(ci):
        return ci * C if isinstance(ci, int) else pl.multiple_of(ci * C, C)

    def pair_blocks(a):
        zero = jnp.zeros((a.shape[0], LANES), a.dtype)
        return jnp.concatenate([jnp.concatenate([a[:, :LANES], zero], axis=1),
                                jnp.concatenate([zero, a[:, LANES:]], axis=1)], axis=0)

    def split2(a):
        hi = a.astype(BF16)
        return hi, (a - hi.astype(F32)).astype(BF16)

    def precompute(src, steps):
        q_ref, k_ref, v_ref, gb_ref, gt_ref = src
        nchunks = q_ref.shape[0] // C
        todo = range(len(steps))
        st = []
        for n, _ in steps:
            per_dir = []
            for d in range(2):
                ci = n if d == 0 else nchunks - 1 - n
                r0 = chunk_rows(ci)
                q = q_ref[pl.ds(r0, C), :]
                k = k_ref[pl.ds(r0, C), :]
                v = v_ref[pl.ds(r0, C), :].astype(F32)
                gb = gb_ref[pl.ds(r0, C), :]
                gt = gt_ref[ci]
                g_lane = d * 2 * GDN_V_HEADS + 2 * jh
                cols = []
                for lane0 in (g_lane, g_lane + GDN_V_HEADS):
                    e0 = jnp.sum(jnp.where(lane_p == lane0, gb, 0.0), axis=1, keepdims=True)
                    e1 = jnp.sum(jnp.where(lane_p == lane0 + 1, gb, 0.0), axis=1, keepdims=True)
                    cols.append((e0, e1))
                (gc0, gc1), (be0, be1) = cols
                g_row = d * GDN_V_HEADS + 2 * jh
                pick = row_p == jnp.where(first_p, g_row, g_row + 1)
                gc_row = jnp.sum(jnp.where(pick, gt, 0.0), axis=0, keepdims=True)
                kq = _dot_nt(jnp.concatenate([k, q], axis=0), jnp.concatenate([k, k], axis=0))
                last = C - 1 if d == 0 else 0
                per_dir.append(dict(q=q.astype(F32), k=k.astype(F32), v=v, gc=(gc0, gc1), be=(be0, be1),
                                    gc_col=jnp.where(first_p, gc0, gc1), be_col=jnp.where(first_p, be0, be1),
                                    gc_row=gc_row, kk=kq[0:C], qk=kq[C:2 * C],
                                    tot=(gc0[last:last + 1], gc1[last:last + 1])))
            st.append(per_dir)
        yield

        ms, qkms = [], []
        for per_dir in st:
            wide = lambda key: jnp.concatenate([per_dir[0][key], per_dir[1][key]], axis=1)
            dm = wide("gc_col") - wide("gc_row")
            decay = jnp.where(incl_w, jnp.exp(jnp.where(incl_w, dm, 0.0)), 0.0)
            ms.append(-jnp.where(strict_w, wide("be_col") * wide("kk") * decay, 0.0))
            qkms.append(wide("qk") * decay)

        xs = list(ms)
        for lvl in range(6):
            lhs = [ms[i] if lvl == 0 else xs[i] if lvl == 5 else jnp.concatenate([ms[i], xs[i]], axis=0)
                   for i in todo]
            if lvl < GDN_INV_EXACT_LEVELS:
                lhs = [split2(a) for a in lhs]
                rhs = [tuple(block_diag(piece) for piece in split2(ms[i])) for i in todo]
                prod = [_dot(jnp.concatenate([lhs[i][0], lhs[i][1], lhs[i][0]], axis=1),
                             jnp.concatenate([rhs[i][0], rhs[i][0], rhs[i][1]], axis=0)) for i in todo]
            else:
                prod = [_dot(lhs[i].astype(BF16), block_diag(ms[i].astype(BF16))) for i in todo]
            yield
            for i in todo:
                if lvl == 0:
                    ms[i] = prod[i]
                elif lvl < 5:
                    xs[i] = xs[i] + ms[i] + prod[i][C:2 * C]
                    ms[i] = prod[i][0:C]
                else:
                    xs[i] = xs[i] + ms[i] + prod[i]

        rhs_all = []
        for per_dir in st:
            rhs = []
            for d in range(2):
                pd = per_dir[d]
                for e in range(2):
                    be, egc = pd["be"][e], jnp.exp(pd["gc"][e])
                    rhs.append(jnp.concatenate([be * pd["v"][:, e * HEAD:(e + 1) * HEAD], (be * egc) * pd["k"]],
                                               axis=1))
            rhs_all.append(jnp.concatenate(rhs, axis=0))
        xr = [_dot(block_diag(xs[i].astype(BF16)), rhs_all[i].astype(BF16)) for i in todo]
        yield

        for i in todo:
            uw = rhs_all[i] + xr[i]
            slot = steps[i][1]
            for d in range(2):
                pd = st[i][d]
                u = jnp.concatenate([uw[(2 * d + e) * C:(2 * d + e + 1) * C, 0:HEAD] for e in range(2)], axis=1)
                w = jnp.concatenate([uw[(2 * d + e) * C:(2 * d + e + 1) * C, HEAD:] for e in range(2)], axis=1)
                qdec = jnp.concatenate([pd["q"] * jnp.exp(pd["gc"][e]) for e in range(2)], axis=1)
                tot0, tot1 = pd["tot"]
                kscale = jnp.concatenate([jnp.broadcast_to(jnp.exp(tot0 - pd["gc"][0]), (C, HEAD)),
                                          jnp.broadcast_to(jnp.exp(tot1 - pd["gc"][1]), (C, HEAD))], axis=1)
                wq_ref[slot, d] = jnp.concatenate([w, qdec], axis=0).astype(BF16)
                u_ref[slot, d] = u
                qkm_ref[slot, d] = qkms[i][:, d * LANES:(d + 1) * LANES].astype(BF16)
                kt_ref[slot, d] = pd["k"].T.astype(BF16)
                ksc_ref[slot, d] = kscale
                gtot_ref[slot, d] = jnp.where(lo_s, jnp.exp(tot0), jnp.exp(tot1))
        yield

    def recur(o_ref, steps):
        nchunks = o_ref.shape[0] // C
        for n, slot in steps:
            state = [s_ref[d] for d in range(2)]
            ws_qs = [_dot(wq_ref[slot, d], pair_blocks(state[d].astype(BF16))) for d in range(2)]
            yield
            v_new = [u_ref[slot, d] - ws_qs[d][0:C] for d in range(2)]
            for d in range(2):
                ci = n if d == 0 else nchunks - 1 - n
                o = ws_qs[d][C:2 * C] + _dot(qkm_ref[slot, d], pair_blocks(v_new[d].astype(BF16)))
                o_ref[pl.ds(chunk_rows(ci), C), :] += o
                ds = _dot(kt_ref[slot, d], (v_new[d] * ksc_ref[slot, d]).astype(BF16))
                s_ref[d] = state[d] * gtot_ref[slot, d] + ds
            yield

    def interleave(*gens):
        gens = list(gens)
        while gens:
            for g in list(gens):
                if next(g, gens) is gens:
                    gens.remove(g)

    def phase(q_ref, k_ref, v_ref, gb_ref, gt_ref, o_ref):
        nchunks = q_ref.shape[0] // C
        unroll = math.gcd(nchunks, GDN_UNROLL)
        groups = nchunks // unroll
        src = (q_ref, k_ref, v_ref, gb_ref, gt_ref)
        interleave(precompute(src, [(t, t) for t in range(unroll)]))

        def body(g, carry):
            base = (g % 2) * unroll
            nxt = unroll - base
            ahead = [(jnp.minimum((g + 1) * unroll + t, nchunks - 1), nxt + t) for t in range(unroll)]
            interleave(recur(o_ref, [(g * unroll + t, base + t) for t in range(unroll)]), precompute(src, ahead))
            return carry

        if groups > 1:
            lax.fori_loop(0, groups, body, 0)
        else:
            interleave(recur(o_ref, [(t, t) for t in range(unroll)]))

    phase(qc_ref, kc_ref, vc_ref, gbc_ref, gtc_ref, oc_ref)
    phase(ql_ref, kl_ref, vl_ref, gbl_ref, gtl_ref, ol_ref)

    def finish(o_ref, z_ref, y_ref):
        rows = o_ref.shape[0]
        blk = math.gcd(rows, 512)
        gain = n_ref[...]

        def body(i, carry):
            r0 = pl.multiple_of(i * blk, blk)
            for e in range(2):
                sl = slice(e * HEAD, (e + 1) * HEAD)
                o = o_ref[pl.ds(r0, blk), sl]
                y = o * lax.rsqrt(jnp.mean(o * o, axis=-1, keepdims=True) + RMS_EPS) * gain
                y_ref[pl.ds(r0, blk), sl] = (y * _silu(z_ref[pl.ds(r0, blk), sl].astype(F32))).astype(BF16)
            return carry

        lax.fori_loop(0, rows // blk, body, 0)

    finish(oc_ref, zc_ref, yc_ref)
    finish(ol_ref, zl_ref, yl_ref)


def _gdn_core(qkv_c, gb_c, gt_c, qkv_l, gb_l, gt_l, norm_row, batch):
    lc = qkv_c.shape[0] // batch
    ls = qkv_l.shape[0] // batch
    nqk = GDN_QK_HEADS
    pair = 2 * HEAD
    slots = 2 * GDN_UNROLL

    def specs(rows):
        return [
            pl.BlockSpec((rows, HEAD), lambda b, j: (b, j)),
            pl.BlockSpec((rows, HEAD), lambda b, j: (b, nqk + j)),
            pl.BlockSpec((rows, pair), lambda b, j: (b, nqk + j)),
            pl.BlockSpec((rows, LANES), lambda b, j: (b, 0)),
            pl.BlockSpec((rows // GDN_CHUNK, 2 * GDN_V_HEADS, LANES), lambda b, j: (b, 0, 0)),
            pl.BlockSpec((rows, pair), lambda b, j: (b, GDN_CONV_CH // pair + j)),
        ]

    return pl.pallas_call(
        _gdn_core_kernel,
        grid=(batch, nqk),
        in_specs=specs(lc) + specs(ls) + [pl.BlockSpec((1, LANES), lambda b, j: (0, 0))],
        out_specs=[pl.BlockSpec((lc, pair), lambda b, j: (b, j)),
                   pl.BlockSpec((ls, pair), lambda b, j: (b, j))],
        out_shape=[jax.ShapeDtypeStruct((batch * lc, GDN_V_W), BF16),
                   jax.ShapeDtypeStruct((batch * ls, GDN_V_W), BF16)],
        scratch_shapes=[
            pltpu.VMEM((lc, pair), F32),
            pltpu.VMEM((ls, pair), F32),
            pltpu.VMEM((2, HEAD, pair), F32),
            pltpu.VMEM((slots, 2, 2 * GDN_CHUNK, pair), BF16),
            pltpu.VMEM((slots, 2, GDN_CHUNK, pair), F32),
            pltpu.VMEM((slots, 2, GDN_CHUNK, HEAD), BF16),
            pltpu.VMEM((slots, 2, HEAD, GDN_CHUNK), BF16),
            pltpu.VMEM((slots, 2, GDN_CHUNK, pair), F32),
            pltpu.VMEM((slots, 2, 1, pair), F32),
        ],
        compiler_params=_params(("arbitrary", "arbitrary")),
        name="gdn_core",
    )(qkv_c, qkv_c, qkv_c, gb_c, gt_c, qkv_c, qkv_l, qkv_l, qkv_l, gb_l, gt_l, qkv_l, norm_row)


def _rope_tables(n_tokens, head_dim):
    rows = n_tokens // GRID_W
    r = jnp.repeat(jnp.arange(rows, dtype=F32), GRID_W)
    col = jnp.tile(jnp.arange(GRID_W, dtype=F32), rows)
    d_axis = head_dim // 2
    inv = ROPE_THETA ** (-jnp.arange(0, d_axis, 2, dtype=F32) / d_axis)
    ang = jnp.concatenate([r[:, None] * inv, col[:, None] * inv], axis=-1)
    cos = jnp.repeat(jnp.cos(ang), 2, axis=-1)
    sin = jnp.repeat(jnp.sin(ang), 2, axis=-1) * jnp.tile(jnp.array([-1.0, 1.0], F32), head_dim // 2)
    reps = LANES // head_dim
    return jnp.tile(cos, (1, reps)), jnp.tile(sin, (1, reps))


def _lane_row(v, reps=1):
    return jnp.tile(v.astype(F32), reps).reshape(1, -1)


def kernel(x, c, ctx, c_ctx, norm_g, mod_w, mod_b, gdn_w_in, gdn_conv_w, gdn_a_log, gdn_dt_bias, gdn_out_norm, gdn_w_out, gqa_w_in, gqa_q_norm, gqa_k_norm, gqa_w_out, pool_w_in, pool_w_grp, pool_scale, pool_w_out, diff_w_in, diff_q_norm, diff_k_norm, diff_lambda_q1, diff_lambda_k1, diff_lambda_q2, diff_lambda_k2, diff_sub_norm, diff_w_out):
    batch, seq, d = x.shape
    lctx = ctx.shape[1]
    depth = norm_g.shape[0]
    lat = x.reshape(batch * seq, d)
    cx = ctx.reshape(batch * lctx, d)

    mod_rows = -(-(batch + 1) // 8) * 8
    c_all = jnp.concatenate([c, c_ctx[None, :], jnp.zeros((mod_rows - batch - 1, d), F32)], axis=0)
    mod3 = _mod_call(c_all, mod_w, mod_b).reshape(depth * mod_rows * 3, 1, d)

    rope_gqa = _rope_tables(seq, HEAD)
    rope_diff = _rope_tables(seq, DIFF_DH)

    for i in range(depth):
        m, jl = i % N_MIXERS, i // N_MIXERS
        need_ctx = i < depth - 1
        base_l, base_c = i * mod_rows, i * mod_rows + batch
        g_row = norm_g[i].reshape(1, d)
        lat_args = (lat, mod3, base_l, seq, g_row)
        ctx_args = (cx, mod3, base_c, None, g_row)
        if m == 0:
            w_in = gdn_w_in[jl]
            w_main = w_in[:, :GDN_CONV_CH + GDN_V_W].astype(BF16)
            w_ab = w_in[:, GDN_CONV_CH + GDN_V_W:].astype(BF16)
            conv_w = gdn_conv_w[jl]
            zeros = jnp.zeros((GDN_V_HEADS,), F32)
            alog_row = jnp.concatenate([gdn_a_log[jl, 0], zeros, gdn_a_log[jl, 1], zeros]).reshape(1, LANES)
            dtb_row = jnp.concatenate([gdn_dt_bias[jl, 0], zeros, gdn_dt_bias[jl, 1], zeros]).reshape(1, LANES)
            p_l, ab_l = _proj_gdn(*lat_args, w_main, w_ab, conv_w, seq, tn=WIDE_TN)
            p_c, ab_c = _proj_gdn(*ctx_args, w_main, w_ab, conv_w, lctx, tn=WIDE_TN)
            gb_l, gt_l = _gdn_gates(ab_l, alog_row, dtb_row, seq)
            gb_c, gt_c = _gdn_gates(ab_c, alog_row, dtb_row, lctx)
            y_c, y_l = _gdn_core(p_c, gb_c, gt_c, p_l, gb_l, gt_l, _lane_row(gdn_out_norm[jl]), batch)
            w_out = gdn_w_out[jl].astype(BF16)
            lat = _out_proj(y_l, w_out, lat, mod3, base_l, seq)
            if need_ctx:
                cx = _out_proj(y_c, w_out, cx, mod3, base_c, None)
        elif m == 1:
            w_in = gqa_w_in[jl].astype(BF16)
            heads = (_lane_row(gqa_q_norm[jl]), _lane_row(gqa_k_norm[jl]), GQA_QW, GQA_KVW, HEAD,
                     HEAD ** -0.5 * LOG2E)
            p_l = _proj(*lat_args, w_in, kind="heads", heads=heads, rope_tabs=rope_gqa)
            p_c = _proj(*ctx_args, w_in, kind="heads", heads=heads)
            w_out = gqa_w_out[jl].astype(BF16)
            y_l = _gqa_attention(p_l, p_c, p_l, batch)
            lat = _out_proj(y_l, w_out, lat, mod3, base_l, seq)
            if need_ctx:
                y_c = _gqa_attention(p_c, p_c, None, batch)
                cx = _out_proj(y_c, w_out, cx, mod3, base_c, None)
        elif m == 2:
            w_in = pool_w_in[jl].astype(BF16)
            w_grp = pool_w_grp[jl].astype(BF16)
            sc_row = pool_scale[jl].reshape(1, d)
            w_out = pool_w_out[jl].astype(BF16)
            y_l = _pool_mix(_proj(*lat_args, w_in, tn=WIDE_TN), w_grp, sc_row, seq)
            lat = _out_proj(y_l, w_out, lat, mod3, base_l, seq)
            if need_ctx:
                y_c = _pool_mix(_proj(*ctx_args, w_in), w_grp, sc_row, lctx)
                cx = _out_proj(y_c, w_out, cx, mod3, base_c, None)
        else:
            lam_init = 0.8 - 0.6 * math.exp(-0.3 * i)
            w_in = diff_w_in[jl].astype(BF16)
            heads = (_lane_row(diff_q_norm[jl], 2), _lane_row(diff_k_norm[jl], 2), DIFF_W, DIFF_W, DIFF_DH,
                     DIFF_DH ** -0.5 * LOG2E)
            p_l = _proj(*lat_args, w_in, kind="heads", heads=heads, rope_tabs=rope_diff, tn=WIDE_TN)
            p_c = _proj(*ctx_args, w_in, kind="heads", heads=heads, j0=DIFF_W // WIDE_TN, nj=2 * DIFF_W // WIDE_TN,
                        tn=WIDE_TN)
            lams = [v[jl].reshape(1, DIFF_DH) for v in (diff_lambda_q1, diff_lambda_k1, diff_lambda_q2, diff_lambda_k2)]
            y_l = _diff_attention(p_l, p_c, lams, _lane_row(diff_sub_norm[jl]), batch, lam_init)
            lat = _out_proj(y_l, diff_w_out[jl].astype(BF16), lat, mod3, base_l, seq)
            assert not need_ctx, "a differential layer that still feeds a context stream is not supported"
    return lat.reshape(batch, seq, d)
```
